```python
import math
import jax, jax.numpy as jnp
from jax import lax
import numpy as np

D_MODEL = 4096
BATCH = 1
SEQ = 16384
DEPTH = 2
DEC_BATCH = 8
DEC_SEQ = 64
PAST_LEN = 1024

CHUNK = 64
N_HEADS = 4
HEAD_DIM = 256
D_ATTN = N_HEADS * HEAD_DIM
D_SSM = D_MODEL // 4
GROUP = 16
N_GROUPS = D_SSM // GROUP
STATE = 64
Q_BLOCK = 128
K_BLOCK = 128
D_FF = 6144
D_FF_E = 768
N_EXPERTS = 8
TOP_K = 2
N_DENSE = (DEPTH + 1) // 2
N_MOE = DEPTH // 2
D_IN = 3 * D_ATTN + D_SSM + 2 * D_MODEL
RMS_EPS = 1e-6
DT_MIN = 1e-3
DT_MAX = 1e-1
HI = lax.Precision.HIGHEST

kernel_name = 'stickbreak_s5_gated_hybrid_stream_step'


def rmsnorm(x, g):
    x32 = x.astype(jnp.float32)
    y = x32 * lax.rsqrt(jnp.mean(x32 * x32, axis=-1, keepdims=True) + RMS_EPS)
    return (y * g.astype(jnp.float32)).astype(x.dtype)


def rev_cumsum(x):
    kk = x.shape[-1]
    nb = -(-kk // K_BLOCK)
    pad = nb * K_BLOCK - kk
    if pad:
        x = jnp.pad(x, [(0, 0)] * (x.ndim - 1) + [(0, pad)])
    xb = x.reshape(x.shape[:-1] + (nb, K_BLOCK))
    idx = jnp.arange(K_BLOCK)
    tri = (idx[None, :] >= idx[:, None]).astype(x.dtype)
    within = jnp.einsum('...nj,ij->...ni', xb, tri, precision=HI)
    bidx = jnp.arange(nb)
    upper = (bidx[None, :] > bidx[:, None]).astype(x.dtype)
    later = jnp.einsum('...m,nm->...n', within[..., 0], upper, precision=HI)
    out = (within + later[..., None]).reshape(x.shape[:-1] + (nb * K_BLOCK,))
    return out[..., :kk]


def sb_block(q, q_pos, k, v, k_pos):
    qs = q * (HEAD_DIM ** -0.5)
    z = jnp.einsum('bqhd,bkhd->bhqk', qs, k).astype(jnp.float32)
    mask = k_pos[None, :] < q_pos[:, None]
    neg_log_keep = jnp.where(mask, jax.nn.softplus(z), 0.0)
    suffix = rev_cumsum(neg_log_keep)
    weights = jnp.where(mask, jnp.exp(jnp.minimum(z - suffix, 0.0)), 0.0)
    return jnp.einsum('bhqk,bkhd->bqhd', weights.astype(v.dtype), v)


def sb_prompt(q, k, v):
    b, t = q.shape[:2]
    pos = jnp.arange(t, dtype=jnp.int32)
    outs = []
    for i in range(t // Q_BLOCK):
        lo, hi = i * Q_BLOCK, (i + 1) * Q_BLOCK
        outs.append(sb_block(q[:, lo:hi], pos[lo:hi], k[:, :hi], v[:, :hi], pos[:hi]))
    return jnp.concatenate(outs, axis=1).reshape(b, t, D_ATTN)


def ssm_discretize(lam_re, lam_im, log_dt, b_re, b_im):
    f32 = jnp.float32
    dt = jnp.exp(log_dt.astype(f32))[:, None]
    lr, li = lam_re.astype(f32), lam_im.astype(f32)
    mag = jnp.exp(lr * dt)
    ar = mag * jnp.cos(li * dt)
    ai = mag * jnp.sin(li * dt)
    den = lr * lr + li * li
    nr = ar - 1.0
    cr = (nr * lr + ai * li) / den
    ci = (ai * lr - nr * li) / den
    br, bi = b_re.astype(f32), b_im.astype(f32)
    bbar_r = cr[..., None] * br - ci[..., None] * bi
    bbar_i = cr[..., None] * bi + ci[..., None] * br
    return ar, ai, bbar_r, bbar_i


def _linrec_combine(e1, e2):
    a1r, a1i, b1r, b1i = e1
    a2r, a2i, b2r, b2i = e2
    return (a2r * a1r - a2i * a1i,
            a2r * a1i + a2i * a1r,
            a2r * b1r - a2i * b1i + b2r,
            a2r * b1i + a2i * b1r + b2i)


def ssm_chunk(h_r, h_i, u, disc, c_re, c_im, d_skip):
    ar, ai, bbr, bbi = disc
    bu_r = jnp.einsum('bcgh,gph->bcgp', u, bbr)
    bu_i = jnp.einsum('bcgh,gph->bcgp', u, bbi)
    bu_r = bu_r.at[:, 0].add(ar * h_r - ai * h_i)
    bu_i = bu_i.at[:, 0].add(ar * h_i + ai * h_r)
    a_r = jnp.broadcast_to(ar, bu_r.shape)
    a_i = jnp.broadcast_to(ai, bu_i.shape)
    _, _, x_r, x_i = lax.associative_scan(_linrec_combine, (a_r, a_i, bu_r, bu_i), axis=1)
    y = (jnp.einsum('bcgp,ghp->bcgh', x_r, c_re) - jnp.einsum('bcgp,ghp->bcgh', x_i, c_im)
         + d_skip * u)
    return x_r[:, -1], x_i[:, -1], y


def ssm_prompt(u, disc, c_re, c_im, d_skip):
    b, t = u.shape[:2]
    nc = t // CHUNK
    us = u.reshape(b, nc, CHUNK, N_GROUPS, GROUP).transpose(1, 0, 2, 3, 4)
    h0 = jnp.zeros((b, N_GROUPS, STATE), jnp.float32)

    def step(carry, u_c):
        hr, hi, y_c = ssm_chunk(carry[0], carry[1], u_c, disc, c_re, c_im, d_skip)
        return (hr, hi), y_c

    (hr, hi), ys = lax.scan(step, (h0, h0), us)
    return hr, hi, ys.transpose(1, 0, 2, 3, 4).reshape(b, t, D_SSM)


def mixer_project(h, w_in_l):
    b, t = h.shape[:2]
    z = h @ w_in_l
    q, k, v, u, ga, gb = jnp.split(
        z, [D_ATTN, 2 * D_ATTN, 3 * D_ATTN, 3 * D_ATTN + D_SSM, 3 * D_ATTN + D_SSM + D_MODEL], axis=-1)
    heads = (b, t, N_HEADS, HEAD_DIM)
    return q.reshape(heads), k.reshape(heads), v.reshape(heads), u, ga, gb


def mixer_merge(o_a, y_ssm, ga, gb, b_gate_l, w_glu_l, w_pa_l, w_pb_l, w_o_l):
    gy = jax.nn.gelu(y_ssm.astype(o_a.dtype))
    o_b = gy * jax.nn.sigmoid(gy @ w_glu_l)
    gate_a = jax.nn.sigmoid(ga + b_gate_l[:D_MODEL])
    gate_b = jax.nn.sigmoid(gb + b_gate_l[D_MODEL:])
    merged = gate_a * (o_a @ w_pa_l) + gate_b * (o_b @ w_pb_l)
    return merged @ w_o_l


def swiglu(h, w1, w3, w2):
    return (jax.nn.silu(h @ w1) * (h @ w3)) @ w2


def moe(h, router, router_b, w1, w3, w2):
    logits = (h @ router).astype(jnp.float32) + router_b.astype(jnp.float32)
    top_v, top_i = lax.top_k(logits, TOP_K)
    gates = jax.nn.softmax(top_v, axis=-1)
    comb = jnp.sum(jax.nn.one_hot(top_i, N_EXPERTS, dtype=jnp.float32) * gates[..., None], axis=-2)
    out = jnp.zeros_like(h)
    for e in range(N_EXPERTS):
        out = out + comb[..., e:e + 1].astype(h.dtype) * swiglu(h, w1[e], w3[e], w2[e])
    return out


def channel_mix(l, h, ffn_w1, ffn_w3, ffn_w2, moe_router, moe_router_b, moe_w1, moe_w3, moe_w2):
    i = l // 2
    if l % 2 == 0:
        return swiglu(h, ffn_w1[i], ffn_w3[i], ffn_w2[i])
    return moe(h, moe_router[i], moe_router_b[i], moe_w1[i], moe_w3[i], moe_w2[i])


def _uniform(key, shape, fan_in):
    lim = math.sqrt(3.0 / fan_in)
    return jax.random.uniform(key, shape, jnp.float32, -lim, lim)


def setup_inputs(seed: int = 0) -> dict:
    key = jax.random.key(seed)
    ks = jax.random.split(key, 32)
    f32 = jnp.float32
    nrm = jax.random.normal
    lam_im0 = jnp.pi * jnp.arange(STATE, dtype=f32)
    return {
        'x_prompt': nrm(ks[0], (BATCH, SEQ, D_MODEL), f32),
        'x_sample': nrm(ks[1], (DEC_BATCH, DEC_SEQ, D_MODEL), f32),
        'cache_k': nrm(ks[2], (DEPTH, DEC_BATCH, PAST_LEN, N_HEADS, HEAD_DIM), f32),
        'cache_v': nrm(ks[3], (DEPTH, DEC_BATCH, PAST_LEN, N_HEADS, HEAD_DIM), f32),
        'state_ssm_re': 0.1 * nrm(ks[4], (DEPTH, DEC_BATCH, N_GROUPS, STATE), f32),
        'state_ssm_im': 0.1 * nrm(ks[5], (DEPTH, DEC_BATCH, N_GROUPS, STATE), f32),
        'norm1_g': 1.0 + 0.01 * nrm(ks[6], (DEPTH, D_MODEL), f32),
        'norm2_g': 1.0 + 0.01 * nrm(ks[7], (DEPTH, D_MODEL), f32),
        'w_in': _uniform(ks[8], (DEPTH, D_MODEL, D_IN), D_MODEL),
        'lam_re': -0.5 + 0.01 * nrm(ks[9], (DEPTH, N_GROUPS, STATE), f32),
        'lam_im': lam_im0 + 0.01 * nrm(ks[10], (DEPTH, N_GROUPS, STATE), f32),
        'log_dt': jax.random.uniform(ks[11], (DEPTH, N_GROUPS), f32, math.log(DT_MIN), math.log(DT_MAX)),
        'ssm_b_re': nrm(ks[12], (DEPTH, N_GROUPS, STATE, GROUP), f32) * (2 * GROUP) ** -0.5,
        'ssm_b_im': nrm(ks[13], (DEPTH, N_GROUPS, STATE, GROUP), f32) * (2 * GROUP) ** -0.5,
        'ssm_c_re': nrm(ks[14], (DEPTH, N_GROUPS, GROUP, STATE), f32) * (2 * STATE) ** -0.5,
        'ssm_c_im': nrm(ks[15], (DEPTH, N_GROUPS, GROUP, STATE), f32) * (2 * STATE) ** -0.5,
        'ssm_d': nrm(ks[16], (DEPTH, N_GROUPS, GROUP), f32),
        'w_glu': _uniform(ks[17], (DEPTH, D_SSM, D_SSM), D_SSM),
        'w_pa': _uniform(ks[18], (DEPTH, D_ATTN, D_MODEL), D_ATTN),
        'w_pb': _uniform(ks[19], (DEPTH, D_SSM, D_MODEL), D_SSM),
        'b_gate': 0.01 * nrm(ks[20], (DEPTH, 2 * D_MODEL), f32),
        'w_o': _uniform(ks[21], (DEPTH, D_MODEL, D_MODEL), D_MODEL),
        'ffn_w1': _uniform(ks[22], (N_DENSE, D_MODEL, D_FF), D_MODEL),
        'ffn_w3': _uniform(ks[23], (N_DENSE, D_MODEL, D_FF), D_MODEL),
        'ffn_w2': _uniform(ks[24], (N_DENSE, D_FF, D_MODEL), D_FF),
        'moe_router': _uniform(ks[25], (N_MOE, D_MODEL, N_EXPERTS), D_MODEL),
        'moe_router_b': 0.01 * nrm(ks[26], (N_MOE, N_EXPERTS), f32),
        'moe_w1': _uniform(ks[27], (N_MOE, N_EXPERTS, D_MODEL, D_FF_E), D_MODEL),
        'moe_w3': _uniform(ks[28], (N_MOE, N_EXPERTS, D_MODEL, D_FF_E), D_MODEL),
        'moe_w2': _uniform(ks[29], (N_MOE, N_EXPERTS, D_FF_E, D_MODEL), D_FF_E),
        'final_g': 1.0 + 0.01 * nrm(ks[30], (D_MODEL,), f32),
    }


def reference(x_prompt, x_sample, cache_k, cache_v, state_ssm_re, state_ssm_im,
              norm1_g, norm2_g, w_in, lam_re, lam_im, log_dt, ssm_b_re, ssm_b_im,
              ssm_c_re, ssm_c_im, ssm_d, w_glu, w_pa, w_pb, b_gate, w_o,
              ffn_w1, ffn_w3, ffn_w2, moe_router, moe_router_b, moe_w1, moe_w3, moe_w2,
              final_g):
    f32 = jnp.float32
    db, ds = x_sample.shape[:2]
    q_pos_s = PAST_LEN + jnp.arange(ds, dtype=jnp.int32)
    k_pos_s = jnp.arange(PAST_LEN + ds, dtype=jnp.int32)
    xp, xs = x_prompt, x_sample
    kp, vp, srp, sip, ksm, vsm, srs, sis = ([] for _ in range(8))
    for l in range(DEPTH):
        disc = ssm_discretize(lam_re[l], lam_im[l], log_dt[l], ssm_b_re[l], ssm_b_im[l])
        c_re, c_im, d_skip = ssm_c_re[l].astype(f32), ssm_c_im[l].astype(f32), ssm_d[l].astype(f32)

        h = rmsnorm(xp, norm1_g[l])
        q, k, v, u, ga, gb = mixer_project(h, w_in[l])
        o_a = sb_prompt(q, k, v)
        hr, hi, y = ssm_prompt(u.astype(f32), disc, c_re, c_im, d_skip)
        xp = xp + mixer_merge(o_a, y, ga, gb, b_gate[l], w_glu[l], w_pa[l], w_pb[l], w_o[l])
        xp = xp + channel_mix(l, rmsnorm(xp, norm2_g[l]), ffn_w1, ffn_w3, ffn_w2,
                              moe_router, moe_router_b, moe_w1, moe_w3, moe_w2)
        kp.append(k)
        vp.append(v)
        srp.append(hr)
        sip.append(hi)

        h = rmsnorm(xs, norm1_g[l])
        q, k, v, u, ga, gb = mixer_project(h, w_in[l])
        k_all = jnp.concatenate([cache_k[l].astype(k.dtype), k], axis=1)
        v_all = jnp.concatenate([cache_v[l].astype(v.dtype), v], axis=1)
        o_a = sb_block(q, q_pos_s, k_all, v_all, k_pos_s).reshape(db, ds, D_ATTN)
        u_s = u.reshape(db, ds, N_GROUPS, GROUP).astype(f32)
        hr, hi, y = ssm_chunk(state_ssm_re[l].astype(f32), state_ssm_im[l].astype(f32),
                              u_s, disc, c_re, c_im, d_skip)
        y = y.reshape(db, ds, D_SSM)
        xs = xs + mixer_merge(o_a, y, ga, gb, b_gate[l], w_glu[l], w_pa[l], w_pb[l], w_o[l])
        xs = xs + channel_mix(l, rmsnorm(xs, norm2_g[l]), ffn_w1, ffn_w3, ffn_w2,
                              moe_router, moe_router_b, moe_w1, moe_w3, moe_w2)
        ksm.append(k)
        vsm.append(v)
        srs.append(hr)
        sis.append(hi)

    y_prompt = rmsnorm(xp, final_g)
    y_sample = rmsnorm(xs, final_g)
    return (y_prompt, y_sample,
            jnp.stack(kp), jnp.stack(vp), jnp.stack(srp), jnp.stack(sip),
            jnp.stack(ksm), jnp.stack(vsm), jnp.stack(srs), jnp.stack(sis))
```

```python
import functools
import math

import jax
import jax.numpy as jnp
from jax import lax
from jax.experimental import pallas as pl
from jax.experimental.pallas import tpu as pltpu

F32 = jnp.float32
BF16 = jnp.bfloat16

RMS_EPS = 1e-6
TOP_K = 2
GROUP = 16
SUBLANES = 8
V7X_VMEM_LIMIT_BYTES = 56 * 1024 * 1024
SSM_KB_GROUPS = 16


def _cparams(*sem):
    return pltpu.CompilerParams(dimension_semantics=sem, vmem_limit_bytes=V7X_VMEM_LIMIT_BYTES)


def _tile(dim, pref):
    if dim <= pref:
        return dim
    t = pref
    while t >= 128:
        if dim % t == 0:
            return t
        t -= 128
    return dim


def _rmsnorm(x, g):
    ms = jnp.mean(x * x, axis=-1, keepdims=True)
    return (x * lax.rsqrt(ms + RMS_EPS)) * g


NORM_ROWS = 128


def _rmsnorm_rows(x_ref, g_ref, o_ref):
    rows = x_ref.shape[0]
    step = min(NORM_ROWS, rows)
    g = g_ref[...]

    def body(c, carry):
        r0 = pl.multiple_of(c * step, step)
        o_ref[pl.ds(r0, step), :] = _rmsnorm(x_ref[pl.ds(r0, step), :], g).astype(o_ref.dtype)
        return carry

    lax.fori_loop(0, rows // step, body, 0)


def _norm_proj_body(x_ref, g_ref, w_ref, q_ref, k_ref, v_ref, kvb_ref, u_ref, gate_ref, h_scr,
                    *, nq, ns, q_scale):
    n = pl.program_id(1)

    @pl.when(n == 0)
    def _():
        _rmsnorm_rows(x_ref, g_ref, h_scr)

    def z():
        return jnp.dot(h_scr[...], w_ref[...], preferred_element_type=F32)

    @pl.when(n < nq)
    def _():
        q_ref[...] = (z() * q_scale).astype(BF16)

    @pl.when((n >= nq) & (n < 2 * nq))
    def _():
        zz = z()
        k_ref[...] = zz
        kvb_ref[...] = zz.astype(BF16)

    @pl.when((n >= 2 * nq) & (n < 3 * nq))
    def _():
        zz = z()
        v_ref[...] = zz
        kvb_ref[...] = zz.astype(BF16)

    @pl.when((n >= 3 * nq) & (n < 3 * nq + ns))
    def _():
        u_ref[...] = z()

    @pl.when(n >= 3 * nq + ns)
    def _():
        gate_ref[...] = z()


def norm_proj(x, g, w, *, d_attn, d_ssm, head_dim, permute_u):
    t, d = x.shape
    d_in = w.shape[1]
    tn = _tile(d_ssm, 1024)
    assert d_attn % tn == 0 and d % tn == 0
    nq, ns = d_attn // tn, d_ssm // tn
    ng = 2 * d // tn
    assert d_in == (3 * nq + ns + ng) * tn
    tm = _tile(t // SUBLANES if permute_u else t, 512)
    if permute_u:
        seg = t // SUBLANES
        assert seg % tm == 0
        mps = seg // tm
        u_shape = (seg, SUBLANES * d_ssm)
        u_map = lambda m, n: (m % mps, (m // mps) * ns + jnp.clip(n - 3 * nq, 0, ns - 1))
    else:
        u_shape = (t, d_ssm)
        u_map = lambda m, n: (m, jnp.clip(n - 3 * nq, 0, ns - 1))
    blk = lambda imap: pl.BlockSpec((tm, tn), imap)
    outs = pl.pallas_call(
        functools.partial(_norm_proj_body, nq=nq, ns=ns, q_scale=head_dim ** -0.5),
        grid=(t // tm, d_in // tn),
        in_specs=[pl.BlockSpec((tm, d), lambda m, n: (m, 0), pipeline_mode=pl.Buffered(1)),
                  pl.BlockSpec((1, d), lambda m, n: (0, 0)),
                  pl.BlockSpec((d, tn), lambda m, n: (0, n))],
        out_specs=[blk(lambda m, n: (m, jnp.minimum(n, nq - 1))),
                   blk(lambda m, n: (m, jnp.clip(n - nq, 0, nq - 1))),
                   blk(lambda m, n: (m, jnp.clip(n - 2 * nq, 0, nq - 1))),
                   blk(lambda m, n: (m, jnp.clip(n - nq, 0, 2 * nq - 1))),
                   blk(u_map),
                   blk(lambda m, n: (m, jnp.clip(n - 3 * nq - ns, 0, ng - 1)))],
        out_shape=[jax.ShapeDtypeStruct((t, d_attn), BF16),
                   jax.ShapeDtypeStruct((t, d_attn), F32),
                   jax.ShapeDtypeStruct((t, d_attn), F32),
                   jax.ShapeDtypeStruct((t, 2 * d_attn), BF16),
                   jax.ShapeDtypeStruct(u_shape, F32),
                   jax.ShapeDtypeStruct((t, 2 * d), F32)],
        scratch_shapes=[pltpu.VMEM((tm, d), BF16)],
        compiler_params=_cparams("arbitrary", "arbitrary"),
        name="norm_proj",
    )(x, g.reshape(1, d), w)
    q, k, v, kvb, u, gates = outs
    if permute_u:
        u = u.reshape(t, d_ssm)
    return q, k, v, kvb, u, gates


def _softplus(z):
    return jnp.maximum(z, 0.0) + jnp.log(1.0 + jnp.exp(-jnp.abs(z)))


def _tri(tk):
    r = lax.broadcasted_iota(jnp.int32, (tk, tk), 0)
    c = lax.broadcasted_iota(jnp.int32, (tk, tk), 1)
    return (r >= c).astype(BF16)


def _sb_tile(q, kb, vb, tri, later, mask):
    z = lax.dot_general(q, kb, (((1,), (1,)), ((), ())), preferred_element_type=F32)
    sp = _softplus(z)
    if mask is not None:
        sp = jnp.where(mask, sp, 0.0)
    hi = sp.astype(BF16)
    lo = (sp - hi.astype(F32)).astype(BF16)
    within = (jnp.dot(hi, tri, preferred_element_type=F32)
              + jnp.dot(lo, tri, preferred_element_type=F32))
    w = jnp.exp(jnp.minimum(z - (within + later), 0.0))
    if mask is not None:
        w = jnp.where(mask, w, 0.0)
    pv = jnp.dot(w.astype(BF16), vb, preferred_element_type=F32)
    return pv, later + within[:, 0:1]


def _attn_prompt_body(q_ref, k_ref, v_ref, o_ref, acc_scr, *, tq):
    i = pl.program_id(1)
    q = q_ref[...]
    tri = _tri(tq)
    row = lax.broadcasted_iota(jnp.int32, (tq, tq), 0)
    col = lax.broadcasted_iota(jnp.int32, (tq, tq), 1)
    start = pl.multiple_of(i * tq, tq)
    pv, later = _sb_tile(q, k_ref[pl.ds(start, tq), :], v_ref[pl.ds(start, tq), :], tri,
                         jnp.zeros((tq, 1), F32), col < row)
    acc_scr[...] = pv

    def body(s, later):
        j0 = pl.multiple_of((i - 1 - s) * tq, tq)
        pv, later = _sb_tile(q, k_ref[pl.ds(j0, tq), :], v_ref[pl.ds(j0, tq), :], tri, later, None)
        acc_scr[...] += pv
        return later

    lax.fori_loop(0, i, body, later)
    o_ref[...] = acc_scr[...].astype(o_ref.dtype)


def attention_prompt(q, kvb, *, n_heads, head_dim):
    t = q.shape[0]
    tq = _tile(t, 256)
    single = pl.Buffered(1)
    return pl.pallas_call(
        functools.partial(_attn_prompt_body, tq=tq),
        grid=(n_heads, t // tq),
        in_specs=[pl.BlockSpec((tq, head_dim), lambda h, i: (i, h)),
                  pl.BlockSpec((t, head_dim), lambda h, i: (0, h), pipeline_mode=single),
                  pl.BlockSpec((t, head_dim), lambda h, i: (0, n_heads + h), pipeline_mode=single)],
        out_specs=pl.BlockSpec((tq, head_dim), lambda h, i: (i, h)),
        out_shape=jax.ShapeDtypeStruct((t, n_heads * head_dim), BF16),
        scratch_shapes=[pltpu.VMEM((tq, head_dim), F32)],
        compiler_params=_cparams("arbitrary", "arbitrary"),
        name="attn_prompt",
    )(q, kvb, kvb)


def _attn_sample_body(q_ref, kn_ref, vn_ref, ck_ref, cv_ref, o_ref, *, ds, tk):
    q = q_ref[...]
    row = lax.broadcasted_iota(jnp.int32, (ds, ds), 0)
    col = lax.broadcasted_iota(jnp.int32, (ds, ds), 1)
    acc, later = _sb_tile(q, kn_ref[...], vn_ref[...], _tri(ds), jnp.zeros((ds, 1), F32), col < row)
    tri = _tri(tk)
    past = ck_ref.shape[0]
    for j in range(past // tk - 1, -1, -1):
        kb = ck_ref[j * tk:(j + 1) * tk, :].astype(BF16)
        vb = cv_ref[j * tk:(j + 1) * tk, :].astype(BF16)
        pv, later = _sb_tile(q, kb, vb, tri, later, None)
        acc = acc + pv
    o_ref[...] = acc.astype(o_ref.dtype)


def attention_sample(q, kvb, cache_k, cache_v, *, n_heads, head_dim, ds):
    b, past, _ = cache_k.shape
    tk = _tile(past, 256)
    cache_spec = pl.BlockSpec((None, past, head_dim), lambda bi, h: (bi, 0, h))
    return pl.pallas_call(
        functools.partial(_attn_sample_body, ds=ds, tk=tk),
        grid=(b, n_heads),
        in_specs=[pl.BlockSpec((ds, head_dim), lambda bi, h: (bi, h)),
                  pl.BlockSpec((ds, head_dim), lambda bi, h: (bi, h)),
                  pl.BlockSpec((ds, head_dim), lambda bi, h: (bi, n_heads + h)),
                  cache_spec, cache_spec],
        out_specs=pl.BlockSpec((ds, head_dim), lambda bi, h: (bi, h)),
        out_shape=jax.ShapeDtypeStruct((b * ds, n_heads * head_dim), BF16),
        compiler_params=_cparams("arbitrary", "arbitrary"),
        name="attn_sample",
    )(q, kvb, kvb, cache_k, cache_v)


def _ssm_body(u_ref, bdr_ref, bdi_ref, cdr_ref, cdi_ref, ar_ref, ai_ref, d_ref, h0r_ref, h0i_ref,
              *rest, n, nkb, emit_y):
    if emit_y:
        y_ref, hr_ref, hi_ref, str_scr, sti_scr, xr_scr, xi_scr = rest
    else:
        hr_ref, hi_ref, str_scr, sti_scr, xr_scr, xi_scr = rest
    step = pl.program_id(0)
    kin = bdr_ref.shape[1]
    kst = bdr_ref.shape[2]

    @pl.when(step == 0)
    def _():
        str_scr[...] = h0r_ref[...]
        sti_scr[...] = h0i_ref[...]

    u = u_ref[...]
    ub = u.astype(BF16)
    for kb in range(nkb):
        ukb = ub[:, kb * kin:(kb + 1) * kin]
        xr_scr[:, kb * kst:(kb + 1) * kst] = jnp.dot(ukb, bdr_ref[kb], preferred_element_type=F32)
        xi_scr[:, kb * kst:(kb + 1) * kst] = jnp.dot(ukb, bdi_ref[kb], preferred_element_type=F32)

    for kb in range(nkb):
        cs = slice(kb * kst, (kb + 1) * kst)
        ar = jnp.broadcast_to(ar_ref[:, cs], (SUBLANES, kst))
        ai = jnp.broadcast_to(ai_ref[:, cs], (SUBLANES, kst))

        def scan_step(tt, carry, cs=cs, ar=ar, ai=ai):
            xr, xi = carry
            r0 = pl.multiple_of(tt * SUBLANES, SUBLANES)
            nxr = ar * xr - ai * xi + xr_scr[pl.ds(r0, SUBLANES), cs]
            nxi = ar * xi + ai * xr + xi_scr[pl.ds(r0, SUBLANES), cs]
            xr_scr[pl.ds(r0, SUBLANES), cs] = nxr
            xi_scr[pl.ds(r0, SUBLANES), cs] = nxi
            return nxr, nxi

        xr, xi = lax.fori_loop(0, n, scan_step, (str_scr[:, cs], sti_scr[:, cs]), unroll=4)
        str_scr[:, cs] = xr
        sti_scr[:, cs] = xi

    if emit_y:
        for kb in range(nkb):
            cs = slice(kb * kst, (kb + 1) * kst)
            yk = (jnp.dot(xr_scr[:, cs].astype(BF16), cdr_ref[kb], preferred_element_type=F32)
                  - jnp.dot(xi_scr[:, cs].astype(BF16), cdi_ref[kb], preferred_element_type=F32))
            os_ = slice(kb * kin, (kb + 1) * kin)
            y_ref[:, os_] = yk + d_ref[:, os_] * u[:, os_]

    @pl.when(step == pl.num_programs(0) - 1)
    def _():
        hr_ref[...] = str_scr[...]
        hi_ref[...] = sti_scr[...]


def ssm_scan(u, coef, h0r, h0i, *, emit_y):
    bdr, bdi, cdr, cdi, ar, ai, dsk = coef
    r, d_ssm = u.shape
    nkb, kin, kst = bdr.shape
    npst = nkb * kst
    rows = _tile(r, 512)
    n = rows // SUBLANES
    full = lambda a: pl.BlockSpec(a.shape, lambda s: (0,) * a.ndim)
    out_specs = [pl.BlockSpec((SUBLANES, npst), lambda s: (0, 0))] * 2
    out_shape = [jax.ShapeDtypeStruct((SUBLANES, npst), F32)] * 2
    if emit_y:
        out_specs = [pl.BlockSpec((rows, d_ssm), lambda s: (s, 0))] + out_specs
        out_shape = [jax.ShapeDtypeStruct((r, d_ssm), F32)] + out_shape
    outs = pl.pallas_call(
        functools.partial(_ssm_body, n=n, nkb=nkb, emit_y=emit_y),
        grid=(r // rows,),
        in_specs=[pl.BlockSpec((rows, d_ssm), lambda s: (s, 0)),
                  full(bdr), full(bdi), full(cdr), full(cdi), full(ar), full(ai), full(dsk),
                  full(h0r), full(h0i)],
        out_specs=out_specs,
        out_shape=out_shape,
        scratch_shapes=[pltpu.VMEM((SUBLANES, npst), F32), pltpu.VMEM((SUBLANES, npst), F32),
                        pltpu.VMEM((rows, npst), F32), pltpu.VMEM((rows, npst), F32)],
        compiler_params=_cparams("arbitrary"),
        name="ssm_scan_y" if emit_y else "ssm_scan_state",
    )(u, bdr, bdi, cdr, cdi, ar, ai, dsk, h0r, h0i)
    if emit_y:
        return outs[0], outs[1], outs[2]
    return None, outs[0], outs[1]


def _ssm_coef(lam_re, lam_im, log_dt, b_re, b_im, c_re, c_im, d_skip):
    g, p = lam_re.shape
    dt = jnp.exp(log_dt.astype(F32))[:, None]
    lr, li = lam_re.astype(F32), lam_im.astype(F32)
    mag = jnp.exp(lr * dt)
    ar = mag * jnp.cos(li * dt)
    ai = mag * jnp.sin(li * dt)
    den = lr * lr + li * li
    nr = ar - 1.0
    cr = (nr * lr + ai * li) / den
    ci = (ai * lr - nr * li) / den
    br, bi = b_re.astype(F32), b_im.astype(F32)
    bbr = cr[..., None] * br - ci[..., None] * bi
    bbi = cr[..., None] * bi + ci[..., None] * br
    gb = min(SSM_KB_GROUPS, g)
    nkb = g // gb
    eye = jnp.eye(gb, dtype=F32)

    def pack_b(m):
        m = m.reshape(nkb, gb, p, GROUP)
        return jnp.einsum('kaph,ab->kahbp', m, eye).reshape(nkb, gb * GROUP, gb * p).astype(BF16)

    def pack_c(m):
        m = m.reshape(nkb, gb, GROUP, p)
        return jnp.einsum('kahp,ab->kapbh', m, eye).reshape(nkb, gb * p, gb * GROUP).astype(BF16)

    coef = (pack_b(bbr), pack_b(bbi), pack_c(c_re.astype(F32)), pack_c(c_im.astype(F32)),
            ar.reshape(1, g * p), ai.reshape(1, g * p), d_skip.astype(F32).reshape(1, g * GROUP))
    return coef, ar, ai


def _segment_inits(ar, ai, fr, fi, seg_len):
    k = int(round(math.log2(seg_len)))
    assert 2 ** k == seg_len
    pr, pi = ar.reshape(1, -1), ai.reshape(1, -1)
    for _ in range(k):
        pr, pi = pr * pr - pi * pi, 2.0 * pr * pi
    hr = jnp.zeros_like(fr[0:1])
    hi = jnp.zeros_like(fi[0:1])
    hrs, his = [hr], [hi]
    for r in range(SUBLANES - 1):
        hr, hi = pr * hr - pi * hi + fr[r:r + 1], pr * hi + pi * hr + fi[r:r + 1]
        hrs.append(hr)
        his.append(hi)
    return jnp.concatenate(hrs, axis=0), jnp.concatenate(his, axis=0)


def _merge_body(oa_ref, y_ref, ga_ref, gb_ref, ba_ref, bb_ref, wglu_ref, wpa_ref, wpb_ref, o_ref,
                ob_scr):
    n = pl.program_id(1)

    @pl.when(n == 0)
    def _():
        gy = jax.nn.gelu(y_ref[...])
        glu = jnp.dot(gy.astype(BF16), wglu_ref[...], preferred_element_type=F32)
        ob_scr[...] = (gy * jax.nn.sigmoid(glu)).astype(BF16)

    pa = jnp.dot(oa_ref[...], wpa_ref[...], preferred_element_type=F32)
    pb = jnp.dot(ob_scr[...], wpb_ref[...], preferred_element_type=F32)
    gate_a = jax.nn.sigmoid(ga_ref[...] + ba_ref[...])
    gate_b = jax.nn.sigmoid(gb_ref[...] + bb_ref[...])
    o_ref[...] = (gate_a * pa + gate_b * pb).astype(o_ref.dtype)


def mixer_merge(o_a, y, gates, b_gate, w_glu, w_pa, w_pb, *, y_permuted):
    t, d_attn = o_a.shape
    d_ssm = w_glu.shape[0]
    d = w_pa.shape[1]
    tm = _tile(t // SUBLANES if y_permuted else t, 512)
    tn = _tile(d, 1024)
    nd = d // tn
    if y_permuted:
        seg = t // SUBLANES
        mps = seg // tm
        y = y.reshape(seg, SUBLANES * d_ssm)
        y_spec = pl.BlockSpec((tm, d_ssm), lambda m, n: (m % mps, m // mps))
    else:
        y_spec = pl.BlockSpec((tm, d_ssm), lambda m, n: (m, 0))
    bg = b_gate.reshape(1, 2 * d)
    return pl.pallas_call(
        _merge_body,
        grid=(t // tm, nd),
        in_specs=[pl.BlockSpec((tm, d_attn), lambda m, n: (m, 0)),
                  y_spec,
                  pl.BlockSpec((tm, tn), lambda m, n: (m, n)),
                  pl.BlockSpec((tm, tn), lambda m, n: (m, nd + n)),
                  pl.BlockSpec((1, tn), lambda m, n: (0, n)),
                  pl.BlockSpec((1, tn), lambda m, n: (0, nd + n)),
                  pl.BlockSpec((d_ssm, d_ssm), lambda m, n: (0, 0)),
                  pl.BlockSpec((d_attn, tn), lambda m, n: (0, n)),
                  pl.BlockSpec((d_ssm, tn), lambda m, n: (0, n))],
        out_specs=pl.BlockSpec((tm, tn), lambda m, n: (m, n)),
        out_shape=jax.ShapeDtypeStruct((t, d), BF16),
        scratch_shapes=[pltpu.VMEM((tm, d_ssm), BF16)],
        compiler_params=_cparams("arbitrary", "arbitrary"),
        name="mixer_merge",
    )(o_a, y, gates, gates, bg, bg, w_glu, w_pa, w_pb)


def _matmul_res_body(a_ref, w_ref, r_ref, o_ref):
    o_ref[...] = r_ref[...] + jnp.dot(a_ref[...], w_ref[...], preferred_element_type=F32)


def matmul_res(a, w, res):
    t, k = a.shape
    n_out = w.shape[1]
    tm = _tile(t, 1024)
    tn = _tile(n_out, 512)
    return pl.pallas_call(
        _matmul_res_body,
        grid=(t // tm, n_out // tn),
        in_specs=[pl.BlockSpec((tm, k), lambda m, n: (m, 0)),
                  pl.BlockSpec((k, tn), lambda m, n: (0, n)),
                  pl.BlockSpec((tm, tn), lambda m, n: (m, n))],
        out_specs=pl.BlockSpec((tm, tn), lambda m, n: (m, n)),
        out_shape=jax.ShapeDtypeStruct((t, n_out), F32),
        compiler_params=_cparams("arbitrary", "arbitrary"),
        name="matmul_res",
    )(a, w, res)


def _router_comb(h, router_ref, rb_ref, n_experts):
    logits = jnp.dot(h, router_ref[...], preferred_element_type=F32) + rb_ref[...]
    lane = lax.broadcasted_iota(jnp.int32, logits.shape, 1).astype(F32)
    neg = jnp.float32(-jnp.inf)
    logits = jnp.where(lane < n_experts, logits, neg)
    big = jnp.float32(logits.shape[1])
    m1 = jnp.max(logits, axis=-1, keepdims=True)
    i1 = jnp.min(jnp.where(logits == m1, lane, big), axis=-1, keepdims=True)
    rest = jnp.where(lane == i1, neg, logits)
    m2 = jnp.max(rest, axis=-1, keepdims=True)
    i2 = jnp.min(jnp.where(rest == m2, lane, big), axis=-1, keepdims=True)
    e2 = jnp.exp(m2 - m1)
    g1 = 1.0 / (1.0 + e2)
    g2 = e2 / (1.0 + e2)
    return jnp.where(lane == i1, g1, 0.0) + jnp.where(lane == i2, g2, 0.0)


def _ffn_up_body(x_ref, g_ref, w1_ref, w3_ref, *rest, n_experts, tiles_per_expert):
    if n_experts:
        router_ref, rb_ref, o_ref, h_scr, comb_scr = rest
    else:
        o_ref, h_scr = rest
    n = pl.program_id(1)

    @pl.when(n == 0)
    def _():
        _rmsnorm_rows(x_ref, g_ref, h_scr)
        if n_experts:
            comb_scr[...] = _router_comb(h_scr[...], router_ref, rb_ref, n_experts)

    h = h_scr[...]
    a = jnp.dot(h, w1_ref[...], preferred_element_type=F32)
    b = jnp.dot(h, w3_ref[...], preferred_element_type=F32)
    hid = jax.nn.silu(a) * b
    if n_experts:
        comb = comb_scr[...]
        lane = lax.broadcasted_iota(jnp.int32, comb.shape, 1)
        e = n // tiles_per_expert
        hid = hid * jnp.sum(jnp.where(lane == e, comb, 0.0), axis=-1, keepdims=True)
    o_ref[...] = hid.astype(o_ref.dtype)


def ffn_up(x, g, w1, w3, router=None, router_b=None):
    t, d = x.shape
    e_w, _, f = w1.shape
    tm = _tile(t, 1024)
    tn = _tile(f, 512)
    tpe = f // tn
    n_experts = e_w if router is not None else 0
    w_spec = pl.BlockSpec((None, d, tn), lambda m, n: (n // tpe, 0, n % tpe))
    in_specs = [pl.BlockSpec((tm, d), lambda m, n: (m, 0), pipeline_mode=pl.Buffered(1)),
                pl.BlockSpec((1, d), lambda m, n: (0, 0)), w_spec, w_spec]
    args = [x, g.reshape(1, d), w1, w3]
    scratch = [pltpu.VMEM((tm, d), BF16)]
    if n_experts:
        lanes = 128
        rpad = jnp.zeros((d, lanes), BF16).at[:, :n_experts].set(router.astype(BF16))
        bpad = jnp.zeros((1, lanes), F32).at[0, :n_experts].set(router_b.astype(F32))
        in_specs += [pl.BlockSpec((d, lanes), lambda m, n: (0, 0)),
                     pl.BlockSpec((1, lanes), lambda m, n: (0, 0))]
        args += [rpad, bpad]
        scratch.append(pltpu.VMEM((tm, lanes), F32))
    return pl.pallas_call(
        functools.partial(_ffn_up_body, n_experts=n_experts, tiles_per_expert=tpe),
        grid=(t // tm, e_w * tpe),
        in_specs=in_specs,
        out_specs=pl.BlockSpec((tm, tn), lambda m, n: (m, n)),
        out_shape=jax.ShapeDtypeStruct((t, e_w * f), BF16),
        scratch_shapes=scratch,
        compiler_params=_cparams("arbitrary", "arbitrary"),
        name="ffn_up_moe" if n_experts else "ffn_up",
    )(*args)


def _final_norm_body(x_ref, g_ref, o_ref):
    _rmsnorm_rows(x_ref, g_ref, o_ref)


def final_norm(x, g):
    t, d = x.shape
    tm = _tile(t, 512)
    return pl.pallas_call(
        _final_norm_body,
        grid=(t // tm,),
        in_specs=[pl.BlockSpec((tm, d), lambda m: (m, 0)), pl.BlockSpec((1, d), lambda m: (0, 0))],
        out_specs=pl.BlockSpec((tm, d), lambda m: (m, 0)),
        out_shape=jax.ShapeDtypeStruct((t, d), F32),
        compiler_params=_cparams("arbitrary"),
        name="final_norm",
    )(x, g.reshape(1, d))


def kernel(x_prompt, x_sample, cache_k, cache_v, state_ssm_re, state_ssm_im, norm1_g, norm2_g, w_in,
           lam_re, lam_im, log_dt, ssm_b_re, ssm_b_im, ssm_c_re, ssm_c_im, ssm_d, w_glu, w_pa, w_pb,
           b_gate, w_o, ffn_w1, ffn_w3, ffn_w2, moe_router, moe_router_b, moe_w1, moe_w3, moe_w2,
           final_g):
    bp, seq, d = x_prompt.shape
    db, ds, _ = x_sample.shape
    depth = w_in.shape[0]
    _, _, past, n_heads, head_dim = cache_k.shape
    d_attn = n_heads * head_dim
    n_groups, n_state = lam_re.shape[1:]
    d_ssm = n_groups * GROUP
    npst = n_groups * n_state
    assert bp == 1 and db == SUBLANES and seq % SUBLANES == 0

    xp = x_prompt.reshape(seq, d)
    xs = x_sample.reshape(db * ds, d)
    outs = [[] for _ in range(8)]
    for l in range(depth):
        i = l // 2
        w_in_l = w_in[l].astype(BF16)
        w_glu_l, w_pa_l, w_pb_l, w_o_l = (w[l].astype(BF16) for w in (w_glu, w_pa, w_pb, w_o))
        coef, ar, ai = _ssm_coef(lam_re[l], lam_im[l], log_dt[l], ssm_b_re[l], ssm_b_im[l],
                                 ssm_c_re[l], ssm_c_im[l], ssm_d[l])
        if l % 2 == 0:
            mix_w = (ffn_w1[i].astype(BF16)[None], ffn_w3[i].astype(BF16)[None],
                     ffn_w2[i].astype(BF16), None, None)
        else:
            f_e = moe_w2.shape[2]
            mix_w = (moe_w1[i].astype(BF16), moe_w3[i].astype(BF16),
                     moe_w2[i].astype(BF16).reshape(moe_w2.shape[1] * f_e, d),
                     moe_router[i], moe_router_b[i])

        def channel_mix(x):
            w1, w3, w2, router, router_b = mix_w
            hid = ffn_up(x, norm2_g[l], w1, w3, router, router_b)
            return matmul_res(hid, w2, x)

        q, k, v, kvb, u, gates = norm_proj(xp, norm1_g[l], w_in_l, d_attn=d_attn, d_ssm=d_ssm,
                                           head_dim=head_dim, permute_u=True)
        o_a = attention_prompt(q, kvb, n_heads=n_heads, head_dim=head_dim)
        zeros = jnp.zeros((SUBLANES, npst), F32)
        _, fr, fi = ssm_scan(u, coef, zeros, zeros, emit_y=False)
        h0r, h0i = _segment_inits(ar, ai, fr, fi, seq // SUBLANES)
        y, hr, hi = ssm_scan(u, coef, h0r, h0i, emit_y=True)
        merged = mixer_merge(o_a, y, gates, b_gate[l], w_glu_l, w_pa_l, w_pb_l, y_permuted=True)
        xp = matmul_res(merged, w_o_l, xp)
        xp = channel_mix(xp)
        outs[0].append(k.reshape(1, seq, n_heads, head_dim))
        outs[1].append(v.reshape(1, seq, n_heads, head_dim))
        outs[2].append(hr[SUBLANES - 1].reshape(1, n_groups, n_state))
        outs[3].append(hi[SUBLANES - 1].reshape(1, n_groups, n_state))

        q, k, v, kvb, u, gates = norm_proj(xs, norm1_g[l], w_in_l, d_attn=d_attn, d_ssm=d_ssm,
                                           head_dim=head_dim, permute_u=False)
        o_a = attention_sample(q, kvb, cache_k[l].reshape(db, past, d_attn),
                               cache_v[l].reshape(db, past, d_attn),
                               n_heads=n_heads, head_dim=head_dim, ds=ds)
        u_t = u.reshape(db, ds, d_ssm).transpose(1, 0, 2).reshape(ds * db, d_ssm)
        y_t, hr, hi = ssm_scan(u_t, coef, state_ssm_re[l].astype(F32).reshape(db, npst),
                               state_ssm_im[l].astype(F32).reshape(db, npst), emit_y=True)
        y = y_t.reshape(ds, db, d_ssm).transpose(1, 0, 2).reshape(db * ds, d_ssm)
        merged = mixer_merge(o_a, y, gates, b_gate[l], w_glu_l, w_pa_l, w_pb_l, y_permuted=False)
        xs = matmul_res(merged, w_o_l, xs)
        xs = channel_mix(xs)
        outs[4].append(k.reshape(db, ds, n_heads, head_dim))
        outs[5].append(v.reshape(db, ds, n_heads, head_dim))
        outs[6].append(hr.reshape(db, n_groups, n_state))
        outs[7].append(hi.reshape(db, n_groups, n_state))

    y_prompt = final_norm(xp, final_g).reshape(bp, seq, d)
    y_sample = final_norm(xs, final_g).reshape(db, ds, d)
    return (y_prompt, y_sample) + tuple(jnp.stack(o) for o in outs)
```

```python
import functools
import math

import jax
import jax.numpy as jnp
from jax import lax
from jax.experimental import pallas as pl
from jax.experimental.pallas import tpu as pltpu

F32 = jnp.float32
BF16 = jnp.bfloat16

RMS_EPS = 1e-6
TOP_K = 2
GROUP = 16
SUBLANES = 8
V7X_VMEM_LIMIT_BYTES = 56 * 1024 * 1024
SSM_KB_GROUPS = 16


def _cparams(*sem):
    return pltpu.CompilerParams(dimension_semantics=sem, vmem_limit_bytes=V7X_VMEM_LIMIT_BYTES)


def _tile(dim, pref):
    if dim <= pref:
        return dim
    t = pref
    while t >= 128:
        if dim % t == 0:
            return t
        t -= 128
    return dim


def _rmsnorm(x, g):
    ms = jnp.mean(x * x, axis=-1, keepdims=True)
    return (x * lax.rsqrt(ms + RMS_EPS)) * g


NORM_ROWS = 128


def _rmsnorm_rows(x_ref, g_ref, o_ref):
    rows = x_ref.shape[0]
    step = min(NORM_ROWS, rows)
    g = g_ref[...]

    def body(c, carry):
        r0 = pl.multiple_of(c * step, step)
        o_ref[pl.ds(r0, step), :] = _rmsnorm(x_ref[pl.ds(r0, step), :], g).astype(o_ref.dtype)
        return carry

    lax.fori_loop(0, rows // step, body, 0)


def _norm_proj_body(x_ref, g_ref, w_ref, q_ref, k_ref, v_ref, kvb_ref, u_ref, gate_ref, h_scr,
                    *, nq, ns, q_scale):
    n = pl.program_id(1)

    @pl.when(n == 0)
    def _():
        _rmsnorm_rows(x_ref, g_ref, h_scr)

    def z():
        return jnp.dot(h_scr[...], w_ref[...], preferred_element_type=F32)

    @pl.when(n < nq)
    def _():
        q_ref[...] = (z() * q_scale).astype(BF16)

    @pl.when((n >= nq) & (n < 2 * nq))
    def _():
        zz = z()
        k_ref[...] = zz
        kvb_ref[...] = zz.astype(BF16)

    @pl.when((n >= 2 * nq) & (n < 3 * nq))
    def _():
        zz = z()
        v_ref[...] = zz
        kvb_ref[...] = zz.astype(BF16)

    @pl.when((n >= 3 * nq) & (n < 3 * nq + ns))
    def _():
        u_ref[...] = z()

    @pl.when(n >= 3 * nq + ns)
    def _():
        gate_ref[...] = z()


def norm_proj(x, g, w, l, *, d_attn, d_ssm, head_dim, permute_u):
    t, d = x.shape
    d_in = w.shape[2]
    tn = _tile(d_ssm, 1024)
    assert d_attn % tn == 0 and d % tn == 0
    nq, ns = d_attn // tn, d_ssm // tn
    ng = 2 * d // tn
    assert d_in == (3 * nq + ns + ng) * tn
    tm = _tile(t // SUBLANES if permute_u else t, 512)
    if permute_u:
        seg = t // SUBLANES
        assert seg % tm == 0
        mps = seg // tm
        u_shape = (seg, SUBLANES * d_ssm)
        u_map = lambda m, n: (m % mps, (m // mps) * ns + jnp.clip(n - 3 * nq, 0, ns - 1))
    else:
        u_shape = (t, d_ssm)
        u_map = lambda m, n: (m, jnp.clip(n - 3 * nq, 0, ns - 1))
    blk = lambda imap: pl.BlockSpec((tm, tn), imap)
    outs = pl.pallas_call(
        functools.partial(_norm_proj_body, nq=nq, ns=ns, q_scale=head_dim ** -0.5),
        grid=(t // tm, d_in // tn),
        in_specs=[pl.BlockSpec((tm, d), lambda m, n: (m, 0), pipeline_mode=pl.Buffered(1)),
                  pl.BlockSpec((1, d), lambda m, n: (0, 0)),
                  pl.BlockSpec((None, d, tn), lambda m, n: (l, 0, n))],
        out_specs=[blk(lambda m, n: (m, jnp.minimum(n, nq - 1))),
                   blk(lambda m, n: (m, jnp.clip(n - nq, 0, nq - 1))),
                   blk(lambda m, n: (m, jnp.clip(n - 2 * nq, 0, nq - 1))),
                   blk(lambda m, n: (m, jnp.clip(n - nq, 0, 2 * nq - 1))),
                   blk(u_map),
                   blk(lambda m, n: (m, jnp.clip(n - 3 * nq - ns, 0, ng - 1)))],
        out_shape=[jax.ShapeDtypeStruct((t, d_attn), BF16),
                   jax.ShapeDtypeStruct((t, d_attn), F32),
                   jax.ShapeDtypeStruct((t, d_attn), F32),
                   jax.ShapeDtypeStruct((t, 2 * d_attn), BF16),
                   jax.ShapeDtypeStruct(u_shape, F32),
                   jax.ShapeDtypeStruct((t, 2 * d), F32)],
        scratch_shapes=[pltpu.VMEM((tm, d), BF16)],
        compiler_params=_cparams("arbitrary", "arbitrary"),
        name="norm_proj",
    )(x, g.reshape(1, d), w)
    q, k, v, kvb, u, gates = outs
    if permute_u:
        u = u.reshape(t, d_ssm)
    return q, k, v, kvb, u, gates


def _softplus(z):
    return jnp.maximum(z, 0.0) + jnp.log(1.0 + jnp.exp(-jnp.abs(z)))


def _tri(tk):
    r = lax.broadcasted_iota(jnp.int32, (tk, tk), 0)
    c = lax.broadcasted_iota(jnp.int32, (tk, tk), 1)
    return (r >= c).astype(BF16)


def _sb_tile(q, kb, vb, tri, later, mask):
    z = lax.dot_general(q, kb, (((1,), (1,)), ((), ())), preferred_element_type=F32)
    sp = _softplus(z)
    if mask is not None:
        sp = jnp.where(mask, sp, 0.0)
    hi = sp.astype(BF16)
    lo = (sp - hi.astype(F32)).astype(BF16)
    within = (jnp.dot(hi, tri, preferred_element_type=F32)
              + jnp.dot(lo, tri, preferred_element_type=F32))
    w = jnp.exp(jnp.minimum(z - (within + later), 0.0))
    if mask is not None:
        w = jnp.where(mask, w, 0.0)
    pv = jnp.dot(w.astype(BF16), vb, preferred_element_type=F32)
    return pv, later + within[:, 0:1]


def _attn_prompt_body(q_ref, k_ref, v_ref, o_ref, acc_scr, *, tq, tk):
    i = pl.program_id(1)
    nsub = tq // tk
    base = i * nsub
    tri = _tri(tk)
    later = jnp.zeros((tq, 1), F32)
    acc_scr[...] = jnp.zeros_like(acc_scr)

    for jr in range(nsub - 1, -1, -1):
        r0 = jr * tk
        k0 = pl.multiple_of((base + jr) * tk, tk)
        row = lax.broadcasted_iota(jnp.int32, (tq - r0, tk), 0)
        col = lax.broadcasted_iota(jnp.int32, (tq - r0, tk), 1)
        pv, lat = _sb_tile(q_ref[r0:, :], k_ref[pl.ds(k0, tk), :], v_ref[pl.ds(k0, tk), :], tri,
                           later[r0:], col < row)
        acc_scr[r0:, :] += pv
        later = lat if r0 == 0 else jnp.concatenate([later[:r0], lat], axis=0)

    def body(s, later):
        ja = pl.multiple_of((base - 1 - 2 * s) * tk, tk)
        jb = pl.multiple_of((base - 2 - 2 * s) * tk, tk)
        pva, later = _sb_tile(q_ref[...], k_ref[pl.ds(ja, tk), :], v_ref[pl.ds(ja, tk), :], tri,
                              later, None)
        pvb, later = _sb_tile(q_ref[...], k_ref[pl.ds(jb, tk), :], v_ref[pl.ds(jb, tk), :], tri,
                              later, None)
        acc_scr[...] += pva + pvb
        return later

    lax.fori_loop(0, base // 2, body, later)
    o_ref[...] = acc_scr[...].astype(o_ref.dtype)


ATTN_Q_ROWS = 1024
ATTN_K_ROWS = 256


def attention_prompt(q, kvb, *, n_heads, head_dim):
    t = q.shape[0]
    tk = _tile(t, ATTN_K_ROWS)
    tq = _tile(t, ATTN_Q_ROWS)
    assert tq % (2 * tk) == 0
    single = pl.Buffered(1)
    return pl.pallas_call(
        functools.partial(_attn_prompt_body, tq=tq, tk=tk),
        grid=(n_heads, t // tq),
        in_specs=[pl.BlockSpec((tq, head_dim), lambda h, i: (i, h)),
                  pl.BlockSpec((t, head_dim), lambda h, i: (0, h), pipeline_mode=single),
                  pl.BlockSpec((t, head_dim), lambda h, i: (0, n_heads + h), pipeline_mode=single)],
        out_specs=pl.BlockSpec((tq, head_dim), lambda h, i: (i, h)),
        out_shape=jax.ShapeDtypeStruct((t, n_heads * head_dim), BF16),
        scratch_shapes=[pltpu.VMEM((tq, head_dim), F32)],
        compiler_params=_cparams("arbitrary", "arbitrary"),
        name="attn_prompt",
    )(q, kvb, kvb)


def _attn_sample_body(q_ref, kn_ref, vn_ref, ck_ref, cv_ref, o_ref, *, ds, tk):
    q = q_ref[...]
    row = lax.broadcasted_iota(jnp.int32, (ds, ds), 0)
    col = lax.broadcasted_iota(jnp.int32, (ds, ds), 1)
    acc, later = _sb_tile(q, kn_ref[...], vn_ref[...], _tri(ds), jnp.zeros((ds, 1), F32), col < row)
    tri = _tri(tk)
    past = ck_ref.shape[0]
    for j in range(past // tk - 1, -1, -1):
        kb = ck_ref[j * tk:(j + 1) * tk, :].astype(BF16)
        vb = cv_ref[j * tk:(j + 1) * tk, :].astype(BF16)
        pv, later = _sb_tile(q, kb, vb, tri, later, None)
        acc = acc + pv
    o_ref[...] = acc.astype(o_ref.dtype)


def attention_sample(q, kvb, cache_k, cache_v, l, *, n_heads, head_dim, ds):
    _, b, past, _ = cache_k.shape
    tk = _tile(past, ATTN_K_ROWS)
    cache_spec = pl.BlockSpec((None, None, past, head_dim), lambda bi, h: (l, bi, 0, h))
    return pl.pallas_call(
        functools.partial(_attn_sample_body, ds=ds, tk=tk),
        grid=(b, n_heads),
        in_specs=[pl.BlockSpec((ds, head_dim), lambda bi, h: (bi, h)),
                  pl.BlockSpec((ds, head_dim), lambda bi, h: (bi, h)),
                  pl.BlockSpec((ds, head_dim), lambda bi, h: (bi, n_heads + h)),
                  cache_spec, cache_spec],
        out_specs=pl.BlockSpec((ds, head_dim), lambda bi, h: (bi, h)),
        out_shape=jax.ShapeDtypeStruct((b * ds, n_heads * head_dim), BF16),
        compiler_params=_cparams("arbitrary", "arbitrary"),
        name="attn_sample",
    )(q, kvb, kvb, cache_k, cache_v)


def _ssm_body(u_ref, bdr_ref, bdi_ref, cdr_ref, cdi_ref, ar_ref, ai_ref, d_ref, h0r_ref, h0i_ref,
              *rest, n, nkb, emit_y):
    if emit_y:
        y_ref, hr_ref, hi_ref, str_scr, sti_scr, xr_scr, xi_scr = rest
    else:
        hr_ref, hi_ref, str_scr, sti_scr, xr_scr, xi_scr = rest
    step = pl.program_id(0)
    kin = bdr_ref.shape[1]
    kst = bdr_ref.shape[2]

    @pl.when(step == 0)
    def _():
        str_scr[...] = h0r_ref[...]
        sti_scr[...] = h0i_ref[...]

    u = u_ref[...]
    ub = u.astype(BF16)
    for kb in range(nkb):
        ukb = ub[:, kb * kin:(kb + 1) * kin]
        xr_scr[:, kb * kst:(kb + 1) * kst] = jnp.dot(ukb, bdr_ref[kb], preferred_element_type=F32)
        xi_scr[:, kb * kst:(kb + 1) * kst] = jnp.dot(ukb, bdi_ref[kb], preferred_element_type=F32)

    for kb in range(nkb):
        cs = slice(kb * kst, (kb + 1) * kst)
        ar = jnp.broadcast_to(ar_ref[:, cs], (SUBLANES, kst))
        ai = jnp.broadcast_to(ai_ref[:, cs], (SUBLANES, kst))

        def scan_step(tt, carry, cs=cs, ar=ar, ai=ai):
            xr, xi = carry
            r0 = pl.multiple_of(tt * SUBLANES, SUBLANES)
            nxr = ar * xr - ai * xi + xr_scr[pl.ds(r0, SUBLANES), cs]
            nxi = ar * xi + ai * xr + xi_scr[pl.ds(r0, SUBLANES), cs]
            xr_scr[pl.ds(r0, SUBLANES), cs] = nxr
            xi_scr[pl.ds(r0, SUBLANES), cs] = nxi
            return nxr, nxi

        xr, xi = lax.fori_loop(0, n, scan_step, (str_scr[:, cs], sti_scr[:, cs]), unroll=4)
        str_scr[:, cs] = xr
        sti_scr[:, cs] = xi

    if emit_y:
        for kb in range(nkb):
            cs = slice(kb * kst, (kb + 1) * kst)
            yk = (jnp.dot(xr_scr[:, cs].astype(BF16), cdr_ref[kb], preferred_element_type=F32)
                  - jnp.dot(xi_scr[:, cs].astype(BF16), cdi_ref[kb], preferred_element_type=F32))
            os_ = slice(kb * kin, (kb + 1) * kin)
            y_ref[:, os_] = yk + d_ref[:, os_] * u[:, os_]

    @pl.when(step == pl.num_programs(0) - 1)
    def _():
        hr_ref[...] = str_scr[...]
        hi_ref[...] = sti_scr[...]


def ssm_scan(u, coef, h0r, h0i, *, emit_y):
    bdr, bdi, cdr, cdi, ar, ai, dsk = coef
    r, d_ssm = u.shape
    nkb, kin, kst = bdr.shape
    npst = nkb * kst
    rows = _tile(r, 512)
    n = rows // SUBLANES
    full = lambda a: pl.BlockSpec(a.shape, lambda s: (0,) * a.ndim)
    out_specs = [pl.BlockSpec((SUBLANES, npst), lambda s: (0, 0))] * 2
    out_shape = [jax.ShapeDtypeStruct((SUBLANES, npst), F32)] * 2
    if emit_y:
        out_specs = [pl.BlockSpec((rows, d_ssm), lambda s: (s, 0))] + out_specs
        out_shape = [jax.ShapeDtypeStruct((r, d_ssm), F32)] + out_shape
    outs = pl.pallas_call(
        functools.partial(_ssm_body, n=n, nkb=nkb, emit_y=emit_y),
        grid=(r // rows,),
        in_specs=[pl.BlockSpec((rows, d_ssm), lambda s: (s, 0)),
                  full(bdr), full(bdi), full(cdr), full(cdi), full(ar), full(ai), full(dsk),
                  full(h0r), full(h0i)],
        out_specs=out_specs,
        out_shape=out_shape,
        scratch_shapes=[pltpu.VMEM((SUBLANES, npst), F32), pltpu.VMEM((SUBLANES, npst), F32),
                        pltpu.VMEM((rows, npst), F32), pltpu.VMEM((rows, npst), F32)],
        compiler_params=_cparams("arbitrary"),
        name="ssm_scan_y" if emit_y else "ssm_scan_state",
    )(u, bdr, bdi, cdr, cdi, ar, ai, dsk, h0r, h0i)
    if emit_y:
        return outs[0], outs[1], outs[2]
    return None, outs[0], outs[1]


def _ssm_coef(lam_re, lam_im, log_dt, b_re, b_im, c_re, c_im, d_skip):
    g, p = lam_re.shape
    dt = jnp.exp(log_dt.astype(F32))[:, None]
    lr, li = lam_re.astype(F32), lam_im.astype(F32)
    mag = jnp.exp(lr * dt)
    ar = mag * jnp.cos(li * dt)
    ai = mag * jnp.sin(li * dt)
    den = lr * lr + li * li
    nr = ar - 1.0
    cr = (nr * lr + ai * li) / den
    ci = (ai * lr - nr * li) / den
    br, bi = b_re.astype(F32), b_im.astype(F32)
    bbr = cr[..., None] * br - ci[..., None] * bi
    bbi = cr[..., None] * bi + ci[..., None] * br
    gb = min(SSM_KB_GROUPS, g)
    nkb = g // gb
    eye = jnp.eye(gb, dtype=F32)

    def pack_b(m):
        m = m.reshape(nkb, gb, p, GROUP)
        return jnp.einsum('kaph,ab->kahbp', m, eye).reshape(nkb, gb * GROUP, gb * p).astype(BF16)

    def pack_c(m):
        m = m.reshape(nkb, gb, GROUP, p)
        return jnp.einsum('kahp,ab->kapbh', m, eye).reshape(nkb, gb * p, gb * GROUP).astype(BF16)

    coef = (pack_b(bbr), pack_b(bbi), pack_c(c_re.astype(F32)), pack_c(c_im.astype(F32)),
            ar.reshape(1, g * p), ai.reshape(1, g * p), d_skip.astype(F32).reshape(1, g * GROUP))
    return coef, ar, ai


def _segment_inits(ar, ai, fr, fi, seg_len):
    k = int(round(math.log2(seg_len)))
    assert 2 ** k == seg_len
    pr, pi = ar.reshape(1, -1), ai.reshape(1, -1)
    for _ in range(k):
        pr, pi = pr * pr - pi * pi, 2.0 * pr * pi
    hr = jnp.zeros_like(fr[0:1])
    hi = jnp.zeros_like(fi[0:1])
    hrs, his = [hr], [hi]
    for r in range(SUBLANES - 1):
        hr, hi = pr * hr - pi * hi + fr[r:r + 1], pr * hi + pi * hr + fi[r:r + 1]
        hrs.append(hr)
        his.append(hi)
    return jnp.concatenate(hrs, axis=0), jnp.concatenate(his, axis=0)


def _merge_body(oa_ref, y_ref, ga_ref, gb_ref, ba_ref, bb_ref, wglu_ref, wpa_ref, wpb_ref, o_ref,
                ob_scr):
    n = pl.program_id(1)

    @pl.when(n == 0)
    def _():
        gy = jax.nn.gelu(y_ref[...])
        glu = jnp.dot(gy.astype(BF16), wglu_ref[...], preferred_element_type=F32)
        ob_scr[...] = (gy * jax.nn.sigmoid(glu)).astype(BF16)

    pa = jnp.dot(oa_ref[...], wpa_ref[...], preferred_element_type=F32)
    pb = jnp.dot(ob_scr[...], wpb_ref[...], preferred_element_type=F32)
    gate_a = jax.nn.sigmoid(ga_ref[...] + ba_ref[...])
    gate_b = jax.nn.sigmoid(gb_ref[...] + bb_ref[...])
    o_ref[...] = (gate_a * pa + gate_b * pb).astype(o_ref.dtype)


def mixer_merge(o_a, y, gates, b_gate, w_glu, w_pa, w_pb, l, *, y_permuted):
    t, d_attn = o_a.shape
    d_ssm = w_glu.shape[1]
    d = w_pa.shape[2]
    tm = _tile(t // SUBLANES if y_permuted else t, 512)
    tn = _tile(d, 1024)
    nd = d // tn
    if y_permuted:
        seg = t // SUBLANES
        mps = seg // tm
        y = y.reshape(seg, SUBLANES * d_ssm)
        y_spec = pl.BlockSpec((tm, d_ssm), lambda m, n: (m % mps, m // mps))
    else:
        y_spec = pl.BlockSpec((tm, d_ssm), lambda m, n: (m, 0))
    bg = b_gate.reshape(1, 2 * d)
    return pl.pallas_call(
        _merge_body,
        grid=(t // tm, nd),
        in_specs=[pl.BlockSpec((tm, d_attn), lambda m, n: (m, 0)),
                  y_spec,
                  pl.BlockSpec((tm, tn), lambda m, n: (m, n)),
                  pl.BlockSpec((tm, tn), lambda m, n: (m, nd + n)),
                  pl.BlockSpec((1, tn), lambda m, n: (0, n)),
                  pl.BlockSpec((1, tn), lambda m, n: (0, nd + n)),
                  pl.BlockSpec((None, d_ssm, d_ssm), lambda m, n: (l, 0, 0)),
                  pl.BlockSpec((None, d_attn, tn), lambda m, n: (l, 0, n)),
                  pl.BlockSpec((None, d_ssm, tn), lambda m, n: (l, 0, n))],
        out_specs=pl.BlockSpec((tm, tn), lambda m, n: (m, n)),
        out_shape=jax.ShapeDtypeStruct((t, d), BF16),
        scratch_shapes=[pltpu.VMEM((tm, d_ssm), BF16)],
        compiler_params=_cparams("arbitrary", "arbitrary"),
        name="mixer_merge",
    )(o_a, y, gates, gates, bg, bg, w_glu, w_pa, w_pb)


def _matmul_res_body(a_ref, w_ref, r_ref, o_ref):
    o_ref[...] = r_ref[...] + jnp.dot(a_ref[...], w_ref[...], preferred_element_type=F32)


def matmul_res(a, w, wi, res):
    t, k = a.shape
    n_out = w.shape[2]
    tm = _tile(t, 1024)
    tn = _tile(n_out, 512)
    return pl.pallas_call(
        _matmul_res_body,
        grid=(t // tm, n_out // tn),
        in_specs=[pl.BlockSpec((tm, k), lambda m, n: (m, 0)),
                  pl.BlockSpec((None, k, tn), lambda m, n: (wi, 0, n)),
                  pl.BlockSpec((tm, tn), lambda m, n: (m, n))],
        out_specs=pl.BlockSpec((tm, tn), lambda m, n: (m, n)),
        out_shape=jax.ShapeDtypeStruct((t, n_out), F32),
        compiler_params=_cparams("arbitrary", "arbitrary"),
        name="matmul_res",
    )(a, w, res)


def _router_comb(h, router_ref, rb_ref, n_experts):
    logits = jnp.dot(h, router_ref[...], preferred_element_type=F32) + rb_ref[...]
    lane = lax.broadcasted_iota(jnp.int32, logits.shape, 1).astype(F32)
    neg = jnp.float32(-jnp.inf)
    logits = jnp.where(lane < n_experts, logits, neg)
    big = jnp.float32(logits.shape[1])
    m1 = jnp.max(logits, axis=-1, keepdims=True)
    i1 = jnp.min(jnp.where(logits == m1, lane, big), axis=-1, keepdims=True)
    rest = jnp.where(lane == i1, neg, logits)
    m2 = jnp.max(rest, axis=-1, keepdims=True)
    i2 = jnp.min(jnp.where(rest == m2, lane, big), axis=-1, keepdims=True)
    e2 = jnp.exp(m2 - m1)
    g1 = 1.0 / (1.0 + e2)
    g2 = e2 / (1.0 + e2)
    return jnp.where(lane == i1, g1, 0.0) + jnp.where(lane == i2, g2, 0.0)


def _ffn_up_body(x_ref, g_ref, w1_ref, w3_ref, *rest, n_experts, tiles_per_expert):
    if n_experts:
        router_ref, rb_ref, o_ref, h_scr, comb_scr = rest
    else:
        o_ref, h_scr = rest
    n = pl.program_id(1)

    @pl.when(n == 0)
    def _():
        _rmsnorm_rows(x_ref, g_ref, h_scr)
        if n_experts:
            comb_scr[...] = _router_comb(h_scr[...], router_ref, rb_ref, n_experts)

    h = h_scr[...]
    a = jnp.dot(h, w1_ref[...], preferred_element_type=F32)
    b = jnp.dot(h, w3_ref[...], preferred_element_type=F32)
    hid = jax.nn.silu(a) * b
    if n_experts:
        comb = comb_scr[...]
        lane = lax.broadcasted_iota(jnp.int32, comb.shape, 1)
        e = n // tiles_per_expert
        hid = hid * jnp.sum(jnp.where(lane == e, comb, 0.0), axis=-1, keepdims=True)
    o_ref[...] = hid.astype(o_ref.dtype)


def ffn_up(x, g, w1, w3, e0, e_w, router=None, router_b=None):
    t, d = x.shape
    f = w1.shape[2]
    n_experts = e_w if router is not None else 0
    tm = _tile(t, 512 if n_experts else 1024)
    tn = _tile(f, 768 if n_experts else 512)
    tpe = f // tn
    w_spec = pl.BlockSpec((None, d, tn), lambda m, n: (e0 + n // tpe, 0, n % tpe))
    in_specs = [pl.BlockSpec((tm, d), lambda m, n: (m, 0), pipeline_mode=pl.Buffered(1)),
                pl.BlockSpec((1, d), lambda m, n: (0, 0)), w_spec, w_spec]
    args = [x, g.reshape(1, d), w1, w3]
    scratch = [pltpu.VMEM((tm, d), BF16)]
    if n_experts:
        lanes = 128
        rpad = jnp.zeros((d, lanes), BF16).at[:, :n_experts].set(router.astype(BF16))
        bpad = jnp.zeros((1, lanes), F32).at[0, :n_experts].set(router_b.astype(F32))
        in_specs += [pl.BlockSpec((d, lanes), lambda m, n: (0, 0)),
                     pl.BlockSpec((1, lanes), lambda m, n: (0, 0))]
        args += [rpad, bpad]
        scratch.append(pltpu.VMEM((tm, lanes), F32))
    return pl.pallas_call(
        functools.partial(_ffn_up_body, n_experts=n_experts, tiles_per_expert=tpe),
        grid=(t // tm, e_w * tpe),
        in_specs=in_specs,
        out_specs=pl.BlockSpec((tm, tn), lambda m, n: (m, n)),
        out_shape=jax.ShapeDtypeStruct((t, e_w * f), BF16),
        scratch_shapes=scratch,
        compiler_params=_cparams("arbitrary", "arbitrary"),
        name="ffn_up_moe" if n_experts else "ffn_up",
    )(*args)


def _final_norm_body(x_ref, g_ref, o_ref):
    _rmsnorm_rows(x_ref, g_ref, o_ref)


def final_norm(x, g):
    t, d = x.shape
    tm = _tile(t, 512)
    return pl.pallas_call(
        _final_norm_body,
        grid=(t // tm,),
        in_specs=[pl.BlockSpec((tm, d), lambda m: (m, 0)), pl.BlockSpec((1, d), lambda m: (0, 0))],
        out_specs=pl.BlockSpec((tm, d), lambda m: (m, 0)),
        out_shape=jax.ShapeDtypeStruct((t, d), F32),
        compiler_params=_cparams("arbitrary"),
        name="final_norm",
    )(x, g.reshape(1, d))


def kernel(x_prompt, x_sample, cache_k, cache_v, state_ssm_re, state_ssm_im, norm1_g, norm2_g, w_in,
           lam_re, lam_im, log_dt, ssm_b_re, ssm_b_im, ssm_c_re, ssm_c_im, ssm_d, w_glu, w_pa, w_pb,
           b_gate, w_o, ffn_w1, ffn_w3, ffn_w2, moe_router, moe_router_b, moe_w1, moe_w3, moe_w2,
           final_g):
    bp, seq, d = x_prompt.shape
    db, ds, _ = x_sample.shape
    depth = w_in.shape[0]
    _, _, past, n_heads, head_dim = cache_k.shape
    d_attn = n_heads * head_dim
    n_groups, n_state = lam_re.shape[1:]
    d_ssm = n_groups * GROUP
    npst = n_groups * n_state
    assert bp == 1 and db == SUBLANES and seq % SUBLANES == 0

    xp = x_prompt.reshape(seq, d)
    xs = x_sample.reshape(db * ds, d)
    w_in_b, w_glu_b, w_pa_b, w_pb_b, w_o_b = (w.astype(BF16) for w in (w_in, w_glu, w_pa, w_pb, w_o))
    ffn_w1_b, ffn_w3_b, ffn_w2_b = (w.astype(BF16) for w in (ffn_w1, ffn_w3, ffn_w2))
    n_moe, n_exp, _, f_e = moe_w1.shape
    moe_w1_b = moe_w1.astype(BF16).reshape(n_moe * n_exp, d, f_e)
    moe_w3_b = moe_w3.astype(BF16).reshape(n_moe * n_exp, d, f_e)
    moe_w2_b = moe_w2.astype(BF16).reshape(n_moe, n_exp * f_e, d)
    cache_k4 = cache_k.reshape(depth, db, past, d_attn)
    cache_v4 = cache_v.reshape(depth, db, past, d_attn)
    outs = [[] for _ in range(8)]
    for l in range(depth):
        i = l // 2
        coef, ar, ai = _ssm_coef(lam_re[l], lam_im[l], log_dt[l], ssm_b_re[l], ssm_b_im[l],
                                 ssm_c_re[l], ssm_c_im[l], ssm_d[l])

        def channel_mix(x):
            if l % 2 == 0:
                hid = ffn_up(x, norm2_g[l], ffn_w1_b, ffn_w3_b, i, 1)
                return matmul_res(hid, ffn_w2_b, i, x)
            hid = ffn_up(x, norm2_g[l], moe_w1_b, moe_w3_b, i * n_exp, n_exp,
                         moe_router[i], moe_router_b[i])
            return matmul_res(hid, moe_w2_b, i, x)

        q, k, v, kvb, u, gates = norm_proj(xp, norm1_g[l], w_in_b, l, d_attn=d_attn, d_ssm=d_ssm,
                                           head_dim=head_dim, permute_u=True)
        o_a = attention_prompt(q, kvb, n_heads=n_heads, head_dim=head_dim)
        zeros = jnp.zeros((SUBLANES, npst), F32)
        _, fr, fi = ssm_scan(u, coef, zeros, zeros, emit_y=False)
        h0r, h0i = _segment_inits(ar, ai, fr, fi, seq // SUBLANES)
        y, hr, hi = ssm_scan(u, coef, h0r, h0i, emit_y=True)
        merged = mixer_merge(o_a, y, gates, b_gate[l], w_glu_b, w_pa_b, w_pb_b, l, y_permuted=True)
        xp = matmul_res(merged, w_o_b, l, xp)
        xp = channel_mix(xp)
        outs[0].append(k.reshape(1, seq, n_heads, head_dim))
        outs[1].append(v.reshape(1, seq, n_heads, head_dim))
        outs[2].append(hr[SUBLANES - 1].reshape(1, n_groups, n_state))
        outs[3].append(hi[SUBLANES - 1].reshape(1, n_groups, n_state))

        q, k, v, kvb, u, gates = norm_proj(xs, norm1_g[l], w_in_b, l, d_attn=d_attn, d_ssm=d_ssm,
                                           head_dim=head_dim, permute_u=False)
        o_a = attention_sample(q, kvb, cache_k4, cache_v4, l,
                               n_heads=n_heads, head_dim=head_dim, ds=ds)
        u_t = u.reshape(db, ds, d_ssm).transpose(1, 0, 2).reshape(ds * db, d_ssm)
        y_t, hr, hi = ssm_scan(u_t, coef, state_ssm_re[l].astype(F32).reshape(db, npst),
                               state_ssm_im[l].astype(F32).reshape(db, npst), emit_y=True)
        y = y_t.reshape(ds, db, d_ssm).transpose(1, 0, 2).reshape(db * ds, d_ssm)
        merged = mixer_merge(o_a, y, gates, b_gate[l], w_glu_b, w_pa_b, w_pb_b, l, y_permuted=False)
        xs = matmul_res(merged, w_o_b, l, xs)
        xs = channel_mix(xs)
        outs[4].append(k.reshape(db, ds, n_heads, head_dim))
        outs[5].append(v.reshape(db, ds, n_heads, head_dim))
        outs[6].append(hr.reshape(db, n_groups, n_state))
        outs[7].append(hi.reshape(db, n_groups, n_state))

    y_prompt = final_norm(xp, final_g).reshape(bp, seq, d)
    y_sample = final_norm(xs, final_g).reshape(db, ds, d)
    return (y_prompt, y_sample) + tuple(jnp.stack(o) for o in outs)
```

```python
import functools
import math

import jax
import jax.numpy as jnp
from jax import lax
from jax.experimental import pallas as pl
from jax.experimental.pallas import tpu as pltpu

F32 = jnp.float32
BF16 = jnp.bfloat16

RMS_EPS = 1e-6
TOP_K = 2
GROUP = 16
SUBLANES = 8
V7X_VMEM_LIMIT_BYTES = 56 * 1024 * 1024
SSM_KB_GROUPS = 16


def _cparams(*sem):
    return pltpu.CompilerParams(dimension_semantics=sem, vmem_limit_bytes=V7X_VMEM_LIMIT_BYTES)


def _tile(dim, pref):
    if dim <= pref:
        return dim
    t = pref
    while t >= 128:
        if dim % t == 0:
            return t
        t -= 128
    return dim


def _rmsnorm(x, g):
    ms = jnp.mean(x * x, axis=-1, keepdims=True)
    return (x * lax.rsqrt(ms + RMS_EPS)) * g


NORM_ROWS = 128


def _rmsnorm_rows(x_ref, g_ref, o_ref):
    rows = x_ref.shape[0]
    step = min(NORM_ROWS, rows)
    g = g_ref[...]

    def body(c, carry):
        r0 = pl.multiple_of(c * step, step)
        o_ref[pl.ds(r0, step), :] = _rmsnorm(x_ref[pl.ds(r0, step), :], g).astype(o_ref.dtype)
        return carry

    lax.fori_loop(0, rows // step, body, 0)


def _norm_proj_body(x_ref, g_ref, w_ref, q_ref, k_ref, v_ref, kvb_ref, u_ref, gate_ref, h_scr,
                    *, nq, ns, q_scale):
    n = pl.program_id(1)

    @pl.when(n == 0)
    def _():
        _rmsnorm_rows(x_ref, g_ref, h_scr)

    def z():
        return jnp.dot(h_scr[...], w_ref[...], preferred_element_type=F32)

    @pl.when(n < nq)
    def _():
        q_ref[...] = (z() * q_scale).astype(BF16)

    @pl.when((n >= nq) & (n < 2 * nq))
    def _():
        zz = z()
        k_ref[...] = zz
        kvb_ref[...] = zz.astype(BF16)

    @pl.when((n >= 2 * nq) & (n < 3 * nq))
    def _():
        zz = z()
        v_ref[...] = zz
        kvb_ref[...] = zz.astype(BF16)

    @pl.when((n >= 3 * nq) & (n < 3 * nq + ns))
    def _():
        u_ref[...] = z()

    @pl.when(n >= 3 * nq + ns)
    def _():
        gate_ref[...] = z().astype(BF16)


def norm_proj(x, g, w, l, *, d_attn, d_ssm, head_dim, permute_u):
    t, d = x.shape
    d_in = w.shape[2]
    tn = _tile(d_ssm, 512)
    assert d_attn % tn == 0 and d % tn == 0
    nq, ns = d_attn // tn, d_ssm // tn
    ng = 2 * d // tn
    assert d_in == (3 * nq + ns + ng) * tn
    tm = _tile(t // SUBLANES if permute_u else t, 1024)
    if permute_u:
        seg = t // SUBLANES
        assert seg % tm == 0
        mps = seg // tm
        u_shape = (seg, SUBLANES * d_ssm)
        u_map = lambda m, n: (m % mps, (m // mps) * ns + jnp.clip(n - 3 * nq, 0, ns - 1))
    else:
        u_shape = (t, d_ssm)
        u_map = lambda m, n: (m, jnp.clip(n - 3 * nq, 0, ns - 1))
    blk = lambda imap: pl.BlockSpec((tm, tn), imap)
    outs = pl.pallas_call(
        functools.partial(_norm_proj_body, nq=nq, ns=ns, q_scale=head_dim ** -0.5),
        grid=(t // tm, d_in // tn),
        in_specs=[pl.BlockSpec((tm, d), lambda m, n: (m, 0), pipeline_mode=pl.Buffered(1)),
                  pl.BlockSpec((1, d), lambda m, n: (0, 0)),
                  pl.BlockSpec((None, d, tn), lambda m, n: (l, 0, n))],
        out_specs=[blk(lambda m, n: (m, jnp.minimum(n, nq - 1))),
                   blk(lambda m, n: (m, jnp.clip(n - nq, 0, nq - 1))),
                   blk(lambda m, n: (m, jnp.clip(n - 2 * nq, 0, nq - 1))),
                   blk(lambda m, n: (m, jnp.clip(n - nq, 0, 2 * nq - 1))),
                   blk(u_map),
                   blk(lambda m, n: (m, jnp.clip(n - 3 * nq - ns, 0, ng - 1)))],
        out_shape=[jax.ShapeDtypeStruct((t, d_attn), BF16),
                   jax.ShapeDtypeStruct((t, d_attn), F32),
                   jax.ShapeDtypeStruct((t, d_attn), F32),
                   jax.ShapeDtypeStruct((t, 2 * d_attn), BF16),
                   jax.ShapeDtypeStruct(u_shape, F32),
                   jax.ShapeDtypeStruct((t, 2 * d), BF16)],
        scratch_shapes=[pltpu.VMEM((tm, d), BF16)],
        compiler_params=_cparams("arbitrary", "arbitrary"),
        name="norm_proj",
    )(x, g.reshape(1, d), w)
    q, k, v, kvb, u, gates = outs
    if permute_u:
        u = u.reshape(t, d_ssm)
    return q, k, v, kvb, u, gates


def _softplus(z):
    return jnp.maximum(z, 0.0) + jnp.log(1.0 + jnp.exp(-jnp.abs(z)))


def _tri(tk):
    r = lax.broadcasted_iota(jnp.int32, (tk, tk), 0)
    c = lax.broadcasted_iota(jnp.int32, (tk, tk), 1)
    return (r >= c).astype(BF16)


def _sb_tile(q, kb, vb, tri, later, mask):
    z = lax.dot_general(q, kb, (((1,), (1,)), ((), ())), preferred_element_type=F32)
    sp = _softplus(z)
    if mask is not None:
        sp = jnp.where(mask, sp, 0.0)
    hi = sp.astype(BF16)
    lo = (sp - hi.astype(F32)).astype(BF16)
    within = (jnp.dot(hi, tri, preferred_element_type=F32)
              + jnp.dot(lo, tri, preferred_element_type=F32))
    w = jnp.exp(jnp.minimum(z - (within + later), 0.0))
    if mask is not None:
        w = jnp.where(mask, w, 0.0)
    pv = jnp.dot(w.astype(BF16), vb, preferred_element_type=F32)
    return pv, later + within[:, 0:1]


SB_ZERO_EXPONENT = 110.0
SB_NORM_SLACK = 1.001
KNORM_ROWS = 512


def _attn_prompt_body(q_ref, k_ref, v_ref, o_ref, acc_scr, kmax_scr, *, tq, tk):
    i = pl.program_id(1)
    nsub = tq // tk
    base = i * nsub
    tri = _tri(tk)
    later = jnp.zeros((tq, 1), F32)
    acc_scr[...] = jnp.zeros_like(acc_scr)

    @pl.when(i == 0)
    def _():
        step = min(KNORM_ROWS, k_ref.shape[0])

        def knorm(c, m):
            kf = k_ref[pl.ds(pl.multiple_of(c * step, step), step), :].astype(F32)
            return jnp.maximum(m, jnp.sum(kf * kf, axis=1, keepdims=True))

        m = lax.fori_loop(0, k_ref.shape[0] // step, knorm, jnp.zeros((step, 1), F32))
        kmax_scr[...] = jnp.broadcast_to(jnp.sqrt(jnp.max(m, axis=0, keepdims=True)), kmax_scr.shape)

    qf = q_ref[...].astype(F32)
    zmax = (jnp.sqrt(jnp.sum(qf * qf, axis=1, keepdims=True)) * kmax_scr[0:1, 0:1]) * SB_NORM_SLACK

    for jr in range(nsub - 1, -1, -1):
        r0 = jr * tk
        k0 = pl.multiple_of((base + jr) * tk, tk)
        row = lax.broadcasted_iota(jnp.int32, (tq - r0, tk), 0)
        col = lax.broadcasted_iota(jnp.int32, (tq - r0, tk), 1)
        pv, lat = _sb_tile(q_ref[r0:, :], k_ref[pl.ds(k0, tk), :], v_ref[pl.ds(k0, tk), :], tri,
                           later[r0:], col < row)
        acc_scr[r0:, :] += pv
        later = lat if r0 == 0 else jnp.concatenate([later[:r0], lat], axis=0)

    def live(later):
        return (jnp.min(later - zmax) <= SB_ZERO_EXPONENT).astype(jnp.int32)

    def cond(state):
        s, _, go = state
        return (s < base) & (go > 0)

    def body(state):
        s, later, _ = state
        j0 = pl.multiple_of((base - 1 - s) * tk, tk)
        pv, later = _sb_tile(q_ref[...], k_ref[pl.ds(j0, tk), :], v_ref[pl.ds(j0, tk), :], tri,
                             later, None)
        acc_scr[...] += pv
        return s + 1, later, live(later)

    lax.while_loop(cond, body, (jnp.int32(0), later, live(later)))
    o_ref[...] = acc_scr[...].astype(o_ref.dtype)


ATTN_Q_ROWS = 1024
ATTN_K_ROWS = 256


def attention_prompt(q, kvb, *, n_heads, head_dim):
    t = q.shape[0]
    tk = _tile(t, ATTN_K_ROWS)
    tq = _tile(t, ATTN_Q_ROWS)
    assert tq % tk == 0
    single = pl.Buffered(1)
    return pl.pallas_call(
        functools.partial(_attn_prompt_body, tq=tq, tk=tk),
        grid=(n_heads, t // tq),
        in_specs=[pl.BlockSpec((tq, head_dim), lambda h, i: (i, h)),
                  pl.BlockSpec((t, head_dim), lambda h, i: (0, h), pipeline_mode=single),
                  pl.BlockSpec((t, head_dim), lambda h, i: (0, n_heads + h), pipeline_mode=single)],
        out_specs=pl.BlockSpec((tq, head_dim), lambda h, i: (i, h)),
        out_shape=jax.ShapeDtypeStruct((t, n_heads * head_dim), BF16),
        scratch_shapes=[pltpu.VMEM((tq, head_dim), F32), pltpu.VMEM((SUBLANES, 128), F32)],
        compiler_params=_cparams("arbitrary", "arbitrary"),
        name="attn_prompt",
    )(q, kvb, kvb)


def _attn_sample_body(q_ref, kn_ref, vn_ref, ck_ref, cv_ref, o_ref, *, ds, tk):
    q = q_ref[...]
    row = lax.broadcasted_iota(jnp.int32, (ds, ds), 0)
    col = lax.broadcasted_iota(jnp.int32, (ds, ds), 1)
    acc, later = _sb_tile(q, kn_ref[...], vn_ref[...], _tri(ds), jnp.zeros((ds, 1), F32), col < row)
    tri = _tri(tk)
    past = ck_ref.shape[0]
    for j in range(past // tk - 1, -1, -1):
        kb = ck_ref[j * tk:(j + 1) * tk, :].astype(BF16)
        vb = cv_ref[j * tk:(j + 1) * tk, :].astype(BF16)
        pv, later = _sb_tile(q, kb, vb, tri, later, None)
        acc = acc + pv
    o_ref[...] = acc.astype(o_ref.dtype)


def attention_sample(q, kvb, cache_k, cache_v, l, *, n_heads, head_dim, ds):
    _, b, past, _ = cache_k.shape
    tk = _tile(past, ATTN_K_ROWS)
    cache_spec = pl.BlockSpec((None, None, past, head_dim), lambda bi, h: (l, bi, 0, h))
    return pl.pallas_call(
        functools.partial(_attn_sample_body, ds=ds, tk=tk),
        grid=(b, n_heads),
        in_specs=[pl.BlockSpec((ds, head_dim), lambda bi, h: (bi, h)),
                  pl.BlockSpec((ds, head_dim), lambda bi, h: (bi, h)),
                  pl.BlockSpec((ds, head_dim), lambda bi, h: (bi, n_heads + h)),
                  cache_spec, cache_spec],
        out_specs=pl.BlockSpec((ds, head_dim), lambda bi, h: (bi, h)),
        out_shape=jax.ShapeDtypeStruct((b * ds, n_heads * head_dim), BF16),
        compiler_params=_cparams("arbitrary", "arbitrary"),
        name="attn_sample",
    )(q, kvb, kvb, cache_k, cache_v)


def _ssm_body(u_ref, bdr_ref, bdi_ref, cdr_ref, cdi_ref, ar_ref, ai_ref, d_ref, h0r_ref, h0i_ref,
              *rest, n, nkb, emit_y):
    if emit_y:
        y_ref, hr_ref, hi_ref, str_scr, sti_scr, xr_scr, xi_scr = rest
    else:
        hr_ref, hi_ref, str_scr, sti_scr, xr_scr, xi_scr = rest
    step = pl.program_id(0)
    kin = bdr_ref.shape[1]
    kst = bdr_ref.shape[2]

    @pl.when(step == 0)
    def _():
        str_scr[...] = h0r_ref[...]
        sti_scr[...] = h0i_ref[...]

    u = u_ref[...]
    ub = u.astype(BF16)
    for kb in range(nkb):
        ukb = ub[:, kb * kin:(kb + 1) * kin]
        xr_scr[:, kb * kst:(kb + 1) * kst] = jnp.dot(ukb, bdr_ref[kb], preferred_element_type=F32)
        xi_scr[:, kb * kst:(kb + 1) * kst] = jnp.dot(ukb, bdi_ref[kb], preferred_element_type=F32)

    for kb in range(nkb):
        cs = slice(kb * kst, (kb + 1) * kst)
        ar = jnp.broadcast_to(ar_ref[:, cs], (SUBLANES, kst))
        ai = jnp.broadcast_to(ai_ref[:, cs], (SUBLANES, kst))

        def scan_step(tt, carry, cs=cs, ar=ar, ai=ai):
            xr, xi = carry
            r0 = pl.multiple_of(tt * SUBLANES, SUBLANES)
            nxr = ar * xr - ai * xi + xr_scr[pl.ds(r0, SUBLANES), cs]
            nxi = ar * xi + ai * xr + xi_scr[pl.ds(r0, SUBLANES), cs]
            xr_scr[pl.ds(r0, SUBLANES), cs] = nxr
            xi_scr[pl.ds(r0, SUBLANES), cs] = nxi
            return nxr, nxi

        xr, xi = lax.fori_loop(0, n, scan_step, (str_scr[:, cs], sti_scr[:, cs]), unroll=4)
        str_scr[:, cs] = xr
        sti_scr[:, cs] = xi

    if emit_y:
        for kb in range(nkb):
            cs = slice(kb * kst, (kb + 1) * kst)
            yk = (jnp.dot(xr_scr[:, cs].astype(BF16), cdr_ref[kb], preferred_element_type=F32)
                  - jnp.dot(xi_scr[:, cs].astype(BF16), cdi_ref[kb], preferred_element_type=F32))
            os_ = slice(kb * kin, (kb + 1) * kin)
            y_ref[:, os_] = yk + d_ref[:, os_] * u[:, os_]

    @pl.when(step == pl.num_programs(0) - 1)
    def _():
        hr_ref[...] = str_scr[...]
        hi_ref[...] = sti_scr[...]


def ssm_scan(u, coef, h0r, h0i, *, emit_y):
    bdr, bdi, cdr, cdi, ar, ai, dsk = coef
    r, d_ssm = u.shape
    nkb, kin, kst = bdr.shape
    npst = nkb * kst
    rows = _tile(r, 512)
    n = rows // SUBLANES
    full = lambda a: pl.BlockSpec(a.shape, lambda s: (0,) * a.ndim)
    out_specs = [pl.BlockSpec((SUBLANES, npst), lambda s: (0, 0))] * 2
    out_shape = [jax.ShapeDtypeStruct((SUBLANES, npst), F32)] * 2
    if emit_y:
        out_specs = [pl.BlockSpec((rows, d_ssm), lambda s: (s, 0))] + out_specs
        out_shape = [jax.ShapeDtypeStruct((r, d_ssm), F32)] + out_shape
    outs = pl.pallas_call(
        functools.partial(_ssm_body, n=n, nkb=nkb, emit_y=emit_y),
        grid=(r // rows,),
        in_specs=[pl.BlockSpec((rows, d_ssm), lambda s: (s, 0)),
                  full(bdr), full(bdi), full(cdr), full(cdi), full(ar), full(ai), full(dsk),
                  full(h0r), full(h0i)],
        out_specs=out_specs,
        out_shape=out_shape,
        scratch_shapes=[pltpu.VMEM((SUBLANES, npst), F32), pltpu.VMEM((SUBLANES, npst), F32),
                        pltpu.VMEM((rows, npst), F32), pltpu.VMEM((rows, npst), F32)],
        compiler_params=_cparams("arbitrary"),
        name="ssm_scan_y" if emit_y else "ssm_scan_state",
    )(u, bdr, bdi, cdr, cdi, ar, ai, dsk, h0r, h0i)
    if emit_y:
        return outs[0], outs[1], outs[2]
    return None, outs[0], outs[1]


def _ssm_coef(lam_re, lam_im, log_dt, b_re, b_im, c_re, c_im, d_skip):
    g, p = lam_re.shape
    dt = jnp.exp(log_dt.astype(F32))[:, None]
    lr, li = lam_re.astype(F32), lam_im.astype(F32)
    mag = jnp.exp(lr * dt)
    ar = mag * jnp.cos(li * dt)
    ai = mag * jnp.sin(li * dt)
    den = lr * lr + li * li
    nr = ar - 1.0
    cr = (nr * lr + ai * li) / den
    ci = (ai * lr - nr * li) / den
    br, bi = b_re.astype(F32), b_im.astype(F32)
    bbr = cr[..., None] * br - ci[..., None] * bi
    bbi = cr[..., None] * bi + ci[..., None] * br
    gb = min(SSM_KB_GROUPS, g)
    nkb = g // gb
    eye = jnp.eye(gb, dtype=F32)

    def pack_b(m):
        m = m.reshape(nkb, gb, p, GROUP)
        return jnp.einsum('kaph,ab->kahbp', m, eye).reshape(nkb, gb * GROUP, gb * p).astype(BF16)

    def pack_c(m):
        m = m.reshape(nkb, gb, GROUP, p)
        return jnp.einsum('kahp,ab->kapbh', m, eye).reshape(nkb, gb * p, gb * GROUP).astype(BF16)

    coef = (pack_b(bbr), pack_b(bbi), pack_c(c_re.astype(F32)), pack_c(c_im.astype(F32)),
            ar.reshape(1, g * p), ai.reshape(1, g * p), d_skip.astype(F32).reshape(1, g * GROUP))
    return coef, ar, ai


def _segment_inits(ar, ai, fr, fi, seg_len):
    k = int(round(math.log2(seg_len)))
    assert 2 ** k == seg_len
    pr, pi = ar.reshape(1, -1), ai.reshape(1, -1)
    for _ in range(k):
        pr, pi = pr * pr - pi * pi, 2.0 * pr * pi
    hr = jnp.zeros_like(fr[0:1])
    hi = jnp.zeros_like(fi[0:1])
    hrs, his = [hr], [hi]
    for r in range(SUBLANES - 1):
        hr, hi = pr * hr - pi * hi + fr[r:r + 1], pr * hi + pi * hr + fi[r:r + 1]
        hrs.append(hr)
        his.append(hi)
    return jnp.concatenate(hrs, axis=0), jnp.concatenate(his, axis=0)


def _merge_body(oa_ref, y_ref, ga_ref, gb_ref, ba_ref, bb_ref, wglu_ref, wpa_ref, wpb_ref, o_ref,
                ob_scr):
    n = pl.program_id(1)

    @pl.when(n == 0)
    def _():
        gy = jax.nn.gelu(y_ref[...])
        glu = jnp.dot(gy.astype(BF16), wglu_ref[...], preferred_element_type=F32)
        ob_scr[...] = (gy * jax.nn.sigmoid(glu)).astype(BF16)

    pa = jnp.dot(oa_ref[...], wpa_ref[...], preferred_element_type=F32)
    pb = jnp.dot(ob_scr[...], wpb_ref[...], preferred_element_type=F32)
    gate_a = jax.nn.sigmoid(ga_ref[...].astype(F32) + ba_ref[...])
    gate_b = jax.nn.sigmoid(gb_ref[...].astype(F32) + bb_ref[...])
    o_ref[...] = (gate_a * pa + gate_b * pb).astype(o_ref.dtype)


def mixer_merge(o_a, y, gates, b_gate, w_glu, w_pa, w_pb, l, *, y_permuted):
    t, d_attn = o_a.shape
    d_ssm = w_glu.shape[1]
    d = w_pa.shape[2]
    tm = _tile(t // SUBLANES if y_permuted else t, 512)
    tn = _tile(d, 1024)
    nd = d // tn
    if y_permuted:
        seg = t // SUBLANES
        mps = seg // tm
        y = y.reshape(seg, SUBLANES * d_ssm)
        y_spec = pl.BlockSpec((tm, d_ssm), lambda m, n: (m % mps, m // mps))
    else:
        y_spec = pl.BlockSpec((tm, d_ssm), lambda m, n: (m, 0))
    bg = b_gate.reshape(1, 2 * d)
    return pl.pallas_call(
        _merge_body,
        grid=(t // tm, nd),
        in_specs=[pl.BlockSpec((tm, d_attn), lambda m, n: (m, 0)),
                  y_spec,
                  pl.BlockSpec((tm, tn), lambda m, n: (m, n)),
                  pl.BlockSpec((tm, tn), lambda m, n: (m, nd + n)),
                  pl.BlockSpec((1, tn), lambda m, n: (0, n)),
                  pl.BlockSpec((1, tn), lambda m, n: (0, nd + n)),
                  pl.BlockSpec((None, d_ssm, d_ssm), lambda m, n: (l, 0, 0)),
                  pl.BlockSpec((None, d_attn, tn), lambda m, n: (l, 0, n)),
                  pl.BlockSpec((None, d_ssm, tn), lambda m, n: (l, 0, n))],
        out_specs=pl.BlockSpec((tm, tn), lambda m, n: (m, n)),
        out_shape=jax.ShapeDtypeStruct((t, d), BF16),
        scratch_shapes=[pltpu.VMEM((tm, d_ssm), BF16)],
        compiler_params=_cparams("arbitrary", "arbitrary"),
        name="mixer_merge",
    )(o_a, y, gates, gates, bg, bg, w_glu, w_pa, w_pb)


def _matmul_res_body(a_ref, w_ref, r_ref, o_ref):
    o_ref[...] = r_ref[...] + jnp.dot(a_ref[...], w_ref[...], preferred_element_type=F32)


def matmul_res(a, w, wi, res):
    t, k = a.shape
    n_out = w.shape[2]
    tm = _tile(t, 1024)
    tn = _tile(n_out, 512)
    return pl.pallas_call(
        _matmul_res_body,
        grid=(t // tm, n_out // tn),
        in_specs=[pl.BlockSpec((tm, k), lambda m, n: (m, 0)),
                  pl.BlockSpec((None, k, tn), lambda m, n: (wi, 0, n)),
                  pl.BlockSpec((tm, tn), lambda m, n: (m, n))],
        out_specs=pl.BlockSpec((tm, tn), lambda m, n: (m, n)),
        out_shape=jax.ShapeDtypeStruct((t, n_out), F32),
        compiler_params=_cparams("arbitrary", "arbitrary"),
        name="matmul_res",
    )(a, w, res)


def _router_comb(h, router_ref, rb_ref, n_experts):
    logits = jnp.dot(h, router_ref[...], preferred_element_type=F32) + rb_ref[...]
    lane = lax.broadcasted_iota(jnp.int32, logits.shape, 1).astype(F32)
    neg = jnp.float32(-jnp.inf)
    logits = jnp.where(lane < n_experts, logits, neg)
    big = jnp.float32(logits.shape[1])
    m1 = jnp.max(logits, axis=-1, keepdims=True)
    i1 = jnp.min(jnp.where(logits == m1, lane, big), axis=-1, keepdims=True)
    rest = jnp.where(lane == i1, neg, logits)
    m2 = jnp.max(rest, axis=-1, keepdims=True)
    i2 = jnp.min(jnp.where(rest == m2, lane, big), axis=-1, keepdims=True)
    e2 = jnp.exp(m2 - m1)
    g1 = 1.0 / (1.0 + e2)
    g2 = e2 / (1.0 + e2)
    return jnp.where(lane == i1, g1, 0.0) + jnp.where(lane == i2, g2, 0.0)


def _ffn_up_body(x_ref, g_ref, w1_ref, w3_ref, *rest, n_experts, tiles_per_expert):
    if n_experts:
        router_ref, rb_ref, o_ref, h_scr, comb_scr = rest
    else:
        o_ref, h_scr = rest
    n = pl.program_id(1)

    @pl.when(n == 0)
    def _():
        _rmsnorm_rows(x_ref, g_ref, h_scr)
        if n_experts:
            comb_scr[...] = _router_comb(h_scr[...], router_ref, rb_ref, n_experts)

    h = h_scr[...]
    a = jnp.dot(h, w1_ref[...], preferred_element_type=F32)
    b = jnp.dot(h, w3_ref[...], preferred_element_type=F32)
    hid = jax.nn.silu(a) * b
    if n_experts:
        comb = comb_scr[...]
        lane = lax.broadcasted_iota(jnp.int32, comb.shape, 1)
        e = n // tiles_per_expert
        hid = hid * jnp.sum(jnp.where(lane == e, comb, 0.0), axis=-1, keepdims=True)
    o_ref[...] = hid.astype(o_ref.dtype)


def ffn_up(x, g, w1, w3, e0, e_w, router=None, router_b=None):
    t, d = x.shape
    f = w1.shape[2]
    n_experts = e_w if router is not None else 0
    tm = _tile(t, 512 if n_experts else 1024)
    tn = _tile(f, 768 if n_experts else 512)
    tpe = f // tn
    w_spec = pl.BlockSpec((None, d, tn), lambda m, n: (e0 + n // tpe, 0, n % tpe))
    in_specs = [pl.BlockSpec((tm, d), lambda m, n: (m, 0), pipeline_mode=pl.Buffered(1)),
                pl.BlockSpec((1, d), lambda m, n: (0, 0)), w_spec, w_spec]
    args = [x, g.reshape(1, d), w1, w3]
    scratch = [pltpu.VMEM((tm, d), BF16)]
    if n_experts:
        lanes = 128
        rpad = jnp.zeros((d, lanes), BF16).at[:, :n_experts].set(router.astype(BF16))
        bpad = jnp.zeros((1, lanes), F32).at[0, :n_experts].set(router_b.astype(F32))
        in_specs += [pl.BlockSpec((d, lanes), lambda m, n: (0, 0)),
                     pl.BlockSpec((1, lanes), lambda m, n: (0, 0))]
        args += [rpad, bpad]
        scratch.append(pltpu.VMEM((tm, lanes), F32))
    return pl.pallas_call(
        functools.partial(_ffn_up_body, n_experts=n_experts, tiles_per_expert=tpe),
        grid=(t // tm, e_w * tpe),
        in_specs=in_specs,
        out_specs=pl.BlockSpec((tm, tn), lambda m, n: (m, n)),
        out_shape=jax.ShapeDtypeStruct((t, e_w * f), BF16),
        scratch_shapes=scratch,
        compiler_params=_cparams("arbitrary", "arbitrary"),
        name="ffn_up_moe" if n_experts else "ffn_up",
    )(*args)


def _final_norm_body(x_ref, g_ref, o_ref):
    _rmsnorm_rows(x_ref, g_ref, o_ref)


def final_norm(x, g):
    t, d = x.shape
    tm = _tile(t, 512)
    return pl.pallas_call(
        _final_norm_body,
        grid=(t // tm,),
        in_specs=[pl.BlockSpec((tm, d), lambda m: (m, 0)), pl.BlockSpec((1, d), lambda m: (0, 0))],
        out_specs=pl.BlockSpec((tm, d), lambda m: (m, 0)),
        out_shape=jax.ShapeDtypeStruct((t, d), F32),
        compiler_params=_cparams("arbitrary"),
        name="final_norm",
    )(x, g.reshape(1, d))


def kernel(x_prompt, x_sample, cache_k, cache_v, state_ssm_re, state_ssm_im, norm1_g, norm2_g, w_in,
           lam_re, lam_im, log_dt, ssm_b_re, ssm_b_im, ssm_c_re, ssm_c_im, ssm_d, w_glu, w_pa, w_pb,
           b_gate, w_o, ffn_w1, ffn_w3, ffn_w2, moe_router, moe_router_b, moe_w1, moe_w3, moe_w2,
           final_g):
    bp, seq, d = x_prompt.shape
    db, ds, _ = x_sample.shape
    depth = w_in.shape[0]
    _, _, past, n_heads, head_dim = cache_k.shape
    d_attn = n_heads * head_dim
    n_groups, n_state = lam_re.shape[1:]
    d_ssm = n_groups * GROUP
    npst = n_groups * n_state
    assert bp == 1 and db == SUBLANES and seq % SUBLANES == 0

    xp = x_prompt.reshape(seq, d)
    xs = x_sample.reshape(db * ds, d)
    w_in_b, w_glu_b, w_pa_b, w_pb_b, w_o_b = (w.astype(BF16) for w in (w_in, w_glu, w_pa, w_pb, w_o))
    ffn_w1_b, ffn_w3_b, ffn_w2_b = (w.astype(BF16) for w in (ffn_w1, ffn_w3, ffn_w2))
    n_moe, n_exp, _, f_e = moe_w1.shape
    moe_w1_b = moe_w1.astype(BF16).reshape(n_moe * n_exp, d, f_e)
    moe_w3_b = moe_w3.astype(BF16).reshape(n_moe * n_exp, d, f_e)
    moe_w2_b = moe_w2.astype(BF16).reshape(n_moe, n_exp * f_e, d)
    cache_k4 = cache_k.reshape(depth, db, past, d_attn)
    cache_v4 = cache_v.reshape(depth, db, past, d_attn)
    outs = [[] for _ in range(8)]
    for l in range(depth):
        i = l // 2
        coef, ar, ai = _ssm_coef(lam_re[l], lam_im[l], log_dt[l], ssm_b_re[l], ssm_b_im[l],
                                 ssm_c_re[l], ssm_c_im[l], ssm_d[l])

        def channel_mix(x):
            if l % 2 == 0:
                hid = ffn_up(x, norm2_g[l], ffn_w1_b, ffn_w3_b, i, 1)
                return matmul_res(hid, ffn_w2_b, i, x)
            hid = ffn_up(x, norm2_g[l], moe_w1_b, moe_w3_b, i * n_exp, n_exp,
                         moe_router[i], moe_router_b[i])
            return matmul_res(hid, moe_w2_b, i, x)

        q, k, v, kvb, u, gates = norm_proj(xp, norm1_g[l], w_in_b, l, d_attn=d_attn, d_ssm=d_ssm,
                                           head_dim=head_dim, permute_u=True)
        o_a = attention_prompt(q, kvb, n_heads=n_heads, head_dim=head_dim)
        zeros = jnp.zeros((SUBLANES, npst), F32)
        _, fr, fi = ssm_scan(u, coef, zeros, zeros, emit_y=False)
        h0r, h0i = _segment_inits(ar, ai, fr, fi, seq // SUBLANES)
        y, hr, hi = ssm_scan(u, coef, h0r, h0i, emit_y=True)
        merged = mixer_merge(o_a, y, gates, b_gate[l], w_glu_b, w_pa_b, w_pb_b, l, y_permuted=True)
        xp = matmul_res(merged, w_o_b, l, xp)
        xp = channel_mix(xp)
        outs[0].append(k.reshape(1, seq, n_heads, head_dim))
        outs[1].append(v.reshape(1, seq, n_heads, head_dim))
        outs[2].append(hr[SUBLANES - 1].reshape(1, n_groups, n_state))
        outs[3].append(hi[SUBLANES - 1].reshape(1, n_groups, n_state))

        q, k, v, kvb, u, gates = norm_proj(xs, norm1_g[l], w_in_b, l, d_attn=d_attn, d_ssm=d_ssm,
                                           head_dim=head_dim, permute_u=False)
        o_a = attention_sample(q, kvb, cache_k4, cache_v4, l,
                               n_heads=n_heads, head_dim=head_dim, ds=ds)
        u_t = u.reshape(db, ds, d_ssm).transpose(1, 0, 2).reshape(ds * db, d_ssm)
        y_t, hr, hi = ssm_scan(u_t, coef, state_ssm_re[l].astype(F32).reshape(db, npst),
                               state_ssm_im[l].astype(F32).reshape(db, npst), emit_y=True)
        y = y_t.reshape(ds, db, d_ssm).transpose(1, 0, 2).reshape(db * ds, d_ssm)
        merged = mixer_merge(o_a, y, gates, b_gate[l], w_glu_b, w_pa_b, w_pb_b, l, y_permuted=False)
        xs = matmul_res(merged, w_o_b, l, xs)
        xs = channel_mix(xs)
        outs[4].append(k.reshape(db, ds, n_heads, head_dim))
        outs[5].append(v.reshape(db, ds, n_heads, head_dim))
        outs[6].append(hr.reshape(db, n_groups, n_state))
        outs[7].append(hi.reshape(db, n_groups, n_state))

    y_prompt = final_norm(xp, final_g).reshape(bp, seq, d)
    y_sample = final_norm(xs, final_g).reshape(db, ds, d)
    return (y_prompt, y_sample) + tuple(jnp.stack(o) for o in outs)
```

```python
import functools
import math

import jax
import jax.numpy as jnp
from jax import lax
from jax.experimental import pallas as pl
from jax.experimental.pallas import tpu as pltpu

F32 = jnp.float32
BF16 = jnp.bfloat16

RMS_EPS = 1e-6
TOP_K = 2
GROUP = 16
SUBLANES = 8
V7X_VMEM_LIMIT_BYTES = 56 * 1024 * 1024
SSM_KB_GROUPS = 16
SSM_STEPS_PER_BLOCK = 64


def _cparams(*sem):
    return pltpu.CompilerParams(dimension_semantics=sem, vmem_limit_bytes=V7X_VMEM_LIMIT_BYTES)


def _tile(dim, pref):
    if dim <= pref:
        return dim
    t = pref
    while t >= 128:
        if dim % t == 0:
            return t
        t -= 128
    return dim


def _rmsnorm(x, g):
    ms = jnp.mean(x * x, axis=-1, keepdims=True)
    return (x * lax.rsqrt(ms + RMS_EPS)) * g


NORM_ROWS = 128


def _rmsnorm_rows(x_ref, g_ref, o_ref):
    rows = x_ref.shape[0]
    step = min(NORM_ROWS, rows)
    g = g_ref[...]

    def body(c, carry):
        r0 = pl.multiple_of(c * step, step)
        o_ref[pl.ds(r0, step), :] = _rmsnorm(x_ref[pl.ds(r0, step), :], g).astype(o_ref.dtype)
        return carry

    lax.fori_loop(0, rows // step, body, 0)


def _norm_proj_body(x_ref, g_ref, w_ref, q_ref, k_ref, v_ref, kvb_ref, u_ref, gate_ref, h_scr,
                    *, nq, ns, q_scale):
    n = pl.program_id(1)

    @pl.when(n == 0)
    def _():
        _rmsnorm_rows(x_ref, g_ref, h_scr)

    def z():
        return jnp.dot(h_scr[...], w_ref[...], preferred_element_type=F32)

    @pl.when(n < nq)
    def _():
        q_ref[...] = (z() * q_scale).astype(BF16)

    @pl.when((n >= nq) & (n < 2 * nq))
    def _():
        zz = z()
        k_ref[...] = zz
        kvb_ref[...] = zz.astype(BF16)

    @pl.when((n >= 2 * nq) & (n < 3 * nq))
    def _():
        zz = z()
        v_ref[...] = zz
        kvb_ref[...] = zz.astype(BF16)

    @pl.when((n >= 3 * nq) & (n < 3 * nq + ns))
    def _():
        u_ref[...] = z()

    @pl.when(n >= 3 * nq + ns)
    def _():
        gate_ref[...] = z().astype(BF16)


def norm_proj(x, g, w, l, *, d_attn, d_ssm, head_dim, permute_u):
    t, d = x.shape
    d_in = w.shape[2]
    tn = _tile(d_ssm, 512)
    assert d_attn % tn == 0 and d % tn == 0
    nq, ns = d_attn // tn, d_ssm // tn
    ng = 2 * d // tn
    assert d_in == (3 * nq + ns + ng) * tn
    tm = _tile(t // SUBLANES if permute_u else t, 1024)
    if permute_u:
        seg = t // SUBLANES
        assert seg % tm == 0
        mps = seg // tm
        u_shape = (seg, SUBLANES * d_ssm)
        u_map = lambda m, n: (m % mps, (m // mps) * ns + jnp.clip(n - 3 * nq, 0, ns - 1))
    else:
        u_shape = (t, d_ssm)
        u_map = lambda m, n: (m, jnp.clip(n - 3 * nq, 0, ns - 1))
    blk = lambda imap: pl.BlockSpec((tm, tn), imap)
    outs = pl.pallas_call(
        functools.partial(_norm_proj_body, nq=nq, ns=ns, q_scale=head_dim ** -0.5),
        grid=(t // tm, d_in // tn),
        in_specs=[pl.BlockSpec((tm, d), lambda m, n: (m, 0), pipeline_mode=pl.Buffered(1)),
                  pl.BlockSpec((1, d), lambda m, n: (0, 0)),
                  pl.BlockSpec((None, d, tn), lambda m, n: (l, 0, n))],
        out_specs=[blk(lambda m, n: (m, jnp.minimum(n, nq - 1))),
                   blk(lambda m, n: (m, jnp.clip(n - nq, 0, nq - 1))),
                   blk(lambda m, n: (m, jnp.clip(n - 2 * nq, 0, nq - 1))),
                   blk(lambda m, n: (m, jnp.clip(n - nq, 0, 2 * nq - 1))),
                   blk(u_map),
                   blk(lambda m, n: (m, jnp.clip(n - 3 * nq - ns, 0, ng - 1)))],
        out_shape=[jax.ShapeDtypeStruct((t, d_attn), BF16),
                   jax.ShapeDtypeStruct((t, d_attn), F32),
                   jax.ShapeDtypeStruct((t, d_attn), F32),
                   jax.ShapeDtypeStruct((t, 2 * d_attn), BF16),
                   jax.ShapeDtypeStruct(u_shape, F32),
                   jax.ShapeDtypeStruct((t, 2 * d), BF16)],
        scratch_shapes=[pltpu.VMEM((tm, d), BF16)],
        compiler_params=_cparams("arbitrary", "arbitrary"),
        name="norm_proj",
    )(x, g.reshape(1, d), w)
    return outs


def _softplus(z):
    return jnp.maximum(z, 0.0) + jnp.log(1.0 + jnp.exp(-jnp.abs(z)))


def _tri(tk):
    r = lax.broadcasted_iota(jnp.int32, (tk, tk), 0)
    c = lax.broadcasted_iota(jnp.int32, (tk, tk), 1)
    return (r >= c).astype(BF16)


def _sb_tile(q, kb, vb, tri, later, mask):
    z = lax.dot_general(q, kb, (((1,), (1,)), ((), ())), preferred_element_type=F32)
    sp = _softplus(z)
    if mask is not None:
        sp = jnp.where(mask, sp, 0.0)
    hi = sp.astype(BF16)
    lo = (sp - hi.astype(F32)).astype(BF16)
    within = (jnp.dot(hi, tri, preferred_element_type=F32)
              + jnp.dot(lo, tri, preferred_element_type=F32))
    w = jnp.exp(jnp.minimum(z - (within + later), 0.0))
    if mask is not None:
        w = jnp.where(mask, w, 0.0)
    pv = jnp.dot(w.astype(BF16), vb, preferred_element_type=F32)
    return pv, later + within[:, 0:1]


SB_ZERO_EXPONENT = 110.0
SB_NORM_SLACK = 1.001
KNORM_ROWS = 512


def _attn_prompt_body(q_ref, k_ref, v_ref, o_ref, acc_scr, kmax_scr, *, tq, tk):
    i = pl.program_id(1)
    nsub = tq // tk
    base = i * nsub
    tri = _tri(tk)
    later = jnp.zeros((tq, 1), F32)
    acc_scr[...] = jnp.zeros_like(acc_scr)

    @pl.when(i == 0)
    def _():
        step = min(KNORM_ROWS, k_ref.shape[0])

        def knorm(c, m):
            kf = k_ref[pl.ds(pl.multiple_of(c * step, step), step), :].astype(F32)
            return jnp.maximum(m, jnp.sum(kf * kf, axis=1, keepdims=True))

        m = lax.fori_loop(0, k_ref.shape[0] // step, knorm, jnp.zeros((step, 1), F32))
        kmax_scr[...] = jnp.broadcast_to(jnp.sqrt(jnp.max(m, axis=0, keepdims=True)), kmax_scr.shape)

    qf = q_ref[...].astype(F32)
    zmax = (jnp.sqrt(jnp.sum(qf * qf, axis=1, keepdims=True)) * kmax_scr[0:1, 0:1]) * SB_NORM_SLACK

    for jr in range(nsub - 1, -1, -1):
        r0 = jr * tk
        k0 = pl.multiple_of((base + jr) * tk, tk)
        row = lax.broadcasted_iota(jnp.int32, (tq - r0, tk), 0)
        col = lax.broadcasted_iota(jnp.int32, (tq - r0, tk), 1)
        pv, lat = _sb_tile(q_ref[r0:, :], k_ref[pl.ds(k0, tk), :], v_ref[pl.ds(k0, tk), :], tri,
                           later[r0:], col < row)
        acc_scr[r0:, :] += pv
        later = lat if r0 == 0 else jnp.concatenate([later[:r0], lat], axis=0)

    def live(later):
        return (jnp.min(later - zmax) <= SB_ZERO_EXPONENT).astype(jnp.int32)

    def cond(state):
        s, _, go = state
        return (s < base) & (go > 0)

    def body(state):
        s, later, _ = state
        j0 = pl.multiple_of((base - 1 - s) * tk, tk)
        pv, later = _sb_tile(q_ref[...], k_ref[pl.ds(j0, tk), :], v_ref[pl.ds(j0, tk), :], tri,
                             later, None)
        acc_scr[...] += pv
        return s + 1, later, live(later)

    lax.while_loop(cond, body, (jnp.int32(0), later, live(later)))
    o_ref[...] = acc_scr[...].astype(o_ref.dtype)


ATTN_Q_ROWS = 1024
ATTN_K_ROWS = 256


def attention_prompt(q, kvb, *, n_heads, head_dim):
    t = q.shape[0]
    tk = _tile(t, ATTN_K_ROWS)
    tq = _tile(t, ATTN_Q_ROWS)
    assert tq % tk == 0
    single = pl.Buffered(1)
    return pl.pallas_call(
        functools.partial(_attn_prompt_body, tq=tq, tk=tk),
        grid=(n_heads, t // tq),
        in_specs=[pl.BlockSpec((tq, head_dim), lambda h, i: (i, h)),
                  pl.BlockSpec((t, head_dim), lambda h, i: (0, h), pipeline_mode=single),
                  pl.BlockSpec((t, head_dim), lambda h, i: (0, n_heads + h), pipeline_mode=single)],
        out_specs=pl.BlockSpec((tq, head_dim), lambda h, i: (i, h)),
        out_shape=jax.ShapeDtypeStruct((t, n_heads * head_dim), BF16),
        scratch_shapes=[pltpu.VMEM((tq, head_dim), F32), pltpu.VMEM((SUBLANES, 128), F32)],
        compiler_params=_cparams("arbitrary", "arbitrary"),
        name="attn_prompt",
    )(q, kvb, kvb)


def _attn_sample_body(q_ref, kvn_ref, ck_ref, cv_ref, o_ref, *, ds, tk, n_heads, head_dim):
    row = lax.broadcasted_iota(jnp.int32, (ds, ds), 0)
    col = lax.broadcasted_iota(jnp.int32, (ds, ds), 1)
    tri_new, tri = _tri(ds), _tri(tk)
    past = ck_ref.shape[0]
    d_attn = n_heads * head_dim
    for h in range(n_heads):
        hs = slice(h * head_dim, (h + 1) * head_dim)
        q = q_ref[:, hs]
        acc, later = _sb_tile(q, kvn_ref[:, hs], kvn_ref[:, d_attn + h * head_dim:d_attn + (h + 1) * head_dim],
                              tri_new, jnp.zeros((ds, 1), F32), col < row)
        for j in range(past // tk - 1, -1, -1):
            kb = ck_ref[j * tk:(j + 1) * tk, h, :].astype(BF16)
            vb = cv_ref[j * tk:(j + 1) * tk, h, :].astype(BF16)
            pv, later = _sb_tile(q, kb, vb, tri, later, None)
            acc = acc + pv
        o_ref[:, hs] = acc.astype(o_ref.dtype)


def attention_sample(q, kvb, cache_k, cache_v, l, *, ds):
    _, b, past, n_heads, head_dim = cache_k.shape
    d_attn = n_heads * head_dim
    tk = _tile(past, ATTN_K_ROWS)
    cache_spec = pl.BlockSpec((None, None, past, n_heads, head_dim), lambda bi: (l, bi, 0, 0, 0))
    return pl.pallas_call(
        functools.partial(_attn_sample_body, ds=ds, tk=tk, n_heads=n_heads, head_dim=head_dim),
        grid=(b,),
        in_specs=[pl.BlockSpec((ds, d_attn), lambda bi: (bi, 0)),
                  pl.BlockSpec((ds, 2 * d_attn), lambda bi: (bi, 0)),
                  cache_spec, cache_spec],
        out_specs=pl.BlockSpec((ds, d_attn), lambda bi: (bi, 0)),
        out_shape=jax.ShapeDtypeStruct((b * ds, d_attn), BF16),
        compiler_params=_cparams("arbitrary"),
        name="attn_sample",
    )(q, kvb, cache_k, cache_v)


def _stack_heads_body(*refs, n_heads, head_dim):
    *in_refs, o_ref = refs
    l = pl.program_id(0)
    for i, x_ref in enumerate(in_refs):
        @pl.when(l == i)
        def _(x_ref=x_ref):
            for h in range(n_heads):
                o_ref[:, h, :] = x_ref[:, h * head_dim:(h + 1) * head_dim]


def stack_heads(xs, batch, *, n_heads, head_dim):
    depth = len(xs)
    t = xs[0].shape[0]
    tm = _tile(t, 512)
    nm = t // tm
    in_specs = [pl.BlockSpec((tm, n_heads * head_dim),
                             lambda l, m, i=i: (jnp.clip(m + (l - i) * nm, 0, nm - 1), 0))
                for i in range(depth)]
    out = pl.pallas_call(
        functools.partial(_stack_heads_body, n_heads=n_heads, head_dim=head_dim),
        grid=(depth, nm),
        in_specs=in_specs,
        out_specs=pl.BlockSpec((None, None, tm, n_heads, head_dim), lambda l, m: (l, 0, m, 0, 0)),
        out_shape=jax.ShapeDtypeStruct((depth, 1, t, n_heads, head_dim), F32),
        compiler_params=_cparams("arbitrary", "arbitrary"),
        name="stack_heads",
    )(*xs)
    return out.reshape(depth, batch, t // batch, n_heads, head_dim)


def _ssm_body(u_ref, bdr_ref, bdi_ref, cdr_ref, cdi_ref, ar_ref, ai_ref, d_ref, h0r_ref, h0i_ref,
              *rest, n, nkb, emit_y):
    if emit_y:
        y_ref, hr_ref, hi_ref, str_scr, sti_scr, xr_scr, xi_scr, rows_scr = rest
    else:
        hr_ref, hi_ref, str_scr, sti_scr, xr_scr, xi_scr, rows_scr = rest
    step = pl.program_id(0)
    kin = bdr_ref.shape[1]
    kst = bdr_ref.shape[2]

    @pl.when(step == 0)
    def _():
        str_scr[...] = h0r_ref[...]
        sti_scr[...] = h0i_ref[...]

    d_ssm = nkb * kin
    for r in range(SUBLANES):
        rows_scr[:, r, :] = u_ref[:, r * d_ssm:(r + 1) * d_ssm]
    u = rows_scr[...].reshape(n * SUBLANES, d_ssm)
    ub = u.astype(BF16)
    for kb in range(nkb):
        ukb = ub[:, kb * kin:(kb + 1) * kin]
        xr_scr[:, kb * kst:(kb + 1) * kst] = jnp.dot(ukb, bdr_ref[kb], preferred_element_type=F32)
        xi_scr[:, kb * kst:(kb + 1) * kst] = jnp.dot(ukb, bdi_ref[kb], preferred_element_type=F32)

    for kb in range(nkb):
        cs = slice(kb * kst, (kb + 1) * kst)
        ar = jnp.broadcast_to(ar_ref[:, cs], (SUBLANES, kst))
        ai = jnp.broadcast_to(ai_ref[:, cs], (SUBLANES, kst))

        def scan_step(tt, carry, cs=cs, ar=ar, ai=ai):
            xr, xi = carry
            r0 = pl.multiple_of(tt * SUBLANES, SUBLANES)
            nxr = ar * xr - ai * xi + xr_scr[pl.ds(r0, SUBLANES), cs]
            nxi = ar * xi + ai * xr + xi_scr[pl.ds(r0, SUBLANES), cs]
            xr_scr[pl.ds(r0, SUBLANES), cs] = nxr
            xi_scr[pl.ds(r0, SUBLANES), cs] = nxi
            return nxr, nxi

        xr, xi = lax.fori_loop(0, n, scan_step, (str_scr[:, cs], sti_scr[:, cs]), unroll=4)
        str_scr[:, cs] = xr
        sti_scr[:, cs] = xi

    if emit_y:
        ys = []
        for kb in range(nkb):
            cs = slice(kb * kst, (kb + 1) * kst)
            yk = (jnp.dot(xr_scr[:, cs].astype(BF16), cdr_ref[kb], preferred_element_type=F32)
                  - jnp.dot(xi_scr[:, cs].astype(BF16), cdi_ref[kb], preferred_element_type=F32))
            os_ = slice(kb * kin, (kb + 1) * kin)
            ys.append(yk + d_ref[:, os_] * u[:, os_])
        rows_scr[...] = jnp.concatenate(ys, axis=1).reshape(n, SUBLANES, d_ssm)
        for r in range(SUBLANES):
            y_ref[:, r * d_ssm:(r + 1) * d_ssm] = rows_scr[:, r, :]

    @pl.when(step == pl.num_programs(0) - 1)
    def _():
        hr_ref[...] = str_scr[...]
        hi_ref[...] = sti_scr[...]


def ssm_scan(u, coef, h0r, h0i, *, emit_y):
    bdr, bdi, cdr, cdi, ar, ai, dsk = coef
    steps, width = u.shape
    d_ssm = width // SUBLANES
    nkb, kin, kst = bdr.shape
    npst = nkb * kst
    n = math.gcd(steps, SSM_STEPS_PER_BLOCK)
    rows = n * SUBLANES
    full = lambda a: pl.BlockSpec(a.shape, lambda s: (0,) * a.ndim)
    out_specs = [pl.BlockSpec((SUBLANES, npst), lambda s: (0, 0))] * 2
    out_shape = [jax.ShapeDtypeStruct((SUBLANES, npst), F32)] * 2
    if emit_y:
        out_specs = [pl.BlockSpec((n, width), lambda s: (s, 0))] + out_specs
        out_shape = [jax.ShapeDtypeStruct((steps, width), F32)] + out_shape
    outs = pl.pallas_call(
        functools.partial(_ssm_body, n=n, nkb=nkb, emit_y=emit_y),
        grid=(steps // n,),
        in_specs=[pl.BlockSpec((n, width), lambda s: (s, 0)),
                  full(bdr), full(bdi), full(cdr), full(cdi), full(ar), full(ai), full(dsk),
                  full(h0r), full(h0i)],
        out_specs=out_specs,
        out_shape=out_shape,
        scratch_shapes=[pltpu.VMEM((SUBLANES, npst), F32), pltpu.VMEM((SUBLANES, npst), F32),
                        pltpu.VMEM((rows, npst), F32), pltpu.VMEM((rows, npst), F32),
                        pltpu.VMEM((n, SUBLANES, d_ssm), F32)],
        compiler_params=_cparams("arbitrary"),
        name="ssm_scan_y" if emit_y else "ssm_scan_state",
    )(u, bdr, bdi, cdr, cdi, ar, ai, dsk, h0r, h0i)
    if emit_y:
        return outs[0], outs[1], outs[2]
    return None, outs[0], outs[1]


def _ssm_coef(lam_re, lam_im, log_dt, b_re, b_im, c_re, c_im, d_skip):
    g, p = lam_re.shape
    dt = jnp.exp(log_dt.astype(F32))[:, None]
    lr, li = lam_re.astype(F32), lam_im.astype(F32)
    mag = jnp.exp(lr * dt)
    ar = mag * jnp.cos(li * dt)
    ai = mag * jnp.sin(li * dt)
    den = lr * lr + li * li
    nr = ar - 1.0
    cr = (nr * lr + ai * li) / den
    ci = (ai * lr - nr * li) / den
    br, bi = b_re.astype(F32), b_im.astype(F32)
    bbr = cr[..., None] * br - ci[..., None] * bi
    bbi = cr[..., None] * bi + ci[..., None] * br
    gb = min(SSM_KB_GROUPS, g)
    nkb = g // gb
    eye = jnp.eye(gb, dtype=F32)

    def pack_b(m):
        m = m.reshape(nkb, gb, p, GROUP)
        return jnp.einsum('kaph,ab->kahbp', m, eye).reshape(nkb, gb * GROUP, gb * p).astype(BF16)

    def pack_c(m):
        m = m.reshape(nkb, gb, GROUP, p)
        return jnp.einsum('kahp,ab->kapbh', m, eye).reshape(nkb, gb * p, gb * GROUP).astype(BF16)

    coef = (pack_b(bbr), pack_b(bbi), pack_c(c_re.astype(F32)), pack_c(c_im.astype(F32)),
            ar.reshape(1, g * p), ai.reshape(1, g * p), d_skip.astype(F32).reshape(1, g * GROUP))
    return coef, ar, ai


def _segment_inits(ar, ai, fr, fi, seg_len):
    k = int(round(math.log2(seg_len)))
    assert 2 ** k == seg_len
    pr, pi = ar.reshape(1, -1), ai.reshape(1, -1)
    for _ in range(k):
        pr, pi = pr * pr - pi * pi, 2.0 * pr * pi
    hr = jnp.zeros_like(fr[0:1])
    hi = jnp.zeros_like(fi[0:1])
    hrs, his = [hr], [hi]
    for r in range(SUBLANES - 1):
        hr, hi = pr * hr - pi * hi + fr[r:r + 1], pr * hi + pi * hr + fi[r:r + 1]
        hrs.append(hr)
        his.append(hi)
    return jnp.concatenate(hrs, axis=0), jnp.concatenate(his, axis=0)


def _merge_body(oa_ref, y_ref, ga_ref, gb_ref, ba_ref, bb_ref, wglu_ref, wpa_ref, wpb_ref, o_ref,
                ob_scr):
    n = pl.program_id(1)

    @pl.when(n == 0)
    def _():
        gy = jax.nn.gelu(y_ref[...])
        glu = jnp.dot(gy.astype(BF16), wglu_ref[...], preferred_element_type=F32)
        ob_scr[...] = (gy * jax.nn.sigmoid(glu)).astype(BF16)

    pa = jnp.dot(oa_ref[...], wpa_ref[...], preferred_element_type=F32)
    pb = jnp.dot(ob_scr[...], wpb_ref[...], preferred_element_type=F32)
    gate_a = jax.nn.sigmoid(ga_ref[...].astype(F32) + ba_ref[...])
    gate_b = jax.nn.sigmoid(gb_ref[...].astype(F32) + bb_ref[...])
    o_ref[...] = (gate_a * pa + gate_b * pb).astype(o_ref.dtype)


def mixer_merge(o_a, y, gates, b_gate, w_glu, w_pa, w_pb, l, *, y_permuted):
    t, d_attn = o_a.shape
    d_ssm = w_glu.shape[1]
    d = w_pa.shape[2]
    tm = _tile(t // SUBLANES if y_permuted else t, 512)
    tn = _tile(d, 1024)
    nd = d // tn
    if y_permuted:
        seg = t // SUBLANES
        mps = seg // tm
        assert y.shape == (seg, SUBLANES * d_ssm)
        y_spec = pl.BlockSpec((tm, d_ssm), lambda m, n: (m % mps, m // mps))
    else:
        y_spec = pl.BlockSpec((tm, d_ssm), lambda m, n: (m, 0))
    bg = b_gate.reshape(1, 2 * d)
    return pl.pallas_call(
        _merge_body,
        grid=(t // tm, nd),
        in_specs=[pl.BlockSpec((tm, d_attn), lambda m, n: (m, 0)),
                  y_spec,
                  pl.BlockSpec((tm, tn), lambda m, n: (m, n)),
                  pl.BlockSpec((tm, tn), lambda m, n: (m, nd + n)),
                  pl.BlockSpec((1, tn), lambda m, n: (0, n)),
                  pl.BlockSpec((1, tn), lambda m, n: (0, nd + n)),
                  pl.BlockSpec((None, d_ssm, d_ssm), lambda m, n: (l, 0, 0)),
                  pl.BlockSpec((None, d_attn, tn), lambda m, n: (l, 0, n)),
                  pl.BlockSpec((None, d_ssm, tn), lambda m, n: (l, 0, n))],
        out_specs=pl.BlockSpec((tm, tn), lambda m, n: (m, n)),
        out_shape=jax.ShapeDtypeStruct((t, d), BF16),
        scratch_shapes=[pltpu.VMEM((tm, d_ssm), BF16)],
        compiler_params=_cparams("arbitrary", "arbitrary"),
        name="mixer_merge",
    )(o_a, y, gates, gates, bg, bg, w_glu, w_pa, w_pb)


def _matmul_res_body(a_ref, w_ref, r_ref, o_ref):
    o_ref[...] = r_ref[...] + jnp.dot(a_ref[...], w_ref[...], preferred_element_type=F32)


def matmul_res(a, w, wi, res):
    t, k = a.shape
    n_out = w.shape[2]
    tm = _tile(t, 1024)
    tn = _tile(n_out, 512)
    return pl.pallas_call(
        _matmul_res_body,
        grid=(t // tm, n_out // tn),
        in_specs=[pl.BlockSpec((tm, k), lambda m, n: (m, 0)),
                  pl.BlockSpec((None, k, tn), lambda m, n: (wi, 0, n)),
                  pl.BlockSpec((tm, tn), lambda m, n: (m, n))],
        out_specs=pl.BlockSpec((tm, tn), lambda m, n: (m, n)),
        out_shape=jax.ShapeDtypeStruct((t, n_out), F32),
        compiler_params=_cparams("arbitrary", "arbitrary"),
        name="matmul_res",
    )(a, w, res)


def _router_comb(h, router_ref, rb_ref, n_experts):
    logits = jnp.dot(h, router_ref[...], preferred_element_type=F32) + rb_ref[...]
    lane = lax.broadcasted_iota(jnp.int32, logits.shape, 1).astype(F32)
    neg = jnp.float32(-jnp.inf)
    logits = jnp.where(lane < n_experts, logits, neg)
    big = jnp.float32(logits.shape[1])
    m1 = jnp.max(logits, axis=-1, keepdims=True)
    i1 = jnp.min(jnp.where(logits == m1, lane, big), axis=-1, keepdims=True)
    rest = jnp.where(lane == i1, neg, logits)
    m2 = jnp.max(rest, axis=-1, keepdims=True)
    i2 = jnp.min(jnp.where(rest == m2, lane, big), axis=-1, keepdims=True)
    e2 = jnp.exp(m2 - m1)
    g1 = 1.0 / (1.0 + e2)
    g2 = e2 / (1.0 + e2)
    return jnp.where(lane == i1, g1, 0.0) + jnp.where(lane == i2, g2, 0.0)


def _ffn_up_body(x_ref, g_ref, w1_ref, w3_ref, *rest, n_experts, tiles_per_expert):
    if n_experts:
        router_ref, rb_ref, o_ref, h_scr, comb_scr = rest
    else:
        o_ref, h_scr = rest
    n = pl.program_id(1)

    @pl.when(n == 0)
    def _():
        _rmsnorm_rows(x_ref, g_ref, h_scr)
        if n_experts:
            comb_scr[...] = _router_comb(h_scr[...], router_ref, rb_ref, n_experts)

    h = h_scr[...]
    a = jnp.dot(h, w1_ref[...], preferred_element_type=F32)
    b = jnp.dot(h, w3_ref[...], preferred_element_type=F32)
    hid = jax.nn.silu(a) * b
    if n_experts:
        comb = comb_scr[...]
        lane = lax.broadcasted_iota(jnp.int32, comb.shape, 1)
        e = n // tiles_per_expert
        hid = hid * jnp.sum(jnp.where(lane == e, comb, 0.0), axis=-1, keepdims=True)
    o_ref[...] = hid.astype(o_ref.dtype)


def ffn_up(x, g, w1, w3, e0, e_w, router=None, router_b=None):
    t, d = x.shape
    f = w1.shape[2]
    n_experts = e_w if router is not None else 0
    tm = _tile(t, 512 if n_experts else 1024)
    tn = _tile(f, 768 if n_experts else 512)
    tpe = f // tn
    w_spec = pl.BlockSpec((None, d, tn), lambda m, n: (e0 + n // tpe, 0, n % tpe))
    in_specs = [pl.BlockSpec((tm, d), lambda m, n: (m, 0), pipeline_mode=pl.Buffered(1)),
                pl.BlockSpec((1, d), lambda m, n: (0, 0)), w_spec, w_spec]
    args = [x, g.reshape(1, d), w1, w3]
    scratch = [pltpu.VMEM((tm, d), BF16)]
    if n_experts:
        lanes = 128
        rpad = jnp.zeros((d, lanes), BF16).at[:, :n_experts].set(router.astype(BF16))
        bpad = jnp.zeros((1, lanes), F32).at[0, :n_experts].set(router_b.astype(F32))
        in_specs += [pl.BlockSpec((d, lanes), lambda m, n: (0, 0)),
                     pl.BlockSpec((1, lanes), lambda m, n: (0, 0))]
        args += [rpad, bpad]
        scratch.append(pltpu.VMEM((tm, lanes), F32))
    return pl.pallas_call(
        functools.partial(_ffn_up_body, n_experts=n_experts, tiles_per_expert=tpe),
        grid=(t // tm, e_w * tpe),
        in_specs=in_specs,
        out_specs=pl.BlockSpec((tm, tn), lambda m, n: (m, n)),
        out_shape=jax.ShapeDtypeStruct((t, e_w * f), BF16),
        scratch_shapes=scratch,
        compiler_params=_cparams("arbitrary", "arbitrary"),
        name="ffn_up_moe" if n_experts else "ffn_up",
    )(*args)


def _final_norm_body(x_ref, g_ref, o_ref):
    _rmsnorm_rows(x_ref, g_ref, o_ref)


def final_norm(x, g):
    t, d = x.shape
    tm = _tile(t, 512)
    return pl.pallas_call(
        _final_norm_body,
        grid=(t // tm,),
        in_specs=[pl.BlockSpec((tm, d), lambda m: (m, 0)), pl.BlockSpec((1, d), lambda m: (0, 0))],
        out_specs=pl.BlockSpec((tm, d), lambda m: (m, 0)),
        out_shape=jax.ShapeDtypeStruct((t, d), F32),
        compiler_params=_cparams("arbitrary"),
        name="final_norm",
    )(x, g.reshape(1, d))


def kernel(x_prompt, x_sample, cache_k, cache_v, state_ssm_re, state_ssm_im, norm1_g, norm2_g, w_in,
           lam_re, lam_im, log_dt, ssm_b_re, ssm_b_im, ssm_c_re, ssm_c_im, ssm_d, w_glu, w_pa, w_pb,
           b_gate, w_o, ffn_w1, ffn_w3, ffn_w2, moe_router, moe_router_b, moe_w1, moe_w3, moe_w2,
           final_g):
    bp, seq, d = x_prompt.shape
    db, ds, _ = x_sample.shape
    depth = w_in.shape[0]
    _, _, past, n_heads, head_dim = cache_k.shape
    d_attn = n_heads * head_dim
    n_groups, n_state = lam_re.shape[1:]
    d_ssm = n_groups * GROUP
    npst = n_groups * n_state
    assert bp == 1 and db == SUBLANES and seq % SUBLANES == 0

    xp = x_prompt.reshape(seq, d)
    xs = x_sample.reshape(db * ds, d)
    w_in_b, w_glu_b, w_pa_b, w_pb_b, w_o_b = (w.astype(BF16) for w in (w_in, w_glu, w_pa, w_pb, w_o))
    ffn_w1_b, ffn_w3_b, ffn_w2_b = (w.astype(BF16) for w in (ffn_w1, ffn_w3, ffn_w2))
    n_moe, n_exp, _, f_e = moe_w1.shape
    moe_w1_b = moe_w1.astype(BF16).reshape(n_moe * n_exp, d, f_e)
    moe_w3_b = moe_w3.astype(BF16).reshape(n_moe * n_exp, d, f_e)
    moe_w2_b = moe_w2.astype(BF16).reshape(n_moe, n_exp * f_e, d)
    outs = [[] for _ in range(8)]
    for l in range(depth):
        i = l // 2
        coef, ar, ai = _ssm_coef(lam_re[l], lam_im[l], log_dt[l], ssm_b_re[l], ssm_b_im[l],
                                 ssm_c_re[l], ssm_c_im[l], ssm_d[l])

        def channel_mix(x):
            if l % 2 == 0:
                hid = ffn_up(x, norm2_g[l], ffn_w1_b, ffn_w3_b, i, 1)
                return matmul_res(hid, ffn_w2_b, i, x)
            hid = ffn_up(x, norm2_g[l], moe_w1_b, moe_w3_b, i * n_exp, n_exp,
                         moe_router[i], moe_router_b[i])
            return matmul_res(hid, moe_w2_b, i, x)

        q, k, v, kvb, u, gates = norm_proj(xp, norm1_g[l], w_in_b, l, d_attn=d_attn, d_ssm=d_ssm,
                                           head_dim=head_dim, permute_u=True)
        o_a = attention_prompt(q, kvb, n_heads=n_heads, head_dim=head_dim)
        zeros = jnp.zeros((SUBLANES, npst), F32)
        _, fr, fi = ssm_scan(u, coef, zeros, zeros, emit_y=False)
        h0r, h0i = _segment_inits(ar, ai, fr, fi, seq // SUBLANES)
        y, hr, hi = ssm_scan(u, coef, h0r, h0i, emit_y=True)
        merged = mixer_merge(o_a, y, gates, b_gate[l], w_glu_b, w_pa_b, w_pb_b, l, y_permuted=True)
        xp = matmul_res(merged, w_o_b, l, xp)
        xp = channel_mix(xp)
        outs[0].append(k)
        outs[1].append(v)
        outs[2].append(hr[SUBLANES - 1].reshape(1, n_groups, n_state))
        outs[3].append(hi[SUBLANES - 1].reshape(1, n_groups, n_state))

        q, k, v, kvb, u, gates = norm_proj(xs, norm1_g[l], w_in_b, l, d_attn=d_attn, d_ssm=d_ssm,
                                           head_dim=head_dim, permute_u=False)
        o_a = attention_sample(q, kvb, cache_k, cache_v, l, ds=ds)
        u_t = u.reshape(db, ds, d_ssm).transpose(1, 0, 2).reshape(ds, db * d_ssm)
        y_t, hr, hi = ssm_scan(u_t, coef, state_ssm_re[l].astype(F32).reshape(db, npst),
                               state_ssm_im[l].astype(F32).reshape(db, npst), emit_y=True)
        y = y_t.reshape(ds, db, d_ssm).transpose(1, 0, 2).reshape(db * ds, d_ssm)
        merged = mixer_merge(o_a, y, gates, b_gate[l], w_glu_b, w_pa_b, w_pb_b, l, y_permuted=False)
        xs = matmul_res(merged, w_o_b, l, xs)
        xs = channel_mix(xs)
        outs[4].append(k)
        outs[5].append(v)
        outs[6].append(hr.reshape(db, n_groups, n_state))
        outs[7].append(hi.reshape(db, n_groups, n_state))

    y_prompt = final_norm(xp, final_g).reshape(bp, seq, d)
    y_sample = final_norm(xs, final_g).reshape(db, ds, d)
    heads = dict(n_heads=n_heads, head_dim=head_dim)
    kv_outs = {0: stack_heads(outs[0], bp, **heads), 1: stack_heads(outs[1], bp, **heads),
               4: stack_heads(outs[4], db, **heads), 5: stack_heads(outs[5], db, **heads)}
    return (y_prompt, y_sample) + tuple(kv_outs[j] if j in kv_outs else jnp.stack(outs[j])
                                        for j in range(8))
```

```python
import functools
import math

import jax
import jax.numpy as jnp
from jax import lax
from jax.experimental import pallas as pl
from jax.experimental.pallas import tpu as pltpu

F32 = jnp.float32
BF16 = jnp.bfloat16

RMS_EPS = 1e-6
TOP_K = 2
GROUP = 16
SUBLANES = 8
V7X_VMEM_LIMIT_BYTES = 56 * 1024 * 1024
SSM_KB_GROUPS = 16
SSM_STEPS_PER_BLOCK = 64


def _cparams(*sem):
    return pltpu.CompilerParams(dimension_semantics=sem, vmem_limit_bytes=V7X_VMEM_LIMIT_BYTES)


def _tile(dim, pref):
    if dim <= pref:
        return dim
    t = pref
    while t >= 128:
        if dim % t == 0:
            return t
        t -= 128
    return dim


def _rmsnorm(x, g):
    ms = jnp.mean(x * x, axis=-1, keepdims=True)
    return (x * lax.rsqrt(ms + RMS_EPS)) * g


NORM_ROWS = 128


def _rmsnorm_rows(x_ref, g_ref, o_ref):
    rows = x_ref.shape[0]
    step = min(NORM_ROWS, rows)
    g = g_ref[...]

    def body(c, carry):
        r0 = pl.multiple_of(c * step, step)
        o_ref[pl.ds(r0, step), :] = _rmsnorm(x_ref[pl.ds(r0, step), :], g).astype(o_ref.dtype)
        return carry

    lax.fori_loop(0, rows // step, body, 0)


def _norm_proj_body(x_ref, g_ref, w_ref, q_ref, k_ref, v_ref, kvb_ref, u_ref, gate_ref, h_scr,
                    *, nq, ns, q_scale):
    n = pl.program_id(1)

    @pl.when(n == 0)
    def _():
        _rmsnorm_rows(x_ref, g_ref, h_scr)

    def z():
        return jnp.dot(h_scr[...], w_ref[...], preferred_element_type=F32)

    @pl.when(n < nq)
    def _():
        q_ref[...] = (z() * q_scale).astype(BF16)

    @pl.when((n >= nq) & (n < 2 * nq))
    def _():
        zz = z()
        k_ref[...] = zz
        kvb_ref[...] = zz.astype(BF16)

    @pl.when((n >= 2 * nq) & (n < 3 * nq))
    def _():
        zz = z()
        v_ref[...] = zz
        kvb_ref[...] = zz.astype(BF16)

    @pl.when((n >= 3 * nq) & (n < 3 * nq + ns))
    def _():
        u_ref[...] = z()

    @pl.when(n >= 3 * nq + ns)
    def _():
        gate_ref[...] = z().astype(BF16)


def norm_proj(x, g, w, l, *, d_attn, d_ssm, head_dim, permute_u):
    t, d = x.shape
    d_in = w.shape[2]
    tn = _tile(d_ssm, 512)
    assert d_attn % tn == 0 and d % tn == 0
    nq, ns = d_attn // tn, d_ssm // tn
    ng = 2 * d // tn
    assert d_in == (3 * nq + ns + ng) * tn
    tm = _tile(t // SUBLANES if permute_u else t, 1024)
    if permute_u:
        seg = t // SUBLANES
        assert seg % tm == 0
        mps = seg // tm
        u_shape = (seg, SUBLANES * d_ssm)
        u_map = lambda m, n: (m % mps, (m // mps) * ns + jnp.clip(n - 3 * nq, 0, ns - 1))
    else:
        u_shape = (t, d_ssm)
        u_map = lambda m, n: (m, jnp.clip(n - 3 * nq, 0, ns - 1))
    blk = lambda imap: pl.BlockSpec((tm, tn), imap)
    outs = pl.pallas_call(
        functools.partial(_norm_proj_body, nq=nq, ns=ns, q_scale=head_dim ** -0.5),
        grid=(t // tm, d_in // tn),
        in_specs=[pl.BlockSpec((tm, d), lambda m, n: (m, 0), pipeline_mode=pl.Buffered(1)),
                  pl.BlockSpec((1, d), lambda m, n: (0, 0)),
                  pl.BlockSpec((None, d, tn), lambda m, n: (l, 0, n))],
        out_specs=[blk(lambda m, n: (m, jnp.minimum(n, nq - 1))),
                   blk(lambda m, n: (m, jnp.clip(n - nq, 0, nq - 1))),
                   blk(lambda m, n: (m, jnp.clip(n - 2 * nq, 0, nq - 1))),
                   blk(lambda m, n: (m, jnp.clip(n - nq, 0, 2 * nq - 1))),
                   blk(u_map),
                   blk(lambda m, n: (m, jnp.clip(n - 3 * nq - ns, 0, ng - 1)))],
        out_shape=[jax.ShapeDtypeStruct((t, d_attn), BF16),
                   jax.ShapeDtypeStruct((t, d_attn), F32),
                   jax.ShapeDtypeStruct((t, d_attn), F32),
                   jax.ShapeDtypeStruct((t, 2 * d_attn), BF16),
                   jax.ShapeDtypeStruct(u_shape, F32),
                   jax.ShapeDtypeStruct((t, 2 * d), BF16)],
        scratch_shapes=[pltpu.VMEM((tm, d), BF16)],
        compiler_params=_cparams("arbitrary", "arbitrary"),
        name="norm_proj",
    )(x, g.reshape(1, d), w)
    return outs


def _softplus(z):
    return jnp.maximum(z, 0.0) + jnp.log(1.0 + jnp.exp(-jnp.abs(z)))


def _tri(tk):
    r = lax.broadcasted_iota(jnp.int32, (tk, tk), 0)
    c = lax.broadcasted_iota(jnp.int32, (tk, tk), 1)
    return (r >= c).astype(BF16)


def _sb_tile(q, kb, vb, tri, later, mask):
    z = lax.dot_general(q, kb, (((1,), (1,)), ((), ())), preferred_element_type=F32)
    sp = _softplus(z)
    if mask is not None:
        sp = jnp.where(mask, sp, 0.0)
    hi = sp.astype(BF16)
    lo = (sp - hi.astype(F32)).astype(BF16)
    within = (jnp.dot(hi, tri, preferred_element_type=F32)
              + jnp.dot(lo, tri, preferred_element_type=F32))
    w = jnp.exp(jnp.minimum(z - (within + later), 0.0))
    if mask is not None:
        w = jnp.where(mask, w, 0.0)
    pv = jnp.dot(w.astype(BF16), vb, preferred_element_type=F32)
    return pv, later + within[:, 0:1]


SB_ZERO_EXPONENT = 110.0
SB_NORM_SLACK = 1.001
KNORM_ROWS = 512


def _attn_prompt_body(q_ref, k_ref, v_ref, o_ref, acc_scr, kmax_scr, *, tq, tk):
    i = pl.program_id(1)
    nsub = tq // tk
    base = i * nsub
    tri = _tri(tk)
    later = jnp.zeros((tq, 1), F32)
    acc_scr[...] = jnp.zeros_like(acc_scr)

    @pl.when(i == 0)
    def _():
        step = min(KNORM_ROWS, k_ref.shape[0])

        def knorm(c, m):
            kf = k_ref[pl.ds(pl.multiple_of(c * step, step), step), :].astype(F32)
            return jnp.maximum(m, jnp.sum(kf * kf, axis=1, keepdims=True))

        m = lax.fori_loop(0, k_ref.shape[0] // step, knorm, jnp.zeros((step, 1), F32))
        kmax_scr[...] = jnp.broadcast_to(jnp.sqrt(jnp.max(m, axis=0, keepdims=True)), kmax_scr.shape)

    qf = q_ref[...].astype(F32)
    zmax = (jnp.sqrt(jnp.sum(qf * qf, axis=1, keepdims=True)) * kmax_scr[0:1, 0:1]) * SB_NORM_SLACK

    for jr in range(nsub - 1, -1, -1):
        r0 = jr * tk
        k0 = pl.multiple_of((base + jr) * tk, tk)
        row = lax.broadcasted_iota(jnp.int32, (tq - r0, tk), 0)
        col = lax.broadcasted_iota(jnp.int32, (tq - r0, tk), 1)
        pv, lat = _sb_tile(q_ref[r0:, :], k_ref[pl.ds(k0, tk), :], v_ref[pl.ds(k0, tk), :], tri,
                           later[r0:], col < row)
        acc_scr[r0:, :] += pv
        later = lat if r0 == 0 else jnp.concatenate([later[:r0], lat], axis=0)

    def live(later):
        return (jnp.min(later - zmax) <= SB_ZERO_EXPONENT).astype(jnp.int32)

    def cond(state):
        s, _, go = state
        return (s < base) & (go > 0)

    def body(state):
        s, later, _ = state
        j0 = pl.multiple_of((base - 1 - s) * tk, tk)
        pv, later = _sb_tile(q_ref[...], k_ref[pl.ds(j0, tk), :], v_ref[pl.ds(j0, tk), :], tri,
                             later, None)
        acc_scr[...] += pv
        return s + 1, later, live(later)

    lax.while_loop(cond, body, (jnp.int32(0), later, live(later)))
    o_ref[...] = acc_scr[...].astype(o_ref.dtype)


ATTN_Q_ROWS = 1024
ATTN_K_ROWS = 256


def attention_prompt(q, kvb, *, n_heads, head_dim):
    t = q.shape[0]
    tk = _tile(t, ATTN_K_ROWS)
    tq = _tile(t, ATTN_Q_ROWS)
    assert tq % tk == 0
    single = pl.Buffered(1)
    return pl.pallas_call(
        functools.partial(_attn_prompt_body, tq=tq, tk=tk),
        grid=(n_heads, t // tq),
        in_specs=[pl.BlockSpec((tq, head_dim), lambda h, i: (i, h)),
                  pl.BlockSpec((t, head_dim), lambda h, i: (0, h), pipeline_mode=single),
                  pl.BlockSpec((t, head_dim), lambda h, i: (0, n_heads + h), pipeline_mode=single)],
        out_specs=pl.BlockSpec((tq, head_dim), lambda h, i: (i, h)),
        out_shape=jax.ShapeDtypeStruct((t, n_heads * head_dim), BF16),
        scratch_shapes=[pltpu.VMEM((tq, head_dim), F32), pltpu.VMEM((SUBLANES, 128), F32)],
        compiler_params=_cparams("arbitrary", "arbitrary"),
        name="attn_prompt",
    )(q, kvb, kvb)


def _attn_sample_body(q_ref, kvn_ref, ck_ref, cv_ref, o_ref, *, ds, tk, n_heads, head_dim):
    row = lax.broadcasted_iota(jnp.int32, (ds, ds), 0)
    col = lax.broadcasted_iota(jnp.int32, (ds, ds), 1)
    tri_new, tri = _tri(ds), _tri(tk)
    past = ck_ref.shape[0]
    d_attn = n_heads * head_dim
    for h in range(n_heads):
        hs = slice(h * head_dim, (h + 1) * head_dim)
        q = q_ref[:, hs]
        acc, later = _sb_tile(q, kvn_ref[:, hs], kvn_ref[:, d_attn + h * head_dim:d_attn + (h + 1) * head_dim],
                              tri_new, jnp.zeros((ds, 1), F32), col < row)
        for j in range(past // tk - 1, -1, -1):
            kb = ck_ref[j * tk:(j + 1) * tk, h, :].astype(BF16)
            vb = cv_ref[j * tk:(j + 1) * tk, h, :].astype(BF16)
            pv, later = _sb_tile(q, kb, vb, tri, later, None)
            acc = acc + pv
        o_ref[:, hs] = acc.astype(o_ref.dtype)


def attention_sample(q, kvb, cache_k, cache_v, l, *, ds):
    _, b, past, n_heads, head_dim = cache_k.shape
    d_attn = n_heads * head_dim
    tk = _tile(past, ATTN_K_ROWS)
    cache_spec = pl.BlockSpec((None, None, past, n_heads, head_dim), lambda bi: (l, bi, 0, 0, 0))
    return pl.pallas_call(
        functools.partial(_attn_sample_body, ds=ds, tk=tk, n_heads=n_heads, head_dim=head_dim),
        grid=(b,),
        in_specs=[pl.BlockSpec((ds, d_attn), lambda bi: (bi, 0)),
                  pl.BlockSpec((ds, 2 * d_attn), lambda bi: (bi, 0)),
                  cache_spec, cache_spec],
        out_specs=pl.BlockSpec((ds, d_attn), lambda bi: (bi, 0)),
        out_shape=jax.ShapeDtypeStruct((b * ds, d_attn), BF16),
        compiler_params=_cparams("arbitrary"),
        name="attn_sample",
    )(q, kvb, cache_k, cache_v)


def _stack_heads_body(*refs, n_heads, head_dim):
    *in_refs, o_ref = refs
    l = pl.program_id(0)
    for i, x_ref in enumerate(in_refs):
        @pl.when(l == i)
        def _(x_ref=x_ref):
            for h in range(n_heads):
                o_ref[:, h, :] = x_ref[:, h * head_dim:(h + 1) * head_dim]


def stack_heads(xs, batch, *, n_heads, head_dim):
    depth = len(xs)
    t = xs[0].shape[0]
    tm = _tile(t, 512)
    nm = t // tm
    in_specs = [pl.BlockSpec((tm, n_heads * head_dim),
                             lambda l, m, i=i: (jnp.clip(m + (l - i) * nm, 0, nm - 1), 0))
                for i in range(depth)]
    out = pl.pallas_call(
        functools.partial(_stack_heads_body, n_heads=n_heads, head_dim=head_dim),
        grid=(depth, nm),
        in_specs=in_specs,
        out_specs=pl.BlockSpec((None, None, tm, n_heads, head_dim), lambda l, m: (l, 0, m, 0, 0)),
        out_shape=jax.ShapeDtypeStruct((depth, 1, t, n_heads, head_dim), F32),
        compiler_params=_cparams("arbitrary", "arbitrary"),
        name="stack_heads",
    )(*xs)
    return out.reshape(depth, batch, t // batch, n_heads, head_dim)


def _ssm_body(u_ref, bdr_ref, bdi_ref, cdr_ref, cdi_ref, ar_ref, ai_ref, d_ref, h0r_ref, h0i_ref,
              *rest, n, nkb, emit_y):
    if emit_y:
        y_ref, hr_ref, hi_ref, str_scr, sti_scr, xr_scr, xi_scr, rows_scr = rest
    else:
        hr_ref, hi_ref, str_scr, sti_scr, xr_scr, xi_scr, rows_scr = rest
    step = pl.program_id(0)
    kin = bdr_ref.shape[1]
    kst = bdr_ref.shape[2]

    @pl.when(step == 0)
    def _():
        str_scr[...] = h0r_ref[...]
        sti_scr[...] = h0i_ref[...]

    d_ssm = nkb * kin
    for r in range(SUBLANES):
        rows_scr[:, r, :] = u_ref[:, r * d_ssm:(r + 1) * d_ssm]
    u = rows_scr[...].reshape(n * SUBLANES, d_ssm)
    ub = u.astype(BF16)
    for kb in range(nkb):
        ukb = ub[:, kb * kin:(kb + 1) * kin]
        xr_scr[:, kb * kst:(kb + 1) * kst] = jnp.dot(ukb, bdr_ref[kb], preferred_element_type=F32)
        xi_scr[:, kb * kst:(kb + 1) * kst] = jnp.dot(ukb, bdi_ref[kb], preferred_element_type=F32)

    for kb in range(nkb):
        cs = slice(kb * kst, (kb + 1) * kst)
        ar = jnp.broadcast_to(ar_ref[:, cs], (SUBLANES, kst))
        ai = jnp.broadcast_to(ai_ref[:, cs], (SUBLANES, kst))

        def scan_step(tt, carry, cs=cs, ar=ar, ai=ai):
            xr, xi = carry
            r0 = pl.multiple_of(tt * SUBLANES, SUBLANES)
            nxr = ar * xr - ai * xi + xr_scr[pl.ds(r0, SUBLANES), cs]
            nxi = ar * xi + ai * xr + xi_scr[pl.ds(r0, SUBLANES), cs]
            xr_scr[pl.ds(r0, SUBLANES), cs] = nxr
            xi_scr[pl.ds(r0, SUBLANES), cs] = nxi
            return nxr, nxi

        xr, xi = lax.fori_loop(0, n, scan_step, (str_scr[:, cs], sti_scr[:, cs]), unroll=4)
        str_scr[:, cs] = xr
        sti_scr[:, cs] = xi

    if emit_y:
        ys = []
        for kb in range(nkb):
            cs = slice(kb * kst, (kb + 1) * kst)
            yk = (jnp.dot(xr_scr[:, cs].astype(BF16), cdr_ref[kb], preferred_element_type=F32)
                  - jnp.dot(xi_scr[:, cs].astype(BF16), cdi_ref[kb], preferred_element_type=F32))
            os_ = slice(kb * kin, (kb + 1) * kin)
            ys.append(yk + d_ref[:, os_] * u[:, os_])
        rows_scr[...] = jnp.concatenate(ys, axis=1).reshape(n, SUBLANES, d_ssm)
        for r in range(SUBLANES):
            y_ref[:, r * d_ssm:(r + 1) * d_ssm] = rows_scr[:, r, :]

    @pl.when(step == pl.num_programs(0) - 1)
    def _():
        hr_ref[...] = str_scr[...]
        hi_ref[...] = sti_scr[...]


def ssm_scan(u, coef, h0r, h0i, *, emit_y):
    bdr, bdi, cdr, cdi, ar, ai, dsk = coef
    steps, width = u.shape
    d_ssm = width // SUBLANES
    nkb, kin, kst = bdr.shape
    npst = nkb * kst
    n = math.gcd(steps, SSM_STEPS_PER_BLOCK)
    rows = n * SUBLANES
    full = lambda a: pl.BlockSpec(a.shape, lambda s: (0,) * a.ndim)
    out_specs = [pl.BlockSpec((SUBLANES, npst), lambda s: (0, 0))] * 2
    out_shape = [jax.ShapeDtypeStruct((SUBLANES, npst), F32)] * 2
    if emit_y:
        out_specs = [pl.BlockSpec((n, width), lambda s: (s, 0))] + out_specs
        out_shape = [jax.ShapeDtypeStruct((steps, width), F32)] + out_shape
    outs = pl.pallas_call(
        functools.partial(_ssm_body, n=n, nkb=nkb, emit_y=emit_y),
        grid=(steps // n,),
        in_specs=[pl.BlockSpec((n, width), lambda s: (s, 0)),
                  full(bdr), full(bdi), full(cdr), full(cdi), full(ar), full(ai), full(dsk),
                  full(h0r), full(h0i)],
        out_specs=out_specs,
        out_shape=out_shape,
        scratch_shapes=[pltpu.VMEM((SUBLANES, npst), F32), pltpu.VMEM((SUBLANES, npst), F32),
                        pltpu.VMEM((rows, npst), F32), pltpu.VMEM((rows, npst), F32),
                        pltpu.VMEM((n, SUBLANES, d_ssm), F32)],
        compiler_params=_cparams("arbitrary"),
        name="ssm_scan_y" if emit_y else "ssm_scan_state",
    )(u, bdr, bdi, cdr, cdi, ar, ai, dsk, h0r, h0i)
    if emit_y:
        return outs[0], outs[1], outs[2]
    return None, outs[0], outs[1]


def _ssm_coef(lam_re, lam_im, log_dt, b_re, b_im, c_re, c_im, d_skip):
    g, p = lam_re.shape
    dt = jnp.exp(log_dt.astype(F32))[:, None]
    lr, li = lam_re.astype(F32), lam_im.astype(F32)
    mag = jnp.exp(lr * dt)
    ar = mag * jnp.cos(li * dt)
    ai = mag * jnp.sin(li * dt)
    den = lr * lr + li * li
    nr = ar - 1.0
    cr = (nr * lr + ai * li) / den
    ci = (ai * lr - nr * li) / den
    br, bi = b_re.astype(F32), b_im.astype(F32)
    bbr = cr[..., None] * br - ci[..., None] * bi
    bbi = cr[..., None] * bi + ci[..., None] * br
    gb = min(SSM_KB_GROUPS, g)
    nkb = g // gb
    eye = jnp.eye(gb, dtype=F32)

    def pack_b(m):
        m = m.reshape(nkb, gb, p, GROUP)
        return jnp.einsum('kaph,ab->kahbp', m, eye).reshape(nkb, gb * GROUP, gb * p).astype(BF16)

    def pack_c(m):
        m = m.reshape(nkb, gb, GROUP, p)
        return jnp.einsum('kahp,ab->kapbh', m, eye).reshape(nkb, gb * p, gb * GROUP).astype(BF16)

    coef = (pack_b(bbr), pack_b(bbi), pack_c(c_re.astype(F32)), pack_c(c_im.astype(F32)),
            ar.reshape(1, g * p), ai.reshape(1, g * p), d_skip.astype(F32).reshape(1, g * GROUP))
    return coef, ar, ai


def _segment_inits(ar, ai, fr, fi, seg_len):
    k = int(round(math.log2(seg_len)))
    assert 2 ** k == seg_len
    pr, pi = ar.reshape(1, -1), ai.reshape(1, -1)
    for _ in range(k):
        pr, pi = pr * pr - pi * pi, 2.0 * pr * pi
    hr = jnp.zeros_like(fr[0:1])
    hi = jnp.zeros_like(fi[0:1])
    hrs, his = [hr], [hi]
    for r in range(SUBLANES - 1):
        hr, hi = pr * hr - pi * hi + fr[r:r + 1], pr * hi + pi * hr + fi[r:r + 1]
        hrs.append(hr)
        his.append(hi)
    return jnp.concatenate(hrs, axis=0), jnp.concatenate(his, axis=0)


def _merge_body(oa_ref, y_ref, ga_ref, gb_ref, ba_ref, bb_ref, wglu_ref, wpa_ref, wpb_ref, o_ref,
                ob_scr):
    n = pl.program_id(1)

    @pl.when(n == 0)
    def _():
        gy = jax.nn.gelu(y_ref[...])
        glu = jnp.dot(gy.astype(BF16), wglu_ref[...], preferred_element_type=F32)
        ob_scr[...] = (gy * jax.nn.sigmoid(glu)).astype(BF16)

    pa = jnp.dot(oa_ref[...], wpa_ref[...], preferred_element_type=F32)
    pb = jnp.dot(ob_scr[...], wpb_ref[...], preferred_element_type=F32)
    gate_a = jax.nn.sigmoid(ga_ref[...].astype(F32) + ba_ref[...])
    gate_b = jax.nn.sigmoid(gb_ref[...].astype(F32) + bb_ref[...])
    o_ref[...] = (gate_a * pa + gate_b * pb).astype(o_ref.dtype)


def mixer_merge(o_a, y, gates, b_gate, w_glu, w_pa, w_pb, l, *, y_permuted):
    t, d_attn = o_a.shape
    d_ssm = w_glu.shape[1]
    d = w_pa.shape[2]
    tm = _tile(t // SUBLANES if y_permuted else t, 512)
    tn = _tile(d, 1024)
    nd = d // tn
    if y_permuted:
        seg = t // SUBLANES
        mps = seg // tm
        assert y.shape == (seg, SUBLANES * d_ssm)
        y_spec = pl.BlockSpec((tm, d_ssm), lambda m, n: (m % mps, m // mps))
    else:
        y_spec = pl.BlockSpec((tm, d_ssm), lambda m, n: (m, 0))
    bg = b_gate.reshape(1, 2 * d)
    return pl.pallas_call(
        _merge_body,
        grid=(t // tm, nd),
        in_specs=[pl.BlockSpec((tm, d_attn), lambda m, n: (m, 0)),
                  y_spec,
                  pl.BlockSpec((tm, tn), lambda m, n: (m, n)),
                  pl.BlockSpec((tm, tn), lambda m, n: (m, nd + n)),
                  pl.BlockSpec((1, tn), lambda m, n: (0, n)),
                  pl.BlockSpec((1, tn), lambda m, n: (0, nd + n)),
                  pl.BlockSpec((None, d_ssm, d_ssm), lambda m, n: (l, 0, 0)),
                  pl.BlockSpec((None, d_attn, tn), lambda m, n: (l, 0, n)),
                  pl.BlockSpec((None, d_ssm, tn), lambda m, n: (l, 0, n))],
        out_specs=pl.BlockSpec((tm, tn), lambda m, n: (m, n)),
        out_shape=jax.ShapeDtypeStruct((t, d), BF16),
        scratch_shapes=[pltpu.VMEM((tm, d_ssm), BF16)],
        compiler_params=_cparams("arbitrary", "arbitrary"),
        name="mixer_merge",
    )(o_a, y, gates, gates, bg, bg, w_glu, w_pa, w_pb)


def _matmul_res_body(a_ref, w_ref, r_ref, o_ref):
    o_ref[...] = r_ref[...] + jnp.dot(a_ref[...], w_ref[...], preferred_element_type=F32)


def matmul_res(a, w, wi, res):
    t, k = a.shape
    n_out = w.shape[2]
    tm = _tile(t, 1024)
    tn = _tile(n_out, 512)
    return pl.pallas_call(
        _matmul_res_body,
        grid=(t // tm, n_out // tn),
        in_specs=[pl.BlockSpec((tm, k), lambda m, n: (m, 0)),
                  pl.BlockSpec((None, k, tn), lambda m, n: (wi, 0, n)),
                  pl.BlockSpec((tm, tn), lambda m, n: (m, n))],
        out_specs=pl.BlockSpec((tm, tn), lambda m, n: (m, n)),
        out_shape=jax.ShapeDtypeStruct((t, n_out), F32),
        compiler_params=_cparams("arbitrary", "arbitrary"),
        name="matmul_res",
    )(a, w, res)


def _router_top2(h, router_ref, rb_ref, n_experts):
    logits = jnp.dot(h, router_ref[...], preferred_element_type=F32) + rb_ref[...]
    lane = lax.broadcasted_iota(jnp.int32, logits.shape, 1).astype(F32)
    neg = jnp.float32(-jnp.inf)
    logits = jnp.where(lane < n_experts, logits, neg)
    big = jnp.float32(logits.shape[1])
    m1 = jnp.max(logits, axis=-1, keepdims=True)
    i1 = jnp.min(jnp.where(logits == m1, lane, big), axis=-1, keepdims=True)
    rest = jnp.where(lane == i1, neg, logits)
    m2 = jnp.max(rest, axis=-1, keepdims=True)
    i2 = jnp.min(jnp.where(rest == m2, lane, big), axis=-1, keepdims=True)
    e2 = jnp.exp(m2 - m1)
    g1 = 1.0 / (1.0 + e2)
    g2 = e2 / (1.0 + e2)
    return (jnp.where(lane == 0.0, i1, 0.0) + jnp.where(lane == 1.0, i2, 0.0)
            + jnp.where(lane == 2.0, g1, 0.0) + jnp.where(lane == 3.0, g2, 0.0))


def _ffn_up_body(x_ref, g_ref, w1_ref, w3_ref, o_ref, h_scr):
    @pl.when(pl.program_id(1) == 0)
    def _():
        _rmsnorm_rows(x_ref, g_ref, h_scr)

    h = h_scr[...]
    a = jnp.dot(h, w1_ref[...], preferred_element_type=F32)
    b = jnp.dot(h, w3_ref[...], preferred_element_type=F32)
    o_ref[...] = (jax.nn.silu(a) * b).astype(o_ref.dtype)


def ffn_up(x, g, w1, w3, wi):
    t, d = x.shape
    f = w1.shape[2]
    tm = _tile(t, 1024)
    tn = _tile(f, 512)
    w_spec = pl.BlockSpec((None, d, tn), lambda m, n: (wi, 0, n))
    return pl.pallas_call(
        _ffn_up_body,
        grid=(t // tm, f // tn),
        in_specs=[pl.BlockSpec((tm, d), lambda m, n: (m, 0), pipeline_mode=pl.Buffered(1)),
                  pl.BlockSpec((1, d), lambda m, n: (0, 0)), w_spec, w_spec],
        out_specs=pl.BlockSpec((tm, tn), lambda m, n: (m, n)),
        out_shape=jax.ShapeDtypeStruct((t, f), BF16),
        scratch_shapes=[pltpu.VMEM((tm, d), BF16)],
        compiler_params=_cparams("arbitrary", "arbitrary"),
        name="ffn_up",
    )(x, g.reshape(1, d), w1, w3)


MOE_ROWS = 256
ROUTE_LANES = 128


def _moe_route_body(x_ref, g_ref, router_ref, rb_ref, top_ref, h_scr, *, n_experts):
    _rmsnorm_rows(x_ref, g_ref, h_scr)
    top_ref[...] = _router_top2(h_scr[...], router_ref, rb_ref, n_experts)


def moe_route(x, g, router, router_b):
    t, d = x.shape
    n_experts = router.shape[1]
    tm = _tile(t, 512)
    rpad = jnp.zeros((d, ROUTE_LANES), BF16).at[:, :n_experts].set(router.astype(BF16))
    bpad = jnp.zeros((1, ROUTE_LANES), F32).at[0, :n_experts].set(router_b.astype(F32))
    return pl.pallas_call(
        functools.partial(_moe_route_body, n_experts=n_experts),
        grid=(t // tm,),
        in_specs=[pl.BlockSpec((tm, d), lambda m: (m, 0)), pl.BlockSpec((1, d), lambda m: (0, 0)),
                  pl.BlockSpec((d, ROUTE_LANES), lambda m: (0, 0)),
                  pl.BlockSpec((1, ROUTE_LANES), lambda m: (0, 0))],
        out_specs=pl.BlockSpec((tm, ROUTE_LANES), lambda m: (m, 0)),
        out_shape=jax.ShapeDtypeStruct((t, ROUTE_LANES), F32),
        scratch_shapes=[pltpu.VMEM((tm, d), BF16)],
        compiler_params=_cparams("arbitrary"),
        name="moe_route",
    )(x, g.reshape(1, d), rpad, bpad)


def _moe_plan(top, n_experts):
    t = top.shape[0]
    n_slots = TOP_K * t
    n_tiles = (n_slots + n_experts * (MOE_ROWS - 1)) // MOE_ROWS
    e = top[:, :TOP_K].astype(jnp.int32).reshape(n_slots)
    gate = top[:, TOP_K:2 * TOP_K].reshape(n_slots)
    slot = jnp.arange(n_slots, dtype=jnp.int32)
    order = jnp.argsort(e * n_slots + slot).astype(jnp.int32)
    counts = jnp.sum((e[:, None] == jnp.arange(n_experts)[None, :]).astype(jnp.int32), axis=0)
    padded = ((counts + MOE_ROWS - 1) // MOE_ROWS) * MOE_ROWS
    pend = jnp.cumsum(padded)
    pstart = pend - padded
    cstart = jnp.cumsum(counts) - counts
    row = jnp.arange(n_tiles * MOE_ROWS, dtype=jnp.int32)
    row_e = jnp.minimum(jnp.searchsorted(pend, row, side='right'), n_experts - 1).astype(jnp.int32)
    rank = row - pstart[row_e]
    real = rank < counts[row_e]
    row_slot = order[jnp.clip(cstart[row_e] + rank, 0, n_slots - 1)]
    row_token = jnp.where(real, row_slot // TOP_K, 0).astype(jnp.int32)
    row_dst = jnp.where(real, (row_slot % TOP_K) * t + row_slot // TOP_K, 0).astype(jnp.int32)
    row_gate = jnp.where(real, gate[row_slot], 0.0).reshape(-1, 1)
    tile0 = jnp.arange(n_tiles, dtype=jnp.int32) * MOE_ROWS
    tile_e = row_e[tile0]
    tile_rows = jnp.clip(pstart[tile_e] + counts[tile_e] - tile0, 0, MOE_ROWS)
    tile_rows = jnp.where(tile0 < pend[-1], tile_rows, 0).astype(jnp.int32)
    return tile_e, tile_rows, row_token, row_dst, row_gate


def _moe_expert_body(te_ref, tr_ref, tok_ref, dst_ref, x_hbm, g_ref, gate_ref, w1_ref, w3_ref, w2_ref,
                     y_hbm, xbuf, ybuf, gsem, ssem):
    rows = xbuf.shape[1]
    j = pl.program_id(0)
    last = pl.num_programs(0) - 1
    slot = lax.rem(j, 2)

    def gather_start(tile, s):
        def one(i, c):
            pltpu.make_async_copy(x_hbm.at[pl.ds(tok_ref[tile * rows + i], 1), :],
                                  xbuf.at[s, pl.ds(i, 1), :], gsem.at[s]).start()
            return c
        lax.fori_loop(0, rows, one, 0, unroll=8)

    def scatter_start(tile, s, n):
        def one(i, c):
            pltpu.make_async_copy(ybuf.at[s, pl.ds(i, 1), :],
                                  y_hbm.at[pl.ds(dst_ref[tile * rows + i], 1), :], ssem.at[s]).start()
            return c
        lax.fori_loop(0, n, one, 0)

    def scatter_wait(s, n):
        bit = rows
        while bit >= 1:
            @pl.when((n & bit) != 0)
            def _(bit=bit):
                pltpu.make_async_copy(ybuf.at[s, pl.ds(0, bit), :], y_hbm.at[pl.ds(0, bit), :],
                                      ssem.at[s]).wait()
            bit //= 2

    @pl.when((j == 0) & (tr_ref[0] > 0))
    def _():
        gather_start(0, 0)

    nxt = jnp.minimum(j + 1, last)

    @pl.when((j < last) & (tr_ref[nxt] > 0))
    def _():
        gather_start(nxt, 1 - slot)

    @pl.when(tr_ref[j] > 0)
    def _():
        pltpu.make_async_copy(x_hbm.at[pl.ds(0, rows), :], xbuf.at[slot], gsem.at[slot]).wait()
        h = _rmsnorm(xbuf[slot], g_ref[...]).astype(BF16)
        a = jnp.dot(h, w1_ref[...], preferred_element_type=F32)
        b = jnp.dot(h, w3_ref[...], preferred_element_type=F32)
        hid = (jax.nn.silu(a) * b).astype(BF16)
        ybuf[slot] = jnp.dot(hid, w2_ref[...], preferred_element_type=F32) * gate_ref[...]
        scatter_start(j, slot, tr_ref[j])

    prv = jnp.maximum(j - 1, 0)

    @pl.when(j >= 1)
    def _():
        scatter_wait(1 - slot, tr_ref[prv])

    @pl.when(j == last)
    def _():
        scatter_wait(slot, tr_ref[j])


def moe_experts(x, g, plan, w1, w3, w2, e0, wi2):
    tile_e, tile_rows, row_token, row_dst, row_gate = plan
    t, d = x.shape
    f = w1.shape[2]
    n_tiles = tile_e.shape[0]
    once = pl.Buffered(1)
    grid_spec = pltpu.PrefetchScalarGridSpec(
        num_scalar_prefetch=4,
        grid=(n_tiles,),
        in_specs=[pl.BlockSpec(memory_space=pl.ANY),
                  pl.BlockSpec((1, d), lambda j, te, tr, tok, dst: (0, 0)),
                  pl.BlockSpec((MOE_ROWS, 1), lambda j, te, tr, tok, dst: (j, 0)),
                  pl.BlockSpec((None, d, f), lambda j, te, tr, tok, dst: (e0 + te[j], 0, 0),
                               pipeline_mode=once),
                  pl.BlockSpec((None, d, f), lambda j, te, tr, tok, dst: (e0 + te[j], 0, 0),
                               pipeline_mode=once),
                  pl.BlockSpec((None, f, d), lambda j, te, tr, tok, dst: (wi2, te[j], 0),
                               pipeline_mode=once)],
        out_specs=pl.BlockSpec(memory_space=pl.ANY),
        scratch_shapes=[pltpu.VMEM((2, MOE_ROWS, d), F32), pltpu.VMEM((2, MOE_ROWS, d), F32),
                        pltpu.SemaphoreType.DMA((2,)), pltpu.SemaphoreType.DMA((2,))])
    return pl.pallas_call(
        _moe_expert_body,
        grid_spec=grid_spec,
        out_shape=jax.ShapeDtypeStruct((TOP_K * t, d), F32),
        compiler_params=_cparams("arbitrary"),
        name="moe_experts",
    )(tile_e, tile_rows, row_token, row_dst, x, g.reshape(1, d), row_gate, w1, w3, w2)


def _moe_combine_body(x_ref, ya_ref, yb_ref, *rest, norm):
    if norm:
        g_ref, o_ref = rest
    else:
        (o_ref,) = rest
    o_ref[...] = x_ref[...] + ya_ref[...] + yb_ref[...]
    if norm:
        _rmsnorm_rows(o_ref, g_ref, o_ref)


def moe_combine(x, y2, g=None):
    t, d = x.shape
    tm = _tile(t, 256)
    y3 = y2.reshape(TOP_K, t, d)
    in_specs = [pl.BlockSpec((tm, d), lambda m: (m, 0)),
                pl.BlockSpec((None, tm, d), lambda m: (0, m, 0)),
                pl.BlockSpec((None, tm, d), lambda m: (1, m, 0))]
    args = [x, y3, y3]
    if g is not None:
        in_specs.append(pl.BlockSpec((1, d), lambda m: (0, 0)))
        args.append(g.reshape(1, d))
    return pl.pallas_call(
        functools.partial(_moe_combine_body, norm=g is not None),
        grid=(t // tm,),
        in_specs=in_specs,
        out_specs=pl.BlockSpec((tm, d), lambda m: (m, 0)),
        out_shape=jax.ShapeDtypeStruct((t, d), F32),
        compiler_params=_cparams("arbitrary"),
        name="moe_combine",
    )(*args)


def _final_norm_body(x_ref, g_ref, o_ref):
    _rmsnorm_rows(x_ref, g_ref, o_ref)


def final_norm(x, g):
    t, d = x.shape
    tm = _tile(t, 512)
    return pl.pallas_call(
        _final_norm_body,
        grid=(t // tm,),
        in_specs=[pl.BlockSpec((tm, d), lambda m: (m, 0)), pl.BlockSpec((1, d), lambda m: (0, 0))],
        out_specs=pl.BlockSpec((tm, d), lambda m: (m, 0)),
        out_shape=jax.ShapeDtypeStruct((t, d), F32),
        compiler_params=_cparams("arbitrary"),
        name="final_norm",
    )(x, g.reshape(1, d))


def kernel(x_prompt, x_sample, cache_k, cache_v, state_ssm_re, state_ssm_im, norm1_g, norm2_g, w_in,
           lam_re, lam_im, log_dt, ssm_b_re, ssm_b_im, ssm_c_re, ssm_c_im, ssm_d, w_glu, w_pa, w_pb,
           b_gate, w_o, ffn_w1, ffn_w3, ffn_w2, moe_router, moe_router_b, moe_w1, moe_w3, moe_w2,
           final_g):
    bp, seq, d = x_prompt.shape
    db, ds, _ = x_sample.shape
    depth = w_in.shape[0]
    _, _, past, n_heads, head_dim = cache_k.shape
    d_attn = n_heads * head_dim
    n_groups, n_state = lam_re.shape[1:]
    d_ssm = n_groups * GROUP
    npst = n_groups * n_state
    assert bp == 1 and db == SUBLANES and seq % SUBLANES == 0

    xp = x_prompt.reshape(seq, d)
    xs = x_sample.reshape(db * ds, d)
    w_in_b, w_glu_b, w_pa_b, w_pb_b, w_o_b = (w.astype(BF16) for w in (w_in, w_glu, w_pa, w_pb, w_o))
    ffn_w1_b, ffn_w3_b, ffn_w2_b = (w.astype(BF16) for w in (ffn_w1, ffn_w3, ffn_w2))
    n_moe, n_exp, _, f_e = moe_w1.shape
    moe_w1_b = moe_w1.astype(BF16).reshape(n_moe * n_exp, d, f_e)
    moe_w3_b = moe_w3.astype(BF16).reshape(n_moe * n_exp, d, f_e)
    moe_w2_b = moe_w2.astype(BF16).reshape(n_moe, n_exp * f_e, d)
    outs = [[] for _ in range(8)]
    for l in range(depth):
        i = l // 2
        coef, ar, ai = _ssm_coef(lam_re[l], lam_im[l], log_dt[l], ssm_b_re[l], ssm_b_im[l],
                                 ssm_c_re[l], ssm_c_im[l], ssm_d[l])

        end_gain = final_g if l == depth - 1 else None

        def channel_mix(x):
            if l % 2 == 0:
                hid = ffn_up(x, norm2_g[l], ffn_w1_b, ffn_w3_b, i)
                x = matmul_res(hid, ffn_w2_b, i, x)
                return x if end_gain is None else final_norm(x, end_gain)
            plan = _moe_plan(moe_route(x, norm2_g[l], moe_router[i], moe_router_b[i]), n_exp)
            y2 = moe_experts(x, norm2_g[l], plan, moe_w1_b, moe_w3_b, moe_w2_b, i * n_exp, i)
            return moe_combine(x, y2, end_gain)

        q, k, v, kvb, u, gates = norm_proj(xp, norm1_g[l], w_in_b, l, d_attn=d_attn, d_ssm=d_ssm,
                                           head_dim=head_dim, permute_u=True)
        o_a = attention_prompt(q, kvb, n_heads=n_heads, head_dim=head_dim)
        zeros = jnp.zeros((SUBLANES, npst), F32)
        _, fr, fi = ssm_scan(u, coef, zeros, zeros, emit_y=False)
        h0r, h0i = _segment_inits(ar, ai, fr, fi, seq // SUBLANES)
        y, hr, hi = ssm_scan(u, coef, h0r, h0i, emit_y=True)
        merged = mixer_merge(o_a, y, gates, b_gate[l], w_glu_b, w_pa_b, w_pb_b, l, y_permuted=True)
        xp = matmul_res(merged, w_o_b, l, xp)
        xp = channel_mix(xp)
        outs[0].append(k)
        outs[1].append(v)
        outs[2].append(hr[SUBLANES - 1].reshape(1, n_groups, n_state))
        outs[3].append(hi[SUBLANES - 1].reshape(1, n_groups, n_state))

        q, k, v, kvb, u, gates = norm_proj(xs, norm1_g[l], w_in_b, l, d_attn=d_attn, d_ssm=d_ssm,
                                           head_dim=head_dim, permute_u=False)
        o_a = attention_sample(q, kvb, cache_k, cache_v, l, ds=ds)
        u_t = u.reshape(db, ds, d_ssm).transpose(1, 0, 2).reshape(ds, db * d_ssm)
        y_t, hr, hi = ssm_scan(u_t, coef, state_ssm_re[l].astype(F32).reshape(db, npst),
                               state_ssm_im[l].astype(F32).reshape(db, npst), emit_y=True)
        y = y_t.reshape(ds, db, d_ssm).transpose(1, 0, 2).reshape(db * ds, d_ssm)
        merged = mixer_merge(o_a, y, gates, b_gate[l], w_glu_b, w_pa_b, w_pb_b, l, y_permuted=False)
        xs = matmul_res(merged, w_o_b, l, xs)
        xs = channel_mix(xs)
        outs[4].append(k)
        outs[5].append(v)
        outs[6].append(hr.reshape(db, n_groups, n_state))
        outs[7].append(hi.reshape(db, n_groups, n_state))

    y_prompt = xp.reshape(bp, seq, d)
    y_sample = xs.reshape(db, ds, d)
    heads = dict(n_heads=n_heads, head_dim=head_dim)
    kv_outs = {0: stack_heads(outs[0], bp, **heads), 1: stack_heads(outs[1], bp, **heads),
               4: stack_heads(outs[4], db, **heads), 5: stack_heads(outs[5], db, **heads)}
    return (y_prompt, y_sample) + tuple(kv_outs[j] if j in kv_outs else jnp.stack(outs[j])
                                        for j in range(8))
```

```python
import functools
import math

import jax
import jax.numpy as jnp
from jax import lax
from jax.experimental import pallas as pl
from jax.experimental.pallas import tpu as pltpu

F32 = jnp.float32
BF16 = jnp.bfloat16

RMS_EPS = 1e-6
TOP_K = 2
GROUP = 16
SUBLANES = 8
V7X_VMEM_LIMIT_BYTES = 56 * 1024 * 1024
SSM_KB_GROUPS = 16
SSM_STEPS_PER_BLOCK = 64


def _cparams(*sem):
    return pltpu.CompilerParams(dimension_semantics=sem, vmem_limit_bytes=V7X_VMEM_LIMIT_BYTES)


def _tile(dim, pref):
    if dim <= pref:
        return dim
    t = pref
    while t >= 128:
        if dim % t == 0:
            return t
        t -= 128
    return dim


def _rmsnorm(x, g):
    ms = jnp.mean(x * x, axis=-1, keepdims=True)
    return (x * lax.rsqrt(ms + RMS_EPS)) * g


NORM_ROWS = 128


def _rmsnorm_rows(x_ref, g_ref, o_ref):
    rows = x_ref.shape[0]
    step = min(NORM_ROWS, rows)
    g = g_ref[...]

    def body(c, carry):
        r0 = pl.multiple_of(c * step, step)
        o_ref[pl.ds(r0, step), :] = _rmsnorm(x_ref[pl.ds(r0, step), :], g).astype(o_ref.dtype)
        return carry

    lax.fori_loop(0, rows // step, body, 0)


def _norm_proj_body(x_ref, g_ref, w_ref, q_ref, k_ref, v_ref, kvb_ref, u_ref, gate_ref, h_scr,
                    *, nq, ns, q_scale):
    n = pl.program_id(1)

    @pl.when(n == 0)
    def _():
        _rmsnorm_rows(x_ref, g_ref, h_scr)

    def z():
        return jnp.dot(h_scr[...], w_ref[...], preferred_element_type=F32)

    @pl.when(n < nq)
    def _():
        q_ref[...] = (z() * q_scale).astype(BF16)

    @pl.when((n >= nq) & (n < 2 * nq))
    def _():
        zz = z()
        k_ref[...] = zz
        kvb_ref[...] = zz.astype(BF16)

    @pl.when((n >= 2 * nq) & (n < 3 * nq))
    def _():
        zz = z()
        v_ref[...] = zz
        kvb_ref[...] = zz.astype(BF16)

    @pl.when((n >= 3 * nq) & (n < 3 * nq + ns))
    def _():
        u_ref[...] = z()

    @pl.when(n >= 3 * nq + ns)
    def _():
        gate_ref[...] = z().astype(BF16)


def norm_proj(x, g, w, l, *, d_attn, d_ssm, head_dim, permute_u):
    t, d = x.shape
    d_in = w.shape[2]
    tn = _tile(d_ssm, 512)
    assert d_attn % tn == 0 and d % tn == 0
    nq, ns = d_attn // tn, d_ssm // tn
    ng = 2 * d // tn
    assert d_in == (3 * nq + ns + ng) * tn
    tm = _tile(t // SUBLANES if permute_u else t, 1024)
    if permute_u:
        seg = t // SUBLANES
        assert seg % tm == 0
        mps = seg // tm
        u_shape = (seg, SUBLANES * d_ssm)
        u_map = lambda m, n: (m % mps, (m // mps) * ns + jnp.clip(n - 3 * nq, 0, ns - 1))
    else:
        u_shape = (t, d_ssm)
        u_map = lambda m, n: (m, jnp.clip(n - 3 * nq, 0, ns - 1))
    blk = lambda imap: pl.BlockSpec((tm, tn), imap)
    outs = pl.pallas_call(
        functools.partial(_norm_proj_body, nq=nq, ns=ns, q_scale=head_dim ** -0.5),
        grid=(t // tm, d_in // tn),
        in_specs=[pl.BlockSpec((tm, d), lambda m, n: (m, 0), pipeline_mode=pl.Buffered(1)),
                  pl.BlockSpec((1, d), lambda m, n: (0, 0)),
                  pl.BlockSpec((None, d, tn), lambda m, n: (l, 0, n))],
        out_specs=[blk(lambda m, n: (m, jnp.minimum(n, nq - 1))),
                   blk(lambda m, n: (m, jnp.clip(n - nq, 0, nq - 1))),
                   blk(lambda m, n: (m, jnp.clip(n - 2 * nq, 0, nq - 1))),
                   blk(lambda m, n: (m, jnp.clip(n - nq, 0, 2 * nq - 1))),
                   blk(u_map),
                   blk(lambda m, n: (m, jnp.clip(n - 3 * nq - ns, 0, ng - 1)))],
        out_shape=[jax.ShapeDtypeStruct((t, d_attn), BF16),
                   jax.ShapeDtypeStruct((t, d_attn), F32),
                   jax.ShapeDtypeStruct((t, d_attn), F32),
                   jax.ShapeDtypeStruct((t, 2 * d_attn), BF16),
                   jax.ShapeDtypeStruct(u_shape, F32),
                   jax.ShapeDtypeStruct((t, 2 * d), BF16)],
        scratch_shapes=[pltpu.VMEM((tm, d), BF16)],
        compiler_params=_cparams("arbitrary", "arbitrary"),
        name="norm_proj",
    )(x, g.reshape(1, d), w)
    return outs


def _softplus(z):
    return jnp.maximum(z, 0.0) + jnp.log(1.0 + jnp.exp(-jnp.abs(z)))


def _tri(tk):
    r = lax.broadcasted_iota(jnp.int32, (tk, tk), 0)
    c = lax.broadcasted_iota(jnp.int32, (tk, tk), 1)
    return (r >= c).astype(BF16)


def _sb_tile(q, kb, vb, tri, later, mask):
    z = lax.dot_general(q, kb, (((1,), (1,)), ((), ())), preferred_element_type=F32)
    sp = _softplus(z)
    if mask is not None:
        sp = jnp.where(mask, sp, 0.0)
    hi = sp.astype(BF16)
    lo = (sp - hi.astype(F32)).astype(BF16)
    within = (jnp.dot(hi, tri, preferred_element_type=F32)
              + jnp.dot(lo, tri, preferred_element_type=F32))
    w = jnp.exp(jnp.minimum(z - (within + later), 0.0))
    if mask is not None:
        w = jnp.where(mask, w, 0.0)
    pv = jnp.dot(w.astype(BF16), vb, preferred_element_type=F32)
    return pv, later + within[:, 0:1]


SB_ZERO_EXPONENT = 110.0
SB_NORM_SLACK = 1.001
KNORM_ROWS = 512


def _attn_prompt_body(q_ref, k_ref, v_ref, o_ref, acc_scr, kmax_scr, *, tq, tk):
    i = pl.program_id(1)
    nsub = tq // tk
    base = i * nsub
    tri = _tri(tk)
    later = jnp.zeros((tq, 1), F32)
    acc_scr[...] = jnp.zeros_like(acc_scr)

    @pl.when(i == 0)
    def _():
        step = min(KNORM_ROWS, k_ref.shape[0])

        def knorm(c, m):
            kf = k_ref[pl.ds(pl.multiple_of(c * step, step), step), :].astype(F32)
            return jnp.maximum(m, jnp.sum(kf * kf, axis=1, keepdims=True))

        m = lax.fori_loop(0, k_ref.shape[0] // step, knorm, jnp.zeros((step, 1), F32))
        kmax_scr[...] = jnp.broadcast_to(jnp.sqrt(jnp.max(m, axis=0, keepdims=True)), kmax_scr.shape)

    qf = q_ref[...].astype(F32)
    zmax = (jnp.sqrt(jnp.sum(qf * qf, axis=1, keepdims=True)) * kmax_scr[0:1, 0:1]) * SB_NORM_SLACK

    for jr in range(nsub - 1, -1, -1):
        r0 = jr * tk
        k0 = pl.multiple_of((base + jr) * tk, tk)
        row = lax.broadcasted_iota(jnp.int32, (tq - r0, tk), 0)
        col = lax.broadcasted_iota(jnp.int32, (tq - r0, tk), 1)
        pv, lat = _sb_tile(q_ref[r0:, :], k_ref[pl.ds(k0, tk), :], v_ref[pl.ds(k0, tk), :], tri,
                           later[r0:], col < row)
        acc_scr[r0:, :] += pv
        later = lat if r0 == 0 else jnp.concatenate([later[:r0], lat], axis=0)

    def live(later):
        return (jnp.min(later - zmax) <= SB_ZERO_EXPONENT).astype(jnp.int32)

    def cond(state):
        s, _, go = state
        return (s < base) & (go > 0)

    def body(state):
        s, later, _ = state
        j0 = pl.multiple_of((base - 1 - s) * tk, tk)
        pv, later = _sb_tile(q_ref[...], k_ref[pl.ds(j0, tk), :], v_ref[pl.ds(j0, tk), :], tri,
                             later, None)
        acc_scr[...] += pv
        return s + 1, later, live(later)

    lax.while_loop(cond, body, (jnp.int32(0), later, live(later)))
    o_ref[...] = acc_scr[...].astype(o_ref.dtype)


ATTN_Q_ROWS = 1024
ATTN_K_ROWS = 256


def attention_prompt(q, kvb, *, n_heads, head_dim):
    t = q.shape[0]
    tk = _tile(t, ATTN_K_ROWS)
    tq = _tile(t, ATTN_Q_ROWS)
    assert tq % tk == 0
    single = pl.Buffered(1)
    return pl.pallas_call(
        functools.partial(_attn_prompt_body, tq=tq, tk=tk),
        grid=(n_heads, t // tq),
        in_specs=[pl.BlockSpec((tq, head_dim), lambda h, i: (i, h)),
                  pl.BlockSpec((t, head_dim), lambda h, i: (0, h), pipeline_mode=single),
                  pl.BlockSpec((t, head_dim), lambda h, i: (0, n_heads + h), pipeline_mode=single)],
        out_specs=pl.BlockSpec((tq, head_dim), lambda h, i: (i, h)),
        out_shape=jax.ShapeDtypeStruct((t, n_heads * head_dim), BF16),
        scratch_shapes=[pltpu.VMEM((tq, head_dim), F32), pltpu.VMEM((SUBLANES, 128), F32)],
        compiler_params=_cparams("arbitrary", "arbitrary"),
        name="attn_prompt",
    )(q, kvb, kvb)


def _attn_sample_body(q_ref, kvn_ref, ck_ref, cv_ref, o_ref, *, ds, tk, n_heads, head_dim):
    row = lax.broadcasted_iota(jnp.int32, (ds, ds), 0)
    col = lax.broadcasted_iota(jnp.int32, (ds, ds), 1)
    tri_new, tri = _tri(ds), _tri(tk)
    past = ck_ref.shape[0]
    d_attn = n_heads * head_dim
    for h in range(n_heads):
        hs = slice(h * head_dim, (h + 1) * head_dim)
        q = q_ref[:, hs]
        acc, later = _sb_tile(q, kvn_ref[:, hs], kvn_ref[:, d_attn + h * head_dim:d_attn + (h + 1) * head_dim],
                              tri_new, jnp.zeros((ds, 1), F32), col < row)
        for j in range(past // tk - 1, -1, -1):
            kb = ck_ref[j * tk:(j + 1) * tk, h, :].astype(BF16)
            vb = cv_ref[j * tk:(j + 1) * tk, h, :].astype(BF16)
            pv, later = _sb_tile(q, kb, vb, tri, later, None)
            acc = acc + pv
        o_ref[:, hs] = acc.astype(o_ref.dtype)


def attention_sample(q, kvb, cache_k, cache_v, l, *, ds):
    _, b, past, n_heads, head_dim = cache_k.shape
    d_attn = n_heads * head_dim
    tk = _tile(past, ATTN_K_ROWS)
    cache_spec = pl.BlockSpec((None, None, past, n_heads, head_dim), lambda bi: (l, bi, 0, 0, 0))
    return pl.pallas_call(
        functools.partial(_attn_sample_body, ds=ds, tk=tk, n_heads=n_heads, head_dim=head_dim),
        grid=(b,),
        in_specs=[pl.BlockSpec((ds, d_attn), lambda bi: (bi, 0)),
                  pl.BlockSpec((ds, 2 * d_attn), lambda bi: (bi, 0)),
                  cache_spec, cache_spec],
        out_specs=pl.BlockSpec((ds, d_attn), lambda bi: (bi, 0)),
        out_shape=jax.ShapeDtypeStruct((b * ds, d_attn), BF16),
        compiler_params=_cparams("arbitrary"),
        name="attn_sample",
    )(q, kvb, cache_k, cache_v)


def _stack_heads_body(*refs, n_heads, head_dim):
    *in_refs, o_ref = refs
    l = pl.program_id(0)
    for i, x_ref in enumerate(in_refs):
        @pl.when(l == i)
        def _(x_ref=x_ref):
            for h in range(n_heads):
                o_ref[:, h, :] = x_ref[:, h * head_dim:(h + 1) * head_dim]


def stack_heads(xs, batch, *, n_heads, head_dim):
    depth = len(xs)
    t = xs[0].shape[0]
    tm = _tile(t, 512)
    nm = t // tm
    in_specs = [pl.BlockSpec((tm, n_heads * head_dim),
                             lambda l, m, i=i: (jnp.clip(m + (l - i) * nm, 0, nm - 1), 0))
                for i in range(depth)]
    out = pl.pallas_call(
        functools.partial(_stack_heads_body, n_heads=n_heads, head_dim=head_dim),
        grid=(depth, nm),
        in_specs=in_specs,
        out_specs=pl.BlockSpec((None, None, tm, n_heads, head_dim), lambda l, m: (l, 0, m, 0, 0)),
        out_shape=jax.ShapeDtypeStruct((depth, 1, t, n_heads, head_dim), F32),
        compiler_params=_cparams("arbitrary", "arbitrary"),
        name="stack_heads",
    )(*xs)
    return out.reshape(depth, batch, t // batch, n_heads, head_dim)


def _ssm_body(u_ref, bdr_ref, bdi_ref, cdr_ref, cdi_ref, ar_ref, ai_ref, d_ref, h0r_ref, h0i_ref,
              *rest, n, nkb, emit_y):
    if emit_y:
        y_ref, hr_ref, hi_ref, str_scr, sti_scr, xr_scr, xi_scr, rows_scr = rest
    else:
        hr_ref, hi_ref, str_scr, sti_scr, xr_scr, xi_scr, rows_scr = rest
    step = pl.program_id(0)
    kin = bdr_ref.shape[1]
    kst = bdr_ref.shape[2]

    @pl.when(step == 0)
    def _():
        str_scr[...] = h0r_ref[...]
        sti_scr[...] = h0i_ref[...]

    d_ssm = nkb * kin
    for r in range(SUBLANES):
        rows_scr[:, r, :] = u_ref[:, r * d_ssm:(r + 1) * d_ssm]
    u = rows_scr[...].reshape(n * SUBLANES, d_ssm)
    ub = u.astype(BF16)
    for kb in range(nkb):
        ukb = ub[:, kb * kin:(kb + 1) * kin]
        xr_scr[:, kb * kst:(kb + 1) * kst] = jnp.dot(ukb, bdr_ref[kb], preferred_element_type=F32)
        xi_scr[:, kb * kst:(kb + 1) * kst] = jnp.dot(ukb, bdi_ref[kb], preferred_element_type=F32)

    for kb in range(nkb):
        cs = slice(kb * kst, (kb + 1) * kst)
        ar = jnp.broadcast_to(ar_ref[:, cs], (SUBLANES, kst))
        ai = jnp.broadcast_to(ai_ref[:, cs], (SUBLANES, kst))

        def scan_step(tt, carry, cs=cs, ar=ar, ai=ai):
            xr, xi = carry
            r0 = pl.multiple_of(tt * SUBLANES, SUBLANES)
            nxr = ar * xr - ai * xi + xr_scr[pl.ds(r0, SUBLANES), cs]
            nxi = ar * xi + ai * xr + xi_scr[pl.ds(r0, SUBLANES), cs]
            xr_scr[pl.ds(r0, SUBLANES), cs] = nxr
            xi_scr[pl.ds(r0, SUBLANES), cs] = nxi
            return nxr, nxi

        xr, xi = lax.fori_loop(0, n, scan_step, (str_scr[:, cs], sti_scr[:, cs]), unroll=4)
        str_scr[:, cs] = xr
        sti_scr[:, cs] = xi

    if emit_y:
        ys = []
        for kb in range(nkb):
            cs = slice(kb * kst, (kb + 1) * kst)
            yk = (jnp.dot(xr_scr[:, cs].astype(BF16), cdr_ref[kb], preferred_element_type=F32)
                  - jnp.dot(xi_scr[:, cs].astype(BF16), cdi_ref[kb], preferred_element_type=F32))
            os_ = slice(kb * kin, (kb + 1) * kin)
            ys.append(yk + d_ref[:, os_] * u[:, os_])
        rows_scr[...] = jnp.concatenate(ys, axis=1).reshape(n, SUBLANES, d_ssm)
        for r in range(SUBLANES):
            y_ref[:, r * d_ssm:(r + 1) * d_ssm] = rows_scr[:, r, :]

    @pl.when(step == pl.num_programs(0) - 1)
    def _():
        hr_ref[...] = str_scr[...]
        hi_ref[...] = sti_scr[...]


def ssm_scan(u, coef, h0r, h0i, *, emit_y):
    bdr, bdi, cdr, cdi, ar, ai, dsk = coef
    steps, width = u.shape
    d_ssm = width // SUBLANES
    nkb, kin, kst = bdr.shape
    npst = nkb * kst
    n = math.gcd(steps, SSM_STEPS_PER_BLOCK)
    rows = n * SUBLANES
    full = lambda a: pl.BlockSpec(a.shape, lambda s: (0,) * a.ndim)
    out_specs = [pl.BlockSpec((SUBLANES, npst), lambda s: (0, 0))] * 2
    out_shape = [jax.ShapeDtypeStruct((SUBLANES, npst), F32)] * 2
    if emit_y:
        out_specs = [pl.BlockSpec((n, width), lambda s: (s, 0))] + out_specs
        out_shape = [jax.ShapeDtypeStruct((steps, width), F32)] + out_shape
    outs = pl.pallas_call(
        functools.partial(_ssm_body, n=n, nkb=nkb, emit_y=emit_y),
        grid=(steps // n,),
        in_specs=[pl.BlockSpec((n, width), lambda s: (s, 0)),
                  full(bdr), full(bdi), full(cdr), full(cdi), full(ar), full(ai), full(dsk),
                  full(h0r), full(h0i)],
        out_specs=out_specs,
        out_shape=out_shape,
        scratch_shapes=[pltpu.VMEM((SUBLANES, npst), F32), pltpu.VMEM((SUBLANES, npst), F32),
                        pltpu.VMEM((rows, npst), F32), pltpu.VMEM((rows, npst), F32),
                        pltpu.VMEM((n, SUBLANES, d_ssm), F32)],
        compiler_params=_cparams("arbitrary"),
        name="ssm_scan_y" if emit_y else "ssm_scan_state",
    )(u, bdr, bdi, cdr, cdi, ar, ai, dsk, h0r, h0i)
    if emit_y:
        return outs[0], outs[1], outs[2]
    return None, outs[0], outs[1]


def _ssm_coef(lam_re, lam_im, log_dt, b_re, b_im, c_re, c_im, d_skip):
    g, p = lam_re.shape
    dt = jnp.exp(log_dt.astype(F32))[:, None]
    lr, li = lam_re.astype(F32), lam_im.astype(F32)
    mag = jnp.exp(lr * dt)
    ar = mag * jnp.cos(li * dt)
    ai = mag * jnp.sin(li * dt)
    den = lr * lr + li * li
    nr = ar - 1.0
    cr = (nr * lr + ai * li) / den
    ci = (ai * lr - nr * li) / den
    br, bi = b_re.astype(F32), b_im.astype(F32)
    bbr = cr[..., None] * br - ci[..., None] * bi
    bbi = cr[..., None] * bi + ci[..., None] * br
    gb = min(SSM_KB_GROUPS, g)
    nkb = g // gb
    eye = jnp.eye(gb, dtype=F32)

    def pack_b(m):
        m = m.reshape(nkb, gb, p, GROUP)
        return jnp.einsum('kaph,ab->kahbp', m, eye).reshape(nkb, gb * GROUP, gb * p).astype(BF16)

    def pack_c(m):
        m = m.reshape(nkb, gb, GROUP, p)
        return jnp.einsum('kahp,ab->kapbh', m, eye).reshape(nkb, gb * p, gb * GROUP).astype(BF16)

    coef = (pack_b(bbr), pack_b(bbi), pack_c(c_re.astype(F32)), pack_c(c_im.astype(F32)),
            ar.reshape(1, g * p), ai.reshape(1, g * p), d_skip.astype(F32).reshape(1, g * GROUP))
    return coef, ar, ai


def _segment_inits(ar, ai, fr, fi, seg_len):
    k = int(round(math.log2(seg_len)))
    assert 2 ** k == seg_len
    pr, pi = ar.reshape(1, -1), ai.reshape(1, -1)
    for _ in range(k):
        pr, pi = pr * pr - pi * pi, 2.0 * pr * pi
    hr = jnp.zeros_like(fr[0:1])
    hi = jnp.zeros_like(fi[0:1])
    hrs, his = [hr], [hi]
    for r in range(SUBLANES - 1):
        hr, hi = pr * hr - pi * hi + fr[r:r + 1], pr * hi + pi * hr + fi[r:r + 1]
        hrs.append(hr)
        his.append(hi)
    return jnp.concatenate(hrs, axis=0), jnp.concatenate(his, axis=0)


def _merge_body(oa_ref, y_ref, ga_ref, gb_ref, ba_ref, bb_ref, wglu_ref, wpa_ref, wpb_ref, o_ref,
                ob_scr):
    n = pl.program_id(1)

    @pl.when(n == 0)
    def _():
        gy = jax.nn.gelu(y_ref[...])
        glu = jnp.dot(gy.astype(BF16), wglu_ref[...], preferred_element_type=F32)
        ob_scr[...] = (gy * jax.nn.sigmoid(glu)).astype(BF16)

    pa = jnp.dot(oa_ref[...], wpa_ref[...], preferred_element_type=F32)
    pb = jnp.dot(ob_scr[...], wpb_ref[...], preferred_element_type=F32)
    gate_a = jax.nn.sigmoid(ga_ref[...].astype(F32) + ba_ref[...])
    gate_b = jax.nn.sigmoid(gb_ref[...].astype(F32) + bb_ref[...])
    o_ref[...] = (gate_a * pa + gate_b * pb).astype(o_ref.dtype)


def mixer_merge(o_a, y, gates, b_gate, w_glu, w_pa, w_pb, l, *, y_permuted):
    t, d_attn = o_a.shape
    d_ssm = w_glu.shape[1]
    d = w_pa.shape[2]
    tm = _tile(t // SUBLANES if y_permuted else t, 1024)
    tn = _tile(d, 1024)
    nd = d // tn
    if y_permuted:
        seg = t // SUBLANES
        mps = seg // tm
        assert y.shape == (seg, SUBLANES * d_ssm)
        y_spec = pl.BlockSpec((tm, d_ssm), lambda m, n: (m % mps, m // mps))
    else:
        y_spec = pl.BlockSpec((tm, d_ssm), lambda m, n: (m, 0))
    bg = b_gate.reshape(1, 2 * d)
    return pl.pallas_call(
        _merge_body,
        grid=(t // tm, nd),
        in_specs=[pl.BlockSpec((tm, d_attn), lambda m, n: (m, 0)),
                  y_spec,
                  pl.BlockSpec((tm, tn), lambda m, n: (m, n)),
                  pl.BlockSpec((tm, tn), lambda m, n: (m, nd + n)),
                  pl.BlockSpec((1, tn), lambda m, n: (0, n)),
                  pl.BlockSpec((1, tn), lambda m, n: (0, nd + n)),
                  pl.BlockSpec((None, d_ssm, d_ssm), lambda m, n: (l, 0, 0)),
                  pl.BlockSpec((None, d_attn, tn), lambda m, n: (l, 0, n)),
                  pl.BlockSpec((None, d_ssm, tn), lambda m, n: (l, 0, n))],
        out_specs=pl.BlockSpec((tm, tn), lambda m, n: (m, n)),
        out_shape=jax.ShapeDtypeStruct((t, d), BF16),
        scratch_shapes=[pltpu.VMEM((tm, d_ssm), BF16)],
        compiler_params=_cparams("arbitrary", "arbitrary"),
        name="mixer_merge",
    )(o_a, y, gates, gates, bg, bg, w_glu, w_pa, w_pb)


def _matmul_res_body(a_ref, w_ref, r_ref, o_ref):
    o_ref[...] = r_ref[...] + jnp.dot(a_ref[...], w_ref[...], preferred_element_type=F32)


def matmul_res(a, w, wi, res):
    t, k = a.shape
    n_out = w.shape[2]
    tm = _tile(t, 1024)
    tn = _tile(n_out, 512)
    return pl.pallas_call(
        _matmul_res_body,
        grid=(t // tm, n_out // tn),
        in_specs=[pl.BlockSpec((tm, k), lambda m, n: (m, 0)),
                  pl.BlockSpec((None, k, tn), lambda m, n: (wi, 0, n)),
                  pl.BlockSpec((tm, tn), lambda m, n: (m, n))],
        out_specs=pl.BlockSpec((tm, tn), lambda m, n: (m, n)),
        out_shape=jax.ShapeDtypeStruct((t, n_out), F32),
        compiler_params=_cparams("arbitrary", "arbitrary"),
        name="matmul_res",
    )(a, w, res)


def _router_top2(h, router_ref, rb_ref, n_experts):
    logits = jnp.dot(h, router_ref[...], preferred_element_type=F32) + rb_ref[...]
    lane = lax.broadcasted_iota(jnp.int32, logits.shape, 1).astype(F32)
    neg = jnp.float32(-jnp.inf)
    logits = jnp.where(lane < n_experts, logits, neg)
    big = jnp.float32(logits.shape[1])
    m1 = jnp.max(logits, axis=-1, keepdims=True)
    i1 = jnp.min(jnp.where(logits == m1, lane, big), axis=-1, keepdims=True)
    rest = jnp.where(lane == i1, neg, logits)
    m2 = jnp.max(rest, axis=-1, keepdims=True)
    i2 = jnp.min(jnp.where(rest == m2, lane, big), axis=-1, keepdims=True)
    e2 = jnp.exp(m2 - m1)
    g1 = 1.0 / (1.0 + e2)
    g2 = e2 / (1.0 + e2)
    return (jnp.where(lane == 0.0, i1, 0.0) + jnp.where(lane == 1.0, i2, 0.0)
            + jnp.where(lane == 2.0, g1, 0.0) + jnp.where(lane == 3.0, g2, 0.0))


def _ffn_up_body(x_ref, g_ref, w1_ref, w3_ref, o_ref, h_scr):
    @pl.when(pl.program_id(1) == 0)
    def _():
        _rmsnorm_rows(x_ref, g_ref, h_scr)

    h = h_scr[...]
    a = jnp.dot(h, w1_ref[...], preferred_element_type=F32)
    b = jnp.dot(h, w3_ref[...], preferred_element_type=F32)
    o_ref[...] = (jax.nn.silu(a) * b).astype(o_ref.dtype)


def ffn_up(x, g, w1, w3, wi):
    t, d = x.shape
    f = w1.shape[2]
    tm = _tile(t, 1024)
    tn = _tile(f, 512)
    w_spec = pl.BlockSpec((None, d, tn), lambda m, n: (wi, 0, n))
    return pl.pallas_call(
        _ffn_up_body,
        grid=(t // tm, f // tn),
        in_specs=[pl.BlockSpec((tm, d), lambda m, n: (m, 0), pipeline_mode=pl.Buffered(1)),
                  pl.BlockSpec((1, d), lambda m, n: (0, 0)), w_spec, w_spec],
        out_specs=pl.BlockSpec((tm, tn), lambda m, n: (m, n)),
        out_shape=jax.ShapeDtypeStruct((t, f), BF16),
        scratch_shapes=[pltpu.VMEM((tm, d), BF16)],
        compiler_params=_cparams("arbitrary", "arbitrary"),
        name="ffn_up",
    )(x, g.reshape(1, d), w1, w3)


MOE_ROWS = 256
ROUTE_LANES = 128


def _moe_route_body(x_ref, g_ref, router_ref, rb_ref, top_ref, h_scr, *, n_experts):
    _rmsnorm_rows(x_ref, g_ref, h_scr)
    top_ref[...] = _router_top2(h_scr[...], router_ref, rb_ref, n_experts)


def moe_route(x, g, router, router_b):
    t, d = x.shape
    n_experts = router.shape[1]
    tm = _tile(t, 512)
    rpad = jnp.zeros((d, ROUTE_LANES), BF16).at[:, :n_experts].set(router.astype(BF16))
    bpad = jnp.zeros((1, ROUTE_LANES), F32).at[0, :n_experts].set(router_b.astype(F32))
    return pl.pallas_call(
        functools.partial(_moe_route_body, n_experts=n_experts),
        grid=(t // tm,),
        in_specs=[pl.BlockSpec((tm, d), lambda m: (m, 0)), pl.BlockSpec((1, d), lambda m: (0, 0)),
                  pl.BlockSpec((d, ROUTE_LANES), lambda m: (0, 0)),
                  pl.BlockSpec((1, ROUTE_LANES), lambda m: (0, 0))],
        out_specs=pl.BlockSpec((tm, ROUTE_LANES), lambda m: (m, 0)),
        out_shape=jax.ShapeDtypeStruct((t, ROUTE_LANES), F32),
        scratch_shapes=[pltpu.VMEM((tm, d), BF16)],
        compiler_params=_cparams("arbitrary"),
        name="moe_route",
    )(x, g.reshape(1, d), rpad, bpad)


def _moe_plan(top, n_experts):
    t = top.shape[0]
    n_slots = TOP_K * t
    n_tiles = (n_slots + n_experts * (MOE_ROWS - 1)) // MOE_ROWS
    e = top[:, :TOP_K].astype(jnp.int32).reshape(n_slots)
    gate = top[:, TOP_K:2 * TOP_K].reshape(n_slots)
    slot = jnp.arange(n_slots, dtype=jnp.int32)
    order = jnp.argsort(e * n_slots + slot).astype(jnp.int32)
    counts = jnp.sum((e[:, None] == jnp.arange(n_experts)[None, :]).astype(jnp.int32), axis=0)
    padded = ((counts + MOE_ROWS - 1) // MOE_ROWS) * MOE_ROWS
    pend = jnp.cumsum(padded)
    pstart = pend - padded
    cstart = jnp.cumsum(counts) - counts
    row = jnp.arange(n_tiles * MOE_ROWS, dtype=jnp.int32)
    row_e = jnp.minimum(jnp.sum((row[:, None] >= pend[None, :]).astype(jnp.int32), axis=1), n_experts - 1)
    rank = row - pstart[row_e]
    real = rank < counts[row_e]
    row_slot = order[jnp.clip(cstart[row_e] + rank, 0, n_slots - 1)]
    row_token = jnp.where(real, row_slot // TOP_K, 0).astype(jnp.int32)
    row_dst = jnp.where(real, (row_slot % TOP_K) * t + row_slot // TOP_K, 0).astype(jnp.int32)
    row_gate = jnp.where(real, gate[row_slot], 0.0).reshape(-1, 1)
    tile0 = jnp.arange(n_tiles, dtype=jnp.int32) * MOE_ROWS
    tile_e = row_e[tile0]
    tile_rows = jnp.clip(pstart[tile_e] + counts[tile_e] - tile0, 0, MOE_ROWS)
    tile_rows = jnp.where(tile0 < pend[-1], tile_rows, 0).astype(jnp.int32)
    return tile_e, tile_rows, row_token, row_dst, row_gate


def _moe_expert_body(te_ref, tr_ref, tok_ref, dst_ref, x_hbm, g_ref, gate_ref, w1_ref, w3_ref, w2_ref,
                     y_hbm, xbuf, ybuf, gsem, ssem):
    rows = xbuf.shape[1]
    j = pl.program_id(0)
    last = pl.num_programs(0) - 1
    slot = lax.rem(j, 2)

    def gather_start(tile, s):
        def one(i, c):
            pltpu.make_async_copy(x_hbm.at[pl.ds(tok_ref[tile * rows + i], 1), :],
                                  xbuf.at[s, pl.ds(i, 1), :], gsem.at[s]).start()
            return c
        lax.fori_loop(0, rows, one, 0, unroll=8)

    def scatter_start(tile, s, n):
        def one(i, c):
            pltpu.make_async_copy(ybuf.at[s, pl.ds(i, 1), :],
                                  y_hbm.at[pl.ds(dst_ref[tile * rows + i], 1), :], ssem.at[s]).start()
            return c
        lax.fori_loop(0, n, one, 0)

    def scatter_wait(s, n):
        bit = rows
        while bit >= 1:
            @pl.when((n & bit) != 0)
            def _(bit=bit):
                pltpu.make_async_copy(ybuf.at[s, pl.ds(0, bit), :], y_hbm.at[pl.ds(0, bit), :],
                                      ssem.at[s]).wait()
            bit //= 2

    @pl.when((j == 0) & (tr_ref[0] > 0))
    def _():
        gather_start(0, 0)

    nxt = jnp.minimum(j + 1, last)

    @pl.when((j < last) & (tr_ref[nxt] > 0))
    def _():
        gather_start(nxt, 1 - slot)

    @pl.when(tr_ref[j] > 0)
    def _():
        pltpu.make_async_copy(x_hbm.at[pl.ds(0, rows), :], xbuf.at[slot], gsem.at[slot]).wait()
        h = _rmsnorm(xbuf[slot], g_ref[...]).astype(BF16)
        a = jnp.dot(h, w1_ref[...], preferred_element_type=F32)
        b = jnp.dot(h, w3_ref[...], preferred_element_type=F32)
        hid = (jax.nn.silu(a) * b).astype(BF16)
        ybuf[slot] = (jnp.dot(hid, w2_ref[...], preferred_element_type=F32) * gate_ref[...]
                      + 0.5 * xbuf[slot])
        scatter_start(j, slot, tr_ref[j])

    prv = jnp.maximum(j - 1, 0)

    @pl.when(j >= 1)
    def _():
        scatter_wait(1 - slot, tr_ref[prv])

    @pl.when(j == last)
    def _():
        scatter_wait(slot, tr_ref[j])


def moe_experts(x, g, plan, w1, w3, w2, e0, wi2):
    tile_e, tile_rows, row_token, row_dst, row_gate = plan
    t, d = x.shape
    f = w1.shape[2]
    n_tiles = tile_e.shape[0]
    once = pl.Buffered(1)
    grid_spec = pltpu.PrefetchScalarGridSpec(
        num_scalar_prefetch=4,
        grid=(n_tiles,),
        in_specs=[pl.BlockSpec(memory_space=pl.ANY),
                  pl.BlockSpec((1, d), lambda j, te, tr, tok, dst: (0, 0)),
                  pl.BlockSpec((MOE_ROWS, 1), lambda j, te, tr, tok, dst: (j, 0)),
                  pl.BlockSpec((None, d, f), lambda j, te, tr, tok, dst: (e0 + te[j], 0, 0),
                               pipeline_mode=once),
                  pl.BlockSpec((None, d, f), lambda j, te, tr, tok, dst: (e0 + te[j], 0, 0),
                               pipeline_mode=once),
                  pl.BlockSpec((None, f, d), lambda j, te, tr, tok, dst: (wi2, te[j], 0),
                               pipeline_mode=once)],
        out_specs=pl.BlockSpec(memory_space=pl.ANY),
        scratch_shapes=[pltpu.VMEM((2, MOE_ROWS, d), F32), pltpu.VMEM((2, MOE_ROWS, d), F32),
                        pltpu.SemaphoreType.DMA((2,)), pltpu.SemaphoreType.DMA((2,))])
    return pl.pallas_call(
        _moe_expert_body,
        grid_spec=grid_spec,
        out_shape=jax.ShapeDtypeStruct((TOP_K * t, d), F32),
        compiler_params=_cparams("arbitrary"),
        name="moe_experts",
    )(tile_e, tile_rows, row_token, row_dst, x, g.reshape(1, d), row_gate, w1, w3, w2)


def _moe_combine_body(ya_ref, yb_ref, *rest, norm):
    if norm:
        g_ref, o_ref = rest
    else:
        (o_ref,) = rest
    o_ref[...] = ya_ref[...] + yb_ref[...]
    if norm:
        _rmsnorm_rows(o_ref, g_ref, o_ref)


def moe_combine(y2, g=None):
    t, d = y2.shape[0] // TOP_K, y2.shape[1]
    tm = _tile(t, 512)
    y3 = y2.reshape(TOP_K, t, d)
    in_specs = [pl.BlockSpec((None, tm, d), lambda m: (0, m, 0)),
                pl.BlockSpec((None, tm, d), lambda m: (1, m, 0))]
    args = [y3, y3]
    if g is not None:
        in_specs.append(pl.BlockSpec((1, d), lambda m: (0, 0)))
        args.append(g.reshape(1, d))
    return pl.pallas_call(
        functools.partial(_moe_combine_body, norm=g is not None),
        grid=(t // tm,),
        in_specs=in_specs,
        out_specs=pl.BlockSpec((tm, d), lambda m: (m, 0)),
        out_shape=jax.ShapeDtypeStruct((t, d), F32),
        compiler_params=_cparams("arbitrary"),
        name="moe_combine",
    )(*args)


def _final_norm_body(x_ref, g_ref, o_ref):
    _rmsnorm_rows(x_ref, g_ref, o_ref)


def final_norm(x, g):
    t, d = x.shape
    tm = _tile(t, 512)
    return pl.pallas_call(
        _final_norm_body,
        grid=(t // tm,),
        in_specs=[pl.BlockSpec((tm, d), lambda m: (m, 0)), pl.BlockSpec((1, d), lambda m: (0, 0))],
        out_specs=pl.BlockSpec((tm, d), lambda m: (m, 0)),
        out_shape=jax.ShapeDtypeStruct((t, d), F32),
        compiler_params=_cparams("arbitrary"),
        name="final_norm",
    )(x, g.reshape(1, d))


def kernel(x_prompt, x_sample, cache_k, cache_v, state_ssm_re, state_ssm_im, norm1_g, norm2_g, w_in,
           lam_re, lam_im, log_dt, ssm_b_re, ssm_b_im, ssm_c_re, ssm_c_im, ssm_d, w_glu, w_pa, w_pb,
           b_gate, w_o, ffn_w1, ffn_w3, ffn_w2, moe_router, moe_router_b, moe_w1, moe_w3, moe_w2,
           final_g):
    bp, seq, d = x_prompt.shape
    db, ds, _ = x_sample.shape
    depth = w_in.shape[0]
    _, _, past, n_heads, head_dim = cache_k.shape
    d_attn = n_heads * head_dim
    n_groups, n_state = lam_re.shape[1:]
    d_ssm = n_groups * GROUP
    npst = n_groups * n_state
    assert bp == 1 and db == SUBLANES and seq % SUBLANES == 0

    xp = x_prompt.reshape(seq, d)
    xs = x_sample.reshape(db * ds, d)
    w_in_b, w_glu_b, w_pa_b, w_pb_b, w_o_b = (w.astype(BF16) for w in (w_in, w_glu, w_pa, w_pb, w_o))
    ffn_w1_b, ffn_w3_b, ffn_w2_b = (w.astype(BF16) for w in (ffn_w1, ffn_w3, ffn_w2))
    n_moe, n_exp, _, f_e = moe_w1.shape
    moe_w1_b = moe_w1.astype(BF16).reshape(n_moe * n_exp, d, f_e)
    moe_w3_b = moe_w3.astype(BF16).reshape(n_moe * n_exp, d, f_e)
    moe_w2_b = moe_w2.astype(BF16).reshape(n_moe, n_exp * f_e, d)
    outs = [[] for _ in range(8)]
    for l in range(depth):
        i = l // 2
        coef, ar, ai = _ssm_coef(lam_re[l], lam_im[l], log_dt[l], ssm_b_re[l], ssm_b_im[l],
                                 ssm_c_re[l], ssm_c_im[l], ssm_d[l])

        end_gain = final_g if l == depth - 1 else None

        def channel_mix(x):
            if l % 2 == 0:
                hid = ffn_up(x, norm2_g[l], ffn_w1_b, ffn_w3_b, i)
                x = matmul_res(hid, ffn_w2_b, i, x)
                return x if end_gain is None else final_norm(x, end_gain)
            plan = _moe_plan(moe_route(x, norm2_g[l], moe_router[i], moe_router_b[i]), n_exp)
            y2 = moe_experts(x, norm2_g[l], plan, moe_w1_b, moe_w3_b, moe_w2_b, i * n_exp, i)
            return moe_combine(y2, end_gain)

        q, k, v, kvb, u, gates = norm_proj(xp, norm1_g[l], w_in_b, l, d_attn=d_attn, d_ssm=d_ssm,
                                           head_dim=head_dim, permute_u=True)
        o_a = attention_prompt(q, kvb, n_heads=n_heads, head_dim=head_dim)
        zeros = jnp.zeros((SUBLANES, npst), F32)
        _, fr, fi = ssm_scan(u, coef, zeros, zeros, emit_y=False)
        h0r, h0i = _segment_inits(ar, ai, fr, fi, seq // SUBLANES)
        y, hr, hi = ssm_scan(u, coef, h0r, h0i, emit_y=True)
        merged = mixer_merge(o_a, y, gates, b_gate[l], w_glu_b, w_pa_b, w_pb_b, l, y_permuted=True)
        xp = matmul_res(merged, w_o_b, l, xp)
        xp = channel_mix(xp)
        outs[0].append(k)
        outs[1].append(v)
        outs[2].append(hr[SUBLANES - 1].reshape(1, n_groups, n_state))
        outs[3].append(hi[SUBLANES - 1].reshape(1, n_groups, n_state))

        q, k, v, kvb, u, gates = norm_proj(xs, norm1_g[l], w_in_b, l, d_attn=d_attn, d_ssm=d_ssm,
                                           head_dim=head_dim, permute_u=False)
        o_a = attention_sample(q, kvb, cache_k, cache_v, l, ds=ds)
        u_t = u.reshape(db, ds, d_ssm).transpose(1, 0, 2).reshape(ds, db * d_ssm)
        y_t, hr, hi = ssm_scan(u_t, coef, state_ssm_re[l].astype(F32).reshape(db, npst),
                               state_ssm_im[l].astype(F32).reshape(db, npst), emit_y=True)
        y = y_t.reshape(ds, db, d_ssm).transpose(1, 0, 2).reshape(db * ds, d_ssm)
        merged = mixer_merge(o_a, y, gates, b_gate[l], w_glu_b, w_pa_b, w_pb_b, l, y_permuted=False)
        xs = matmul_res(merged, w_o_b, l, xs)
        xs = channel_mix(xs)
        outs[4].append(k)
        outs[5].append(v)
        outs[6].append(hr.reshape(db, n_groups, n_state))
        outs[7].append(hi.reshape(db, n_groups, n_state))

    y_prompt = xp.reshape(bp, seq, d)
    y_sample = xs.reshape(db, ds, d)
    heads = dict(n_heads=n_heads, head_dim=head_dim)
    kv_outs = {0: stack_heads(outs[0], bp, **heads), 1: stack_heads(outs[1], bp, **heads),
               4: stack_heads(outs[4], db, **heads), 5: stack_heads(outs[5], db, **heads)}
    return (y_prompt, y_sample) + tuple(kv_outs[j] if j in kv_outs else jnp.stack(outs[j])
                                        for j in range(8))
```

```python
import functools
import math

import jax
import jax.numpy as jnp
from jax import lax
from jax.experimental import pallas as pl
from jax.experimental.pallas import tpu as pltpu

F32 = jnp.float32
BF16 = jnp.bfloat16

RMS_EPS = 1e-6
TOP_K = 2
GROUP = 16
SUBLANES = 8
V7X_VMEM_LIMIT_BYTES = 56 * 1024 * 1024
SSM_KB_GROUPS = 16
SSM_STEPS_PER_BLOCK = 64


def _cparams(*sem):
    return pltpu.CompilerParams(dimension_semantics=sem, vmem_limit_bytes=V7X_VMEM_LIMIT_BYTES)


def _tile(dim, pref):
    if dim <= pref:
        return dim
    t = pref
    while t >= 128:
        if dim % t == 0:
            return t
        t -= 128
    return dim


def _rmsnorm(x, g):
    ms = jnp.mean(x * x, axis=-1, keepdims=True)
    return (x * lax.rsqrt(ms + RMS_EPS)) * g


NORM_ROWS = 128


def _rmsnorm_rows(x_ref, g_ref, o_ref):
    rows = x_ref.shape[0]
    step = min(NORM_ROWS, rows)
    g = g_ref[...]

    def body(c, carry):
        r0 = pl.multiple_of(c * step, step)
        o_ref[pl.ds(r0, step), :] = _rmsnorm(x_ref[pl.ds(r0, step), :], g).astype(o_ref.dtype)
        return carry

    lax.fori_loop(0, rows // step, body, 0)


def _load_norm_rows(x_hbm, g_ref, xbuf, sem, h_scr):
    m, n = pl.program_id(0), pl.program_id(1)
    tm = xbuf.shape[0]

    def copy(mi):
        return pltpu.make_async_copy(x_hbm.at[pl.ds(pl.multiple_of(mi * tm, tm), tm), :], xbuf, sem)

    @pl.when((m == 0) & (n == 0))
    def _():
        copy(0).start()

    @pl.when(n == 0)
    def _():
        copy(m).wait()
        _rmsnorm_rows(xbuf, g_ref, h_scr)

    @pl.when((n == 1) & (m + 1 < pl.num_programs(0)))
    def _():
        copy(m + 1).start()


def _norm_proj_body(x_hbm, g_ref, w_ref, q_ref, k_ref, v_ref, kvb_ref, u_ref, gate_ref, h_scr, xbuf, sem,
                    *, nq, ns, q_scale):
    n = pl.program_id(1)
    _load_norm_rows(x_hbm, g_ref, xbuf, sem, h_scr)

    def z():
        return jnp.dot(h_scr[...], w_ref[...], preferred_element_type=F32)

    @pl.when(n < nq)
    def _():
        q_ref[...] = (z() * q_scale).astype(BF16)

    @pl.when((n >= nq) & (n < 2 * nq))
    def _():
        zz = z()
        k_ref[...] = zz
        kvb_ref[...] = zz.astype(BF16)

    @pl.when((n >= 2 * nq) & (n < 3 * nq))
    def _():
        zz = z()
        v_ref[...] = zz
        kvb_ref[...] = zz.astype(BF16)

    @pl.when((n >= 3 * nq) & (n < 3 * nq + ns))
    def _():
        u_ref[...] = z()

    @pl.when(n >= 3 * nq + ns)
    def _():
        gate_ref[...] = z().astype(BF16)


def norm_proj(x, g, w, l, *, d_attn, d_ssm, head_dim, permute_u):
    t, d = x.shape
    d_in = w.shape[2]
    tn = _tile(d_ssm, 512)
    assert d_attn % tn == 0 and d % tn == 0
    nq, ns = d_attn // tn, d_ssm // tn
    ng = 2 * d // tn
    assert d_in == (3 * nq + ns + ng) * tn
    tm = _tile(t // SUBLANES if permute_u else t, 1024)
    if permute_u:
        seg = t // SUBLANES
        assert seg % tm == 0
        mps = seg // tm
        u_shape = (seg, SUBLANES * d_ssm)
        u_map = lambda m, n: (m % mps, (m // mps) * ns + jnp.clip(n - 3 * nq, 0, ns - 1))
    else:
        u_shape = (t, d_ssm)
        u_map = lambda m, n: (m, jnp.clip(n - 3 * nq, 0, ns - 1))
    blk = lambda imap: pl.BlockSpec((tm, tn), imap)
    outs = pl.pallas_call(
        functools.partial(_norm_proj_body, nq=nq, ns=ns, q_scale=head_dim ** -0.5),
        grid=(t // tm, d_in // tn),
        in_specs=[pl.BlockSpec(memory_space=pl.ANY),
                  pl.BlockSpec((1, d), lambda m, n: (0, 0)),
                  pl.BlockSpec((None, d, tn), lambda m, n: (l, 0, n))],
        out_specs=[blk(lambda m, n: (m, jnp.minimum(n, nq - 1))),
                   blk(lambda m, n: (m, jnp.clip(n - nq, 0, nq - 1))),
                   blk(lambda m, n: (m, jnp.clip(n - 2 * nq, 0, nq - 1))),
                   blk(lambda m, n: (m, jnp.clip(n - nq, 0, 2 * nq - 1))),
                   blk(u_map),
                   blk(lambda m, n: (m, jnp.clip(n - 3 * nq - ns, 0, ng - 1)))],
        out_shape=[jax.ShapeDtypeStruct((t, d_attn), BF16),
                   jax.ShapeDtypeStruct((t, d_attn), F32),
                   jax.ShapeDtypeStruct((t, d_attn), F32),
                   jax.ShapeDtypeStruct((t, 2 * d_attn), BF16),
                   jax.ShapeDtypeStruct(u_shape, F32),
                   jax.ShapeDtypeStruct((t, 2 * d), BF16)],
        scratch_shapes=[pltpu.VMEM((tm, d), BF16), pltpu.VMEM((tm, d), F32), pltpu.SemaphoreType.DMA(())],
        compiler_params=_cparams("arbitrary", "arbitrary"),
        name="norm_proj",
    )(x, g.reshape(1, d), w)
    return outs


def _softplus(z):
    return jnp.maximum(z, 0.0) + jnp.log(1.0 + jnp.exp(-jnp.abs(z)))


def _tri(tk):
    r = lax.broadcasted_iota(jnp.int32, (tk, tk), 0)
    c = lax.broadcasted_iota(jnp.int32, (tk, tk), 1)
    return (r >= c).astype(BF16)


def _sb_tile(q, kb, vb, tri, later, mask):
    z = lax.dot_general(q, kb, (((1,), (1,)), ((), ())), preferred_element_type=F32)
    sp = _softplus(z)
    if mask is not None:
        sp = jnp.where(mask, sp, 0.0)
    hi = sp.astype(BF16)
    lo = (sp - hi.astype(F32)).astype(BF16)
    within = (jnp.dot(hi, tri, preferred_element_type=F32)
              + jnp.dot(lo, tri, preferred_element_type=F32))
    w = jnp.exp(jnp.minimum(z - (within + later), 0.0))
    if mask is not None:
        w = jnp.where(mask, w, 0.0)
    pv = jnp.dot(w.astype(BF16), vb, preferred_element_type=F32)
    return pv, later + within[:, 0:1]


SB_ZERO_EXPONENT = 110.0
SB_NORM_SLACK = 1.001
KNORM_ROWS = 512


def _attn_prompt_body(q_ref, k_ref, v_ref, o_ref, acc_scr, kmax_scr, *, tq, tk):
    i = pl.program_id(1)
    nsub = tq // tk
    base = i * nsub
    tri = _tri(tk)
    later = jnp.zeros((tq, 1), F32)
    acc_scr[...] = jnp.zeros_like(acc_scr)

    @pl.when(i == 0)
    def _():
        step = min(KNORM_ROWS, k_ref.shape[0])

        def knorm(c, m):
            kf = k_ref[pl.ds(pl.multiple_of(c * step, step), step), :].astype(F32)
            return jnp.maximum(m, jnp.sum(kf * kf, axis=1, keepdims=True))

        m = lax.fori_loop(0, k_ref.shape[0] // step, knorm, jnp.zeros((step, 1), F32))
        kmax_scr[...] = jnp.broadcast_to(jnp.sqrt(jnp.max(m, axis=0, keepdims=True)), kmax_scr.shape)

    qf = q_ref[...].astype(F32)
    zmax = (jnp.sqrt(jnp.sum(qf * qf, axis=1, keepdims=True)) * kmax_scr[0:1, 0:1]) * SB_NORM_SLACK

    for jr in range(nsub - 1, -1, -1):
        r0 = jr * tk
        k0 = pl.multiple_of((base + jr) * tk, tk)
        row = lax.broadcasted_iota(jnp.int32, (tq - r0, tk), 0)
        col = lax.broadcasted_iota(jnp.int32, (tq - r0, tk), 1)
        pv, lat = _sb_tile(q_ref[r0:, :], k_ref[pl.ds(k0, tk), :], v_ref[pl.ds(k0, tk), :], tri,
                           later[r0:], col < row)
        acc_scr[r0:, :] += pv
        later = lat if r0 == 0 else jnp.concatenate([later[:r0], lat], axis=0)

    def live(later):
        return (jnp.min(later - zmax) <= SB_ZERO_EXPONENT).astype(jnp.int32)

    def cond(state):
        s, _, go = state
        return (s < base) & (go > 0)

    def body(state):
        s, later, _ = state
        j0 = pl.multiple_of((base - 1 - s) * tk, tk)
        pv, later = _sb_tile(q_ref[...], k_ref[pl.ds(j0, tk), :], v_ref[pl.ds(j0, tk), :], tri,
                             later, None)
        acc_scr[...] += pv
        return s + 1, later, live(later)

    lax.while_loop(cond, body, (jnp.int32(0), later, live(later)))
    o_ref[...] = acc_scr[...].astype(o_ref.dtype)


ATTN_Q_ROWS = 1024
ATTN_K_ROWS = 256


def attention_prompt(q, kvb, *, n_heads, head_dim):
    t = q.shape[0]
    tk = _tile(t, ATTN_K_ROWS)
    tq = _tile(t, ATTN_Q_ROWS)
    assert tq % tk == 0
    single = pl.Buffered(1)
    return pl.pallas_call(
        functools.partial(_attn_prompt_body, tq=tq, tk=tk),
        grid=(n_heads, t // tq),
        in_specs=[pl.BlockSpec((tq, head_dim), lambda h, i: (i, h)),
                  pl.BlockSpec((t, head_dim), lambda h, i: (0, h), pipeline_mode=single),
                  pl.BlockSpec((t, head_dim), lambda h, i: (0, n_heads + h), pipeline_mode=single)],
        out_specs=pl.BlockSpec((tq, head_dim), lambda h, i: (i, h)),
        out_shape=jax.ShapeDtypeStruct((t, n_heads * head_dim), BF16),
        scratch_shapes=[pltpu.VMEM((tq, head_dim), F32), pltpu.VMEM((SUBLANES, 128), F32)],
        compiler_params=_cparams("arbitrary", "arbitrary"),
        name="attn_prompt",
    )(q, kvb, kvb)


def _attn_sample_body(q_ref, kvn_ref, ck_ref, cv_ref, o_ref, *, ds, tk, n_heads, head_dim):
    row = lax.broadcasted_iota(jnp.int32, (ds, ds), 0)
    col = lax.broadcasted_iota(jnp.int32, (ds, ds), 1)
    tri_new, tri = _tri(ds), _tri(tk)
    past = ck_ref.shape[0]
    d_attn = n_heads * head_dim
    for h in range(n_heads):
        hs = slice(h * head_dim, (h + 1) * head_dim)
        q = q_ref[:, hs]
        acc, later = _sb_tile(q, kvn_ref[:, hs], kvn_ref[:, d_attn + h * head_dim:d_attn + (h + 1) * head_dim],
                              tri_new, jnp.zeros((ds, 1), F32), col < row)
        for j in range(past // tk - 1, -1, -1):
            kb = ck_ref[j * tk:(j + 1) * tk, h, :].astype(BF16)
            vb = cv_ref[j * tk:(j + 1) * tk, h, :].astype(BF16)
            pv, later = _sb_tile(q, kb, vb, tri, later, None)
            acc = acc + pv
        o_ref[:, hs] = acc.astype(o_ref.dtype)


def attention_sample(q, kvb, cache_k, cache_v, l, *, ds):
    _, b, past, n_heads, head_dim = cache_k.shape
    d_attn = n_heads * head_dim
    tk = _tile(past, ATTN_K_ROWS)
    cache_spec = pl.BlockSpec((None, None, past, n_heads, head_dim), lambda bi: (l, bi, 0, 0, 0))
    return pl.pallas_call(
        functools.partial(_attn_sample_body, ds=ds, tk=tk, n_heads=n_heads, head_dim=head_dim),
        grid=(b,),
        in_specs=[pl.BlockSpec((ds, d_attn), lambda bi: (bi, 0)),
                  pl.BlockSpec((ds, 2 * d_attn), lambda bi: (bi, 0)),
                  cache_spec, cache_spec],
        out_specs=pl.BlockSpec((ds, d_attn), lambda bi: (bi, 0)),
        out_shape=jax.ShapeDtypeStruct((b * ds, d_attn), BF16),
        compiler_params=_cparams("arbitrary"),
        name="attn_sample",
    )(q, kvb, cache_k, cache_v)


def _stack_heads_body(*refs, n_heads, head_dim):
    *in_refs, o_ref = refs
    l = pl.program_id(0)
    for i, x_ref in enumerate(in_refs):
        @pl.when(l == i)
        def _(x_ref=x_ref):
            for h in range(n_heads):
                o_ref[:, h, :] = x_ref[:, h * head_dim:(h + 1) * head_dim]


def stack_heads(xs, batch, *, n_heads, head_dim):
    depth = len(xs)
    t = xs[0].shape[0]
    tm = _tile(t, 512)
    nm = t // tm
    in_specs = [pl.BlockSpec((tm, n_heads * head_dim),
                             lambda l, m, i=i: (jnp.clip(m + (l - i) * nm, 0, nm - 1), 0))
                for i in range(depth)]
    out = pl.pallas_call(
        functools.partial(_stack_heads_body, n_heads=n_heads, head_dim=head_dim),
        grid=(depth, nm),
        in_specs=in_specs,
        out_specs=pl.BlockSpec((None, None, tm, n_heads, head_dim), lambda l, m: (l, 0, m, 0, 0)),
        out_shape=jax.ShapeDtypeStruct((depth, 1, t, n_heads, head_dim), F32),
        compiler_params=_cparams("arbitrary", "arbitrary"),
        name="stack_heads",
    )(*xs)
    return out.reshape(depth, batch, t // batch, n_heads, head_dim)


def _ssm_body(u_ref, bdr_ref, bdi_ref, cdr_ref, cdi_ref, ar_ref, ai_ref, d_ref, h0r_ref, h0i_ref,
              *rest, n, nkb, emit_y):
    if emit_y:
        y_ref, hr_ref, hi_ref, str_scr, sti_scr, xr_scr, xi_scr, rows_scr = rest
    else:
        hr_ref, hi_ref, str_scr, sti_scr, xr_scr, xi_scr, rows_scr = rest
    step = pl.program_id(0)
    kin = bdr_ref.shape[1]
    kst = bdr_ref.shape[2]

    @pl.when(step == 0)
    def _():
        str_scr[...] = h0r_ref[...]
        sti_scr[...] = h0i_ref[...]

    d_ssm = nkb * kin
    for r in range(SUBLANES):
        rows_scr[:, r, :] = u_ref[:, r * d_ssm:(r + 1) * d_ssm]
    u = rows_scr[...].reshape(n * SUBLANES, d_ssm)
    ub = u.astype(BF16)
    for kb in range(nkb):
        ukb = ub[:, kb * kin:(kb + 1) * kin]
        xr_scr[:, kb * kst:(kb + 1) * kst] = jnp.dot(ukb, bdr_ref[kb], preferred_element_type=F32)
        xi_scr[:, kb * kst:(kb + 1) * kst] = jnp.dot(ukb, bdi_ref[kb], preferred_element_type=F32)

    for kb in range(nkb):
        cs = slice(kb * kst, (kb + 1) * kst)
        ar = jnp.broadcast_to(ar_ref[:, cs], (SUBLANES, kst))
        ai = jnp.broadcast_to(ai_ref[:, cs], (SUBLANES, kst))

        def scan_step(tt, carry, cs=cs, ar=ar, ai=ai):
            xr, xi = carry
            r0 = pl.multiple_of(tt * SUBLANES, SUBLANES)
            nxr = ar * xr - ai * xi + xr_scr[pl.ds(r0, SUBLANES), cs]
            nxi = ar * xi + ai * xr + xi_scr[pl.ds(r0, SUBLANES), cs]
            xr_scr[pl.ds(r0, SUBLANES), cs] = nxr
            xi_scr[pl.ds(r0, SUBLANES), cs] = nxi
            return nxr, nxi

        xr, xi = lax.fori_loop(0, n, scan_step, (str_scr[:, cs], sti_scr[:, cs]), unroll=4)
        str_scr[:, cs] = xr
        sti_scr[:, cs] = xi

    if emit_y:
        ys = []
        for kb in range(nkb):
            cs = slice(kb * kst, (kb + 1) * kst)
            yk = (jnp.dot(xr_scr[:, cs].astype(BF16), cdr_ref[kb], preferred_element_type=F32)
                  - jnp.dot(xi_scr[:, cs].astype(BF16), cdi_ref[kb], preferred_element_type=F32))
            os_ = slice(kb * kin, (kb + 1) * kin)
            ys.append(yk + d_ref[:, os_] * u[:, os_])
        rows_scr[...] = jnp.concatenate(ys, axis=1).reshape(n, SUBLANES, d_ssm)
        for r in range(SUBLANES):
            y_ref[:, r * d_ssm:(r + 1) * d_ssm] = rows_scr[:, r, :]

    @pl.when(step == pl.num_programs(0) - 1)
    def _():
        hr_ref[...] = str_scr[...]
        hi_ref[...] = sti_scr[...]


def ssm_scan(u, coef, h0r, h0i, *, emit_y):
    bdr, bdi, cdr, cdi, ar, ai, dsk = coef
    steps, width = u.shape
    d_ssm = width // SUBLANES
    nkb, kin, kst = bdr.shape
    npst = nkb * kst
    n = math.gcd(steps, SSM_STEPS_PER_BLOCK)
    rows = n * SUBLANES
    full = lambda a: pl.BlockSpec(a.shape, lambda s: (0,) * a.ndim)
    out_specs = [pl.BlockSpec((SUBLANES, npst), lambda s: (0, 0))] * 2
    out_shape = [jax.ShapeDtypeStruct((SUBLANES, npst), F32)] * 2
    if emit_y:
        out_specs = [pl.BlockSpec((n, width), lambda s: (s, 0))] + out_specs
        out_shape = [jax.ShapeDtypeStruct((steps, width), F32)] + out_shape
    outs = pl.pallas_call(
        functools.partial(_ssm_body, n=n, nkb=nkb, emit_y=emit_y),
        grid=(steps // n,),
        in_specs=[pl.BlockSpec((n, width), lambda s: (s, 0)),
                  full(bdr), full(bdi), full(cdr), full(cdi), full(ar), full(ai), full(dsk),
                  full(h0r), full(h0i)],
        out_specs=out_specs,
        out_shape=out_shape,
        scratch_shapes=[pltpu.VMEM((SUBLANES, npst), F32), pltpu.VMEM((SUBLANES, npst), F32),
                        pltpu.VMEM((rows, npst), F32), pltpu.VMEM((rows, npst), F32),
                        pltpu.VMEM((n, SUBLANES, d_ssm), F32)],
        compiler_params=_cparams("arbitrary"),
        name="ssm_scan_y" if emit_y else "ssm_scan_state",
    )(u, bdr, bdi, cdr, cdi, ar, ai, dsk, h0r, h0i)
    if emit_y:
        return outs[0], outs[1], outs[2]
    return None, outs[0], outs[1]


def _ssm_coef(lam_re, lam_im, log_dt, b_re, b_im, c_re, c_im, d_skip):
    g, p = lam_re.shape
    dt = jnp.exp(log_dt.astype(F32))[:, None]
    lr, li = lam_re.astype(F32), lam_im.astype(F32)
    mag = jnp.exp(lr * dt)
    ar = mag * jnp.cos(li * dt)
    ai = mag * jnp.sin(li * dt)
    den = lr * lr + li * li
    nr = ar - 1.0
    cr = (nr * lr + ai * li) / den
    ci = (ai * lr - nr * li) / den
    br, bi = b_re.astype(F32), b_im.astype(F32)
    bbr = cr[..., None] * br - ci[..., None] * bi
    bbi = cr[..., None] * bi + ci[..., None] * br
    gb = min(SSM_KB_GROUPS, g)
    nkb = g // gb
    eye = jnp.eye(gb, dtype=F32)

    def pack_b(m):
        m = m.reshape(nkb, gb, p, GROUP)
        return jnp.einsum('kaph,ab->kahbp', m, eye).reshape(nkb, gb * GROUP, gb * p).astype(BF16)

    def pack_c(m):
        m = m.reshape(nkb, gb, GROUP, p)
        return jnp.einsum('kahp,ab->kapbh', m, eye).reshape(nkb, gb * p, gb * GROUP).astype(BF16)

    coef = (pack_b(bbr), pack_b(bbi), pack_c(c_re.astype(F32)), pack_c(c_im.astype(F32)),
            ar.reshape(1, g * p), ai.reshape(1, g * p), d_skip.astype(F32).reshape(1, g * GROUP))
    return coef, ar, ai


def _segment_inits(ar, ai, fr, fi, seg_len):
    k = int(round(math.log2(seg_len)))
    assert 2 ** k == seg_len
    pr, pi = ar.reshape(1, -1), ai.reshape(1, -1)
    for _ in range(k):
        pr, pi = pr * pr - pi * pi, 2.0 * pr * pi
    hr = jnp.zeros_like(fr[0:1])
    hi = jnp.zeros_like(fi[0:1])
    hrs, his = [hr], [hi]
    for r in range(SUBLANES - 1):
        hr, hi = pr * hr - pi * hi + fr[r:r + 1], pr * hi + pi * hr + fi[r:r + 1]
        hrs.append(hr)
        his.append(hi)
    return jnp.concatenate(hrs, axis=0), jnp.concatenate(his, axis=0)


def _merge_body(oa_ref, y_ref, ga_ref, gb_ref, ba_ref, bb_ref, wglu_ref, wpa_ref, wpb_ref, o_ref,
                ob_scr):
    n = pl.program_id(1)

    @pl.when(n == 0)
    def _():
        gy = jax.nn.gelu(y_ref[...])
        glu = jnp.dot(gy.astype(BF16), wglu_ref[...], preferred_element_type=F32)
        ob_scr[...] = (gy * jax.nn.sigmoid(glu)).astype(BF16)

    pa = jnp.dot(oa_ref[...], wpa_ref[...], preferred_element_type=F32)
    pb = jnp.dot(ob_scr[...], wpb_ref[...], preferred_element_type=F32)
    gate_a = jax.nn.sigmoid(ga_ref[...].astype(F32) + ba_ref[...])
    gate_b = jax.nn.sigmoid(gb_ref[...].astype(F32) + bb_ref[...])
    o_ref[...] = (gate_a * pa + gate_b * pb).astype(o_ref.dtype)


def mixer_merge(o_a, y, gates, b_gate, w_glu, w_pa, w_pb, l, *, y_permuted):
    t, d_attn = o_a.shape
    d_ssm = w_glu.shape[1]
    d = w_pa.shape[2]
    tm = _tile(t // SUBLANES if y_permuted else t, 1024)
    tn = _tile(d, 1024)
    nd = d // tn
    if y_permuted:
        seg = t // SUBLANES
        mps = seg // tm
        assert y.shape == (seg, SUBLANES * d_ssm)
        y_spec = pl.BlockSpec((tm, d_ssm), lambda m, n: (m % mps, m // mps))
    else:
        y_spec = pl.BlockSpec((tm, d_ssm), lambda m, n: (m, 0))
    bg = b_gate.reshape(1, 2 * d)
    return pl.pallas_call(
        _merge_body,
        grid=(t // tm, nd),
        in_specs=[pl.BlockSpec((tm, d_attn), lambda m, n: (m, 0)),
                  y_spec,
                  pl.BlockSpec((tm, tn), lambda m, n: (m, n)),
                  pl.BlockSpec((tm, tn), lambda m, n: (m, nd + n)),
                  pl.BlockSpec((1, tn), lambda m, n: (0, n)),
                  pl.BlockSpec((1, tn), lambda m, n: (0, nd + n)),
                  pl.BlockSpec((None, d_ssm, d_ssm), lambda m, n: (l, 0, 0)),
                  pl.BlockSpec((None, d_attn, tn), lambda m, n: (l, 0, n)),
                  pl.BlockSpec((None, d_ssm, tn), lambda m, n: (l, 0, n))],
        out_specs=pl.BlockSpec((tm, tn), lambda m, n: (m, n)),
        out_shape=jax.ShapeDtypeStruct((t, d), BF16),
        scratch_shapes=[pltpu.VMEM((tm, d_ssm), BF16)],
        compiler_params=_cparams("arbitrary", "arbitrary"),
        name="mixer_merge",
    )(o_a, y, gates, gates, bg, bg, w_glu, w_pa, w_pb)


def _matmul_res_body(a_ref, w_ref, r_ref, o_ref):
    o_ref[...] = r_ref[...] + jnp.dot(a_ref[...], w_ref[...], preferred_element_type=F32)


def matmul_res(a, w, wi, res):
    t, k = a.shape
    n_out = w.shape[2]
    tm = _tile(t, 1024)
    tn = _tile(n_out, 512)
    return pl.pallas_call(
        _matmul_res_body,
        grid=(t // tm, n_out // tn),
        in_specs=[pl.BlockSpec((tm, k), lambda m, n: (m, 0)),
                  pl.BlockSpec((None, k, tn), lambda m, n: (wi, 0, n)),
                  pl.BlockSpec((tm, tn), lambda m, n: (m, n))],
        out_specs=pl.BlockSpec((tm, tn), lambda m, n: (m, n)),
        out_shape=jax.ShapeDtypeStruct((t, n_out), F32),
        compiler_params=_cparams("arbitrary", "arbitrary"),
        name="matmul_res",
    )(a, w, res)


def _router_top2(h, router_ref, rb_ref, n_experts):
    logits = jnp.dot(h, router_ref[...], preferred_element_type=F32) + rb_ref[...]
    lane = lax.broadcasted_iota(jnp.int32, logits.shape, 1).astype(F32)
    neg = jnp.float32(-jnp.inf)
    logits = jnp.where(lane < n_experts, logits, neg)
    big = jnp.float32(logits.shape[1])
    m1 = jnp.max(logits, axis=-1, keepdims=True)
    i1 = jnp.min(jnp.where(logits == m1, lane, big), axis=-1, keepdims=True)
    rest = jnp.where(lane == i1, neg, logits)
    m2 = jnp.max(rest, axis=-1, keepdims=True)
    i2 = jnp.min(jnp.where(rest == m2, lane, big), axis=-1, keepdims=True)
    e2 = jnp.exp(m2 - m1)
    g1 = 1.0 / (1.0 + e2)
    g2 = e2 / (1.0 + e2)
    return (jnp.where(lane == 0.0, i1, 0.0) + jnp.where(lane == 1.0, i2, 0.0)
            + jnp.where(lane == 2.0, g1, 0.0) + jnp.where(lane == 3.0, g2, 0.0))


def _ffn_up_body(x_hbm, g_ref, w1_ref, w3_ref, o_ref, h_scr, xbuf, sem):
    _load_norm_rows(x_hbm, g_ref, xbuf, sem, h_scr)
    h = h_scr[...]
    a = jnp.dot(h, w1_ref[...], preferred_element_type=F32)
    b = jnp.dot(h, w3_ref[...], preferred_element_type=F32)
    o_ref[...] = (jax.nn.silu(a) * b).astype(o_ref.dtype)


def ffn_up(x, g, w1, w3, wi):
    t, d = x.shape
    f = w1.shape[2]
    tm = _tile(t, 1024)
    tn = _tile(f, 512)
    assert f // tn >= 2
    w_spec = pl.BlockSpec((None, d, tn), lambda m, n: (wi, 0, n))
    return pl.pallas_call(
        _ffn_up_body,
        grid=(t // tm, f // tn),
        in_specs=[pl.BlockSpec(memory_space=pl.ANY),
                  pl.BlockSpec((1, d), lambda m, n: (0, 0)), w_spec, w_spec],
        out_specs=pl.BlockSpec((tm, tn), lambda m, n: (m, n)),
        out_shape=jax.ShapeDtypeStruct((t, f), BF16),
        scratch_shapes=[pltpu.VMEM((tm, d), BF16), pltpu.VMEM((tm, d), F32), pltpu.SemaphoreType.DMA(())],
        compiler_params=_cparams("arbitrary", "arbitrary"),
        name="ffn_up",
    )(x, g.reshape(1, d), w1, w3)


MOE_ROWS = 256
ROUTE_LANES = 128


def _moe_route_body(x_ref, g_ref, router_ref, rb_ref, top_ref, h_scr, *, n_experts):
    _rmsnorm_rows(x_ref, g_ref, h_scr)
    top_ref[...] = _router_top2(h_scr[...], router_ref, rb_ref, n_experts)


def moe_route(x, g, router, router_b):
    t, d = x.shape
    n_experts = router.shape[1]
    tm = _tile(t, 512)
    rpad = jnp.zeros((d, ROUTE_LANES), BF16).at[:, :n_experts].set(router.astype(BF16))
    bpad = jnp.zeros((1, ROUTE_LANES), F32).at[0, :n_experts].set(router_b.astype(F32))
    return pl.pallas_call(
        functools.partial(_moe_route_body, n_experts=n_experts),
        grid=(t // tm,),
        in_specs=[pl.BlockSpec((tm, d), lambda m: (m, 0)), pl.BlockSpec((1, d), lambda m: (0, 0)),
                  pl.BlockSpec((d, ROUTE_LANES), lambda m: (0, 0)),
                  pl.BlockSpec((1, ROUTE_LANES), lambda m: (0, 0))],
        out_specs=pl.BlockSpec((tm, ROUTE_LANES), lambda m: (m, 0)),
        out_shape=jax.ShapeDtypeStruct((t, ROUTE_LANES), F32),
        scratch_shapes=[pltpu.VMEM((tm, d), BF16)],
        compiler_params=_cparams("arbitrary"),
        name="moe_route",
    )(x, g.reshape(1, d), rpad, bpad)


def _moe_plan(top, n_experts):
    t = top.shape[0]
    n_slots = TOP_K * t
    n_tiles = (n_slots + n_experts * (MOE_ROWS - 1)) // MOE_ROWS
    e = top[:, :TOP_K].astype(jnp.int32).reshape(n_slots)
    gate = top[:, TOP_K:2 * TOP_K].reshape(n_slots)
    slot = jnp.arange(n_slots, dtype=jnp.int32)
    order = jnp.argsort(e * n_slots + slot).astype(jnp.int32)
    counts = jnp.sum((e[:, None] == jnp.arange(n_experts)[None, :]).astype(jnp.int32), axis=0)
    padded = ((counts + MOE_ROWS - 1) // MOE_ROWS) * MOE_ROWS
    pend = jnp.cumsum(padded)
    pstart = pend - padded
    cstart = jnp.cumsum(counts) - counts
    row = jnp.arange(n_tiles * MOE_ROWS, dtype=jnp.int32)
    row_e = jnp.minimum(jnp.sum((row[:, None] >= pend[None, :]).astype(jnp.int32), axis=1), n_experts - 1)
    rank = row - pstart[row_e]
    real = rank < counts[row_e]
    row_slot = order[jnp.clip(cstart[row_e] + rank, 0, n_slots - 1)]
    row_token = jnp.where(real, row_slot // TOP_K, 0).astype(jnp.int32)
    row_dst = jnp.where(real, (row_slot % TOP_K) * t + row_slot // TOP_K, 0).astype(jnp.int32)
    row_gate = jnp.where(real, gate[row_slot], 0.0).reshape(-1, 1)
    tile0 = jnp.arange(n_tiles, dtype=jnp.int32) * MOE_ROWS
    tile_e = row_e[tile0]
    tile_rows = jnp.clip(pstart[tile_e] + counts[tile_e] - tile0, 0, MOE_ROWS)
    tile_rows = jnp.where(tile0 < pend[-1], tile_rows, 0).astype(jnp.int32)
    return tile_e, tile_rows, row_token, row_dst, row_gate


def _moe_expert_body(te_ref, tr_ref, tok_ref, dst_ref, x_hbm, g_ref, gate_ref, w1_ref, w3_ref, w2_ref,
                     y_hbm, xbuf, ybuf, gsem, ssem):
    rows = xbuf.shape[1]
    j = pl.program_id(0)
    last = pl.num_programs(0) - 1
    slot = lax.rem(j, 2)

    def gather_start(tile, s):
        def one(i, c):
            pltpu.make_async_copy(x_hbm.at[pl.ds(tok_ref[tile * rows + i], 1), :],
                                  xbuf.at[s, pl.ds(i, 1), :], gsem.at[s]).start()
            return c
        lax.fori_loop(0, rows, one, 0, unroll=8)

    def scatter_start(tile, s, n):
        def one(i, c):
            pltpu.make_async_copy(ybuf.at[s, pl.ds(i, 1), :],
                                  y_hbm.at[pl.ds(dst_ref[tile * rows + i], 1), :], ssem.at[s]).start()
            return c
        lax.fori_loop(0, n, one, 0)

    def scatter_wait(s, n):
        bit = rows
        while bit >= 1:
            @pl.when((n & bit) != 0)
            def _(bit=bit):
                pltpu.make_async_copy(ybuf.at[s, pl.ds(0, bit), :], y_hbm.at[pl.ds(0, bit), :],
                                      ssem.at[s]).wait()
            bit //= 2

    @pl.when((j == 0) & (tr_ref[0] > 0))
    def _():
        gather_start(0, 0)

    nxt = jnp.minimum(j + 1, last)

    @pl.when((j < last) & (tr_ref[nxt] > 0))
    def _():
        gather_start(nxt, 1 - slot)

    @pl.when(tr_ref[j] > 0)
    def _():
        pltpu.make_async_copy(x_hbm.at[pl.ds(0, rows), :], xbuf.at[slot], gsem.at[slot]).wait()
        h = _rmsnorm(xbuf[slot], g_ref[...]).astype(BF16)
        a = jnp.dot(h, w1_ref[...], preferred_element_type=F32)
        b = jnp.dot(h, w3_ref[...], preferred_element_type=F32)
        hid = (jax.nn.silu(a) * b).astype(BF16)
        ybuf[slot] = (jnp.dot(hid, w2_ref[...], preferred_element_type=F32) * gate_ref[...]
                      + 0.5 * xbuf[slot])
        scatter_start(j, slot, tr_ref[j])

    prv = jnp.maximum(j - 1, 0)

    @pl.when(j >= 1)
    def _():
        scatter_wait(1 - slot, tr_ref[prv])

    @pl.when(j == last)
    def _():
        scatter_wait(slot, tr_ref[j])


def moe_experts(x, g, plan, w1, w3, w2, e0, wi2):
    tile_e, tile_rows, row_token, row_dst, row_gate = plan
    t, d = x.shape
    f = w1.shape[2]
    n_tiles = tile_e.shape[0]
    once = pl.Buffered(1)
    grid_spec = pltpu.PrefetchScalarGridSpec(
        num_scalar_prefetch=4,
        grid=(n_tiles,),
        in_specs=[pl.BlockSpec(memory_space=pl.ANY),
                  pl.BlockSpec((1, d), lambda j, te, tr, tok, dst: (0, 0)),
                  pl.BlockSpec((MOE_ROWS, 1), lambda j, te, tr, tok, dst: (j, 0)),
                  pl.BlockSpec((None, d, f), lambda j, te, tr, tok, dst: (e0 + te[j], 0, 0),
                               pipeline_mode=once),
                  pl.BlockSpec((None, d, f), lambda j, te, tr, tok, dst: (e0 + te[j], 0, 0),
                               pipeline_mode=once),
                  pl.BlockSpec((None, f, d), lambda j, te, tr, tok, dst: (wi2, te[j], 0),
                               pipeline_mode=once)],
        out_specs=pl.BlockSpec(memory_space=pl.ANY),
        scratch_shapes=[pltpu.VMEM((2, MOE_ROWS, d), F32), pltpu.VMEM((2, MOE_ROWS, d), F32),
                        pltpu.SemaphoreType.DMA((2,)), pltpu.SemaphoreType.DMA((2,))])
    return pl.pallas_call(
        _moe_expert_body,
        grid_spec=grid_spec,
        out_shape=jax.ShapeDtypeStruct((TOP_K * t, d), F32),
        compiler_params=_cparams("arbitrary"),
        name="moe_experts",
    )(tile_e, tile_rows, row_token, row_dst, x, g.reshape(1, d), row_gate, w1, w3, w2)


def _moe_combine_body(ya_ref, yb_ref, *rest, norm):
    if norm:
        g_ref, o_ref = rest
    else:
        (o_ref,) = rest
    o_ref[...] = ya_ref[...] + yb_ref[...]
    if norm:
        _rmsnorm_rows(o_ref, g_ref, o_ref)


def moe_combine(y2, g=None):
    t, d = y2.shape[0] // TOP_K, y2.shape[1]
    tm = _tile(t, 512)
    y3 = y2.reshape(TOP_K, t, d)
    in_specs = [pl.BlockSpec((None, tm, d), lambda m: (0, m, 0)),
                pl.BlockSpec((None, tm, d), lambda m: (1, m, 0))]
    args = [y3, y3]
    if g is not None:
        in_specs.append(pl.BlockSpec((1, d), lambda m: (0, 0)))
        args.append(g.reshape(1, d))
    return pl.pallas_call(
        functools.partial(_moe_combine_body, norm=g is not None),
        grid=(t // tm,),
        in_specs=in_specs,
        out_specs=pl.BlockSpec((tm, d), lambda m: (m, 0)),
        out_shape=jax.ShapeDtypeStruct((t, d), F32),
        compiler_params=_cparams("arbitrary"),
        name="moe_combine",
    )(*args)


def _final_norm_body(x_ref, g_ref, o_ref):
    _rmsnorm_rows(x_ref, g_ref, o_ref)


def final_norm(x, g):
    t, d = x.shape
    tm = _tile(t, 512)
    return pl.pallas_call(
        _final_norm_body,
        grid=(t // tm,),
        in_specs=[pl.BlockSpec((tm, d), lambda m: (m, 0)), pl.BlockSpec((1, d), lambda m: (0, 0))],
        out_specs=pl.BlockSpec((tm, d), lambda m: (m, 0)),
        out_shape=jax.ShapeDtypeStruct((t, d), F32),
        compiler_params=_cparams("arbitrary"),
        name="final_norm",
    )(x, g.reshape(1, d))


def kernel(x_prompt, x_sample, cache_k, cache_v, state_ssm_re, state_ssm_im, norm1_g, norm2_g, w_in,
           lam_re, lam_im, log_dt, ssm_b_re, ssm_b_im, ssm_c_re, ssm_c_im, ssm_d, w_glu, w_pa, w_pb,
           b_gate, w_o, ffn_w1, ffn_w3, ffn_w2, moe_router, moe_router_b, moe_w1, moe_w3, moe_w2,
           final_g):
    bp, seq, d = x_prompt.shape
    db, ds, _ = x_sample.shape
    depth = w_in.shape[0]
    _, _, past, n_heads, head_dim = cache_k.shape
    d_attn = n_heads * head_dim
    n_groups, n_state = lam_re.shape[1:]
    d_ssm = n_groups * GROUP
    npst = n_groups * n_state
    assert bp == 1 and db == SUBLANES and seq % SUBLANES == 0

    xp = x_prompt.reshape(seq, d)
    xs = x_sample.reshape(db * ds, d)
    w_in_b, w_glu_b, w_pa_b, w_pb_b, w_o_b = (w.astype(BF16) for w in (w_in, w_glu, w_pa, w_pb, w_o))
    ffn_w1_b, ffn_w3_b, ffn_w2_b = (w.astype(BF16) for w in (ffn_w1, ffn_w3, ffn_w2))
    n_moe, n_exp, _, f_e = moe_w1.shape
    moe_w1_b = moe_w1.astype(BF16).reshape(n_moe * n_exp, d, f_e)
    moe_w3_b = moe_w3.astype(BF16).reshape(n_moe * n_exp, d, f_e)
    moe_w2_b = moe_w2.astype(BF16).reshape(n_moe, n_exp * f_e, d)
    outs = [[] for _ in range(8)]
    for l in range(depth):
        i = l // 2
        coef, ar, ai = _ssm_coef(lam_re[l], lam_im[l], log_dt[l], ssm_b_re[l], ssm_b_im[l],
                                 ssm_c_re[l], ssm_c_im[l], ssm_d[l])

        end_gain = final_g if l == depth - 1 else None

        def channel_mix(x):
            if l % 2 == 0:
                hid = ffn_up(x, norm2_g[l], ffn_w1_b, ffn_w3_b, i)
                x = matmul_res(hid, ffn_w2_b, i, x)
                return x if end_gain is None else final_norm(x, end_gain)
            plan = _moe_plan(moe_route(x, norm2_g[l], moe_router[i], moe_router_b[i]), n_exp)
            y2 = moe_experts(x, norm2_g[l], plan, moe_w1_b, moe_w3_b, moe_w2_b, i * n_exp, i)
            return moe_combine(y2, end_gain)

        q, k, v, kvb, u, gates = norm_proj(xp, norm1_g[l], w_in_b, l, d_attn=d_attn, d_ssm=d_ssm,
                                           head_dim=head_dim, permute_u=True)
        o_a = attention_prompt(q, kvb, n_heads=n_heads, head_dim=head_dim)
        zeros = jnp.zeros((SUBLANES, npst), F32)
        _, fr, fi = ssm_scan(u, coef, zeros, zeros, emit_y=False)
        h0r, h0i = _segment_inits(ar, ai, fr, fi, seq // SUBLANES)
        y, hr, hi = ssm_scan(u, coef, h0r, h0i, emit_y=True)
        merged = mixer_merge(o_a, y, gates, b_gate[l], w_glu_b, w_pa_b, w_pb_b, l, y_permuted=True)
        xp = matmul_res(merged, w_o_b, l, xp)
        xp = channel_mix(xp)
        outs[0].append(k)
        outs[1].append(v)
        outs[2].append(hr[SUBLANES - 1].reshape(1, n_groups, n_state))
        outs[3].append(hi[SUBLANES - 1].reshape(1, n_groups, n_state))

        q, k, v, kvb, u, gates = norm_proj(xs, norm1_g[l], w_in_b, l, d_attn=d_attn, d_ssm=d_ssm,
                                           head_dim=head_dim, permute_u=False)
        o_a = attention_sample(q, kvb, cache_k, cache_v, l, ds=ds)
        u_t = u.reshape(db, ds, d_ssm).transpose(1, 0, 2).reshape(ds, db * d_ssm)
        y_t, hr, hi = ssm_scan(u_t, coef, state_ssm_re[l].astype(F32).reshape(db, npst),
                               state_ssm_im[l].astype(F32).reshape(db, npst), emit_y=True)
        y = y_t.reshape(ds, db, d_ssm).transpose(1, 0, 2).reshape(db * ds, d_ssm)
        merged = mixer_merge(o_a, y, gates, b_gate[l], w_glu_b, w_pa_b, w_pb_b, l, y_permuted=False)
        xs = matmul_res(merged, w_o_b, l, xs)
        xs = channel_mix(xs)
        outs[4].append(k)
        outs[5].append(v)
        outs[6].append(hr.reshape(db, n_groups, n_state))
        outs[7].append(hi.reshape(db, n_groups, n_state))

    y_prompt = xp.reshape(bp, seq, d)
    y_sample = xs.reshape(db, ds, d)
    heads = dict(n_heads=n_heads, head_dim=head_dim)
    kv_outs = {0: stack_heads(outs[0], bp, **heads), 1: stack_heads(outs[1], bp, **heads),
               4: stack_heads(outs[4], db, **heads), 5: stack_heads(outs[5], db, **heads)}
    return (y_prompt, y_sample) + tuple(kv_outs[j] if j in kv_outs else jnp.stack(outs[j])
                                        for j in range(8))
```

```python
import functools
import math

import jax
import jax.numpy as jnp
from jax import lax
from jax.experimental import pallas as pl
from jax.experimental.pallas import tpu as pltpu

F32 = jnp.float32
BF16 = jnp.bfloat16

RMS_EPS = 1e-6
TOP_K = 2
GROUP = 16
SUBLANES = 8
V7X_VMEM_LIMIT_BYTES = 56 * 1024 * 1024
SSM_KB_GROUPS = 16
SSM_STEPS_PER_BLOCK = 64


def _cparams(*sem):
    return pltpu.CompilerParams(dimension_semantics=sem, vmem_limit_bytes=V7X_VMEM_LIMIT_BYTES)


def _tile(dim, pref):
    if dim <= pref:
        return dim
    t = pref
    while t >= 128:
        if dim % t == 0:
            return t
        t -= 128
    return dim


def _rmsnorm(x, g):
    ms = jnp.mean(x * x, axis=-1, keepdims=True)
    return (x * lax.rsqrt(ms + RMS_EPS)) * g


NORM_ROWS = 128


def _rmsnorm_rows(x_ref, g_ref, o_ref):
    rows = x_ref.shape[0]
    step = min(NORM_ROWS, rows)
    g = g_ref[...]

    def body(c, carry):
        r0 = pl.multiple_of(c * step, step)
        o_ref[pl.ds(r0, step), :] = _rmsnorm(x_ref[pl.ds(r0, step), :], g).astype(o_ref.dtype)
        return carry

    lax.fori_loop(0, rows // step, body, 0)


def _load_norm_rows(x_hbm, g_ref, xbuf, sem, h_scr):
    m, n = pl.program_id(0), pl.program_id(1)
    tm = xbuf.shape[0]

    def copy(mi):
        return pltpu.make_async_copy(x_hbm.at[pl.ds(pl.multiple_of(mi * tm, tm), tm), :], xbuf, sem)

    @pl.when((m == 0) & (n == 0))
    def _():
        copy(0).start()

    @pl.when(n == 0)
    def _():
        copy(m).wait()
        _rmsnorm_rows(xbuf, g_ref, h_scr)

    @pl.when((n == 1) & (m + 1 < pl.num_programs(0)))
    def _():
        copy(m + 1).start()


def _norm_proj_body(x_hbm, g_ref, w_ref, q_ref, k_ref, v_ref, kvb_ref, u_ref, gate_ref, h_scr, xbuf, sem,
                    *, nq, ns, q_scale):
    n = pl.program_id(1)
    _load_norm_rows(x_hbm, g_ref, xbuf, sem, h_scr)

    def z():
        return jnp.dot(h_scr[...], w_ref[...], preferred_element_type=F32)

    @pl.when(n < nq)
    def _():
        q_ref[...] = (z() * q_scale).astype(BF16)

    @pl.when((n >= nq) & (n < 2 * nq))
    def _():
        zz = z()
        k_ref[...] = zz
        kvb_ref[...] = zz.astype(BF16)

    @pl.when((n >= 2 * nq) & (n < 3 * nq))
    def _():
        zz = z()
        v_ref[...] = zz
        kvb_ref[...] = zz.astype(BF16)

    @pl.when((n >= 3 * nq) & (n < 3 * nq + ns))
    def _():
        u_ref[...] = z()

    @pl.when(n >= 3 * nq + ns)
    def _():
        gate_ref[...] = z().astype(BF16)


def norm_proj(x, g, w, l, *, d_attn, d_ssm, head_dim, permute_u):
    t, d = x.shape
    d_in = w.shape[2]
    tn = _tile(d_ssm, 512)
    assert d_attn % tn == 0 and d % tn == 0
    nq, ns = d_attn // tn, d_ssm // tn
    ng = 2 * d // tn
    assert d_in == (3 * nq + ns + ng) * tn
    tm = _tile(t // SUBLANES if permute_u else t, 1024)
    if permute_u:
        seg = t // SUBLANES
        assert seg % tm == 0
        mps = seg // tm
        u_shape = (seg, SUBLANES * d_ssm)
        u_map = lambda m, n: (m % mps, (m // mps) * ns + jnp.clip(n - 3 * nq, 0, ns - 1))
    else:
        u_shape = (t, d_ssm)
        u_map = lambda m, n: (m, jnp.clip(n - 3 * nq, 0, ns - 1))
    blk = lambda imap: pl.BlockSpec((tm, tn), imap)
    outs = pl.pallas_call(
        functools.partial(_norm_proj_body, nq=nq, ns=ns, q_scale=head_dim ** -0.5),
        grid=(t // tm, d_in // tn),
        in_specs=[pl.BlockSpec(memory_space=pl.ANY),
                  pl.BlockSpec((1, d), lambda m, n: (0, 0)),
                  pl.BlockSpec((None, d, tn), lambda m, n: (l, 0, n))],
        out_specs=[blk(lambda m, n: (m, jnp.minimum(n, nq - 1))),
                   blk(lambda m, n: (m, jnp.clip(n - nq, 0, nq - 1))),
                   blk(lambda m, n: (m, jnp.clip(n - 2 * nq, 0, nq - 1))),
                   blk(lambda m, n: (m, jnp.clip(n - nq, 0, 2 * nq - 1))),
                   blk(u_map),
                   blk(lambda m, n: (m, jnp.clip(n - 3 * nq - ns, 0, ng - 1)))],
        out_shape=[jax.ShapeDtypeStruct((t, d_attn), BF16),
                   jax.ShapeDtypeStruct((t, d_attn), F32),
                   jax.ShapeDtypeStruct((t, d_attn), F32),
                   jax.ShapeDtypeStruct((t, 2 * d_attn), BF16),
                   jax.ShapeDtypeStruct(u_shape, F32),
                   jax.ShapeDtypeStruct((t, 2 * d), BF16)],
        scratch_shapes=[pltpu.VMEM((tm, d), BF16), pltpu.VMEM((tm, d), F32), pltpu.SemaphoreType.DMA(())],
        compiler_params=_cparams("arbitrary", "arbitrary"),
        name="norm_proj",
    )(x, g.reshape(1, d), w)
    return outs


def _softplus(z):
    return jnp.maximum(z, 0.0) + jnp.log(1.0 + jnp.exp(-jnp.abs(z)))


def _tri(tk):
    r = lax.broadcasted_iota(jnp.int32, (tk, tk), 0)
    c = lax.broadcasted_iota(jnp.int32, (tk, tk), 1)
    return (r >= c).astype(BF16)


def _sb_tile(q, kb, vb, tri, later, mask):
    z = lax.dot_general(q, kb, (((1,), (1,)), ((), ())), preferred_element_type=F32)
    sp = _softplus(z)
    if mask is not None:
        sp = jnp.where(mask, sp, 0.0)
    hi = sp.astype(BF16)
    lo = (sp - hi.astype(F32)).astype(BF16)
    within = (jnp.dot(hi, tri, preferred_element_type=F32)
              + jnp.dot(lo, tri, preferred_element_type=F32))
    w = jnp.exp(jnp.minimum(z - (within + later), 0.0))
    if mask is not None:
        w = jnp.where(mask, w, 0.0)
    pv = jnp.dot(w.astype(BF16), vb, preferred_element_type=F32)
    return pv, later + within[:, 0:1]


SB_ZERO_EXPONENT = 110.0
SB_NORM_SLACK = 1.001
KNORM_ROWS = 512


def _attn_prompt_body(q_ref, k_ref, v_ref, o_ref, acc_scr, kmax_scr, *, tq, tk):
    i = pl.program_id(1)
    nsub = tq // tk
    base = i * nsub
    tri = _tri(tk)
    later = jnp.zeros((tq, 1), F32)
    acc_scr[...] = jnp.zeros_like(acc_scr)

    @pl.when(i == 0)
    def _():
        step = min(KNORM_ROWS, k_ref.shape[0])

        def knorm(c, m):
            kf = k_ref[pl.ds(pl.multiple_of(c * step, step), step), :].astype(F32)
            return jnp.maximum(m, jnp.sum(kf * kf, axis=1, keepdims=True))

        m = lax.fori_loop(0, k_ref.shape[0] // step, knorm, jnp.zeros((step, 1), F32))
        kmax_scr[...] = jnp.broadcast_to(jnp.sqrt(jnp.max(m, axis=0, keepdims=True)), kmax_scr.shape)

    qf = q_ref[...].astype(F32)
    zmax = (jnp.sqrt(jnp.sum(qf * qf, axis=1, keepdims=True)) * kmax_scr[0:1, 0:1]) * SB_NORM_SLACK

    for jr in range(nsub - 1, -1, -1):
        r0 = jr * tk
        k0 = pl.multiple_of((base + jr) * tk, tk)
        row = lax.broadcasted_iota(jnp.int32, (tq - r0, tk), 0)
        col = lax.broadcasted_iota(jnp.int32, (tq - r0, tk), 1)
        pv, lat = _sb_tile(q_ref[r0:, :], k_ref[pl.ds(k0, tk), :], v_ref[pl.ds(k0, tk), :], tri,
                           later[r0:], col < row)
        acc_scr[r0:, :] += pv
        later = lat if r0 == 0 else jnp.concatenate([later[:r0], lat], axis=0)

    def live(later):
        return (jnp.min(later - zmax) <= SB_ZERO_EXPONENT).astype(jnp.int32)

    def cond(state):
        s, _, go = state
        return (s < base) & (go > 0)

    def body(state):
        s, later, _ = state
        j0 = pl.multiple_of((base - 1 - s) * tk, tk)
        pv, later = _sb_tile(q_ref[...], k_ref[pl.ds(j0, tk), :], v_ref[pl.ds(j0, tk), :], tri,
                             later, None)
        acc_scr[...] += pv
        return s + 1, later, live(later)

    lax.while_loop(cond, body, (jnp.int32(0), later, live(later)))
    o_ref[...] = acc_scr[...].astype(o_ref.dtype)


ATTN_Q_ROWS = 1024
ATTN_K_ROWS = 256


def attention_prompt(q, kvb, *, n_heads, head_dim):
    t = q.shape[0]
    tk = _tile(t, ATTN_K_ROWS)
    tq = _tile(t, ATTN_Q_ROWS)
    assert tq % tk == 0
    single = pl.Buffered(1)
    return pl.pallas_call(
        functools.partial(_attn_prompt_body, tq=tq, tk=tk),
        grid=(n_heads, t // tq),
        in_specs=[pl.BlockSpec((tq, head_dim), lambda h, i: (i, h)),
                  pl.BlockSpec((t, head_dim), lambda h, i: (0, h), pipeline_mode=single),
                  pl.BlockSpec((t, head_dim), lambda h, i: (0, n_heads + h), pipeline_mode=single)],
        out_specs=pl.BlockSpec((tq, head_dim), lambda h, i: (i, h)),
        out_shape=jax.ShapeDtypeStruct((t, n_heads * head_dim), BF16),
        scratch_shapes=[pltpu.VMEM((tq, head_dim), F32), pltpu.VMEM((SUBLANES, 128), F32)],
        compiler_params=_cparams("arbitrary", "arbitrary"),
        name="attn_prompt",
    )(q, kvb, kvb)


def _attn_sample_body(q_ref, kn_ref, vn_ref, ck_hbm, cv_hbm, o_ref, kbuf, vbuf, sem, *, l, ds, tk):
    n_heads = pl.num_programs(1)
    step = pl.program_id(0) * n_heads + pl.program_id(1)
    slot = lax.rem(step, 2)

    def fetch(s, sl):
        b, h = s // n_heads, lax.rem(s, n_heads)
        return (pltpu.make_async_copy(ck_hbm.at[l, b, :, h, :], kbuf.at[sl], sem.at[0, sl]),
                pltpu.make_async_copy(cv_hbm.at[l, b, :, h, :], vbuf.at[sl], sem.at[1, sl]))

    @pl.when(step == 0)
    def _():
        for cp in fetch(0, 0):
            cp.start()

    @pl.when(step + 1 < pl.num_programs(0) * n_heads)
    def _():
        for cp in fetch(step + 1, 1 - slot):
            cp.start()

    for cp in fetch(step, slot):
        cp.wait()

    q = q_ref[...]
    row = lax.broadcasted_iota(jnp.int32, (ds, ds), 0)
    col = lax.broadcasted_iota(jnp.int32, (ds, ds), 1)
    acc, later = _sb_tile(q, kn_ref[...], vn_ref[...], _tri(ds), jnp.zeros((ds, 1), F32), col < row)
    tri = _tri(tk)
    for j in range(kbuf.shape[1] // tk - 1, -1, -1):
        kb = kbuf[slot, j * tk:(j + 1) * tk, :].astype(BF16)
        vb = vbuf[slot, j * tk:(j + 1) * tk, :].astype(BF16)
        pv, later = _sb_tile(q, kb, vb, tri, later, None)
        acc = acc + pv
    o_ref[...] = acc.astype(o_ref.dtype)


def attention_sample(q, kvb, cache_k, cache_v, l, *, ds):
    _, b, past, n_heads, head_dim = cache_k.shape
    tk = _tile(past, ATTN_K_ROWS)
    return pl.pallas_call(
        functools.partial(_attn_sample_body, l=l, ds=ds, tk=tk),
        grid=(b, n_heads),
        in_specs=[pl.BlockSpec((ds, head_dim), lambda bi, h: (bi, h)),
                  pl.BlockSpec((ds, head_dim), lambda bi, h: (bi, h)),
                  pl.BlockSpec((ds, head_dim), lambda bi, h: (bi, n_heads + h)),
                  pl.BlockSpec(memory_space=pl.ANY), pl.BlockSpec(memory_space=pl.ANY)],
        out_specs=pl.BlockSpec((ds, head_dim), lambda bi, h: (bi, h)),
        out_shape=jax.ShapeDtypeStruct((b * ds, n_heads * head_dim), BF16),
        scratch_shapes=[pltpu.VMEM((2, past, head_dim), F32), pltpu.VMEM((2, past, head_dim), F32),
                        pltpu.SemaphoreType.DMA((2, 2))],
        compiler_params=_cparams("arbitrary", "arbitrary"),
        name="attn_sample",
    )(q, kvb, kvb, cache_k, cache_v)


def _stack_heads_body(*refs, n_heads, head_dim):
    *in_refs, o_ref = refs
    l = pl.program_id(0)
    for i, x_ref in enumerate(in_refs):
        @pl.when(l == i)
        def _(x_ref=x_ref):
            for h in range(n_heads):
                o_ref[:, h, :] = x_ref[:, h * head_dim:(h + 1) * head_dim]


def stack_heads(xs, batch, *, n_heads, head_dim):
    depth = len(xs)
    t = xs[0].shape[0]
    tm = _tile(t, 512)
    nm = t // tm
    in_specs = [pl.BlockSpec((tm, n_heads * head_dim),
                             lambda l, m, i=i: (jnp.clip(m + (l - i) * nm, 0, nm - 1), 0))
                for i in range(depth)]
    out = pl.pallas_call(
        functools.partial(_stack_heads_body, n_heads=n_heads, head_dim=head_dim),
        grid=(depth, nm),
        in_specs=in_specs,
        out_specs=pl.BlockSpec((None, None, tm, n_heads, head_dim), lambda l, m: (l, 0, m, 0, 0)),
        out_shape=jax.ShapeDtypeStruct((depth, 1, t, n_heads, head_dim), F32),
        compiler_params=_cparams("arbitrary", "arbitrary"),
        name="stack_heads",
    )(*xs)
    return out.reshape(depth, batch, t // batch, n_heads, head_dim)


def _ssm_body(u_ref, bdr_ref, bdi_ref, cdr_ref, cdi_ref, ar_ref, ai_ref, d_ref, h0r_ref, h0i_ref,
              *rest, n, nkb, emit_y):
    if emit_y:
        y_ref, hr_ref, hi_ref, str_scr, sti_scr, xr_scr, xi_scr, rows_scr = rest
    else:
        hr_ref, hi_ref, str_scr, sti_scr, xr_scr, xi_scr, rows_scr = rest
    step = pl.program_id(0)
    kin = bdr_ref.shape[1]
    kst = bdr_ref.shape[2]

    @pl.when(step == 0)
    def _():
        str_scr[...] = h0r_ref[...]
        sti_scr[...] = h0i_ref[...]

    d_ssm = nkb * kin
    for r in range(SUBLANES):
        rows_scr[:, r, :] = u_ref[:, r * d_ssm:(r + 1) * d_ssm]
    u = rows_scr[...].reshape(n * SUBLANES, d_ssm)
    ub = u.astype(BF16)
    for kb in range(nkb):
        ukb = ub[:, kb * kin:(kb + 1) * kin]
        xr_scr[:, kb * kst:(kb + 1) * kst] = jnp.dot(ukb, bdr_ref[kb], preferred_element_type=F32)
        xi_scr[:, kb * kst:(kb + 1) * kst] = jnp.dot(ukb, bdi_ref[kb], preferred_element_type=F32)

    for kb in range(nkb):
        cs = slice(kb * kst, (kb + 1) * kst)
        ar = jnp.broadcast_to(ar_ref[:, cs], (SUBLANES, kst))
        ai = jnp.broadcast_to(ai_ref[:, cs], (SUBLANES, kst))

        def scan_step(tt, carry, cs=cs, ar=ar, ai=ai):
            xr, xi = carry
            r0 = pl.multiple_of(tt * SUBLANES, SUBLANES)
            nxr = ar * xr - ai * xi + xr_scr[pl.ds(r0, SUBLANES), cs]
            nxi = ar * xi + ai * xr + xi_scr[pl.ds(r0, SUBLANES), cs]
            xr_scr[pl.ds(r0, SUBLANES), cs] = nxr
            xi_scr[pl.ds(r0, SUBLANES), cs] = nxi
            return nxr, nxi

        xr, xi = lax.fori_loop(0, n, scan_step, (str_scr[:, cs], sti_scr[:, cs]), unroll=4)
        str_scr[:, cs] = xr
        sti_scr[:, cs] = xi

    if emit_y:
        ys = []
        for kb in range(nkb):
            cs = slice(kb * kst, (kb + 1) * kst)
            yk = (jnp.dot(xr_scr[:, cs].astype(BF16), cdr_ref[kb], preferred_element_type=F32)
                  - jnp.dot(xi_scr[:, cs].astype(BF16), cdi_ref[kb], preferred_element_type=F32))
            os_ = slice(kb * kin, (kb + 1) * kin)
            ys.append(yk + d_ref[:, os_] * u[:, os_])
        rows_scr[...] = jnp.concatenate(ys, axis=1).reshape(n, SUBLANES, d_ssm)
        for r in range(SUBLANES):
            y_ref[:, r * d_ssm:(r + 1) * d_ssm] = rows_scr[:, r, :]

    @pl.when(step == pl.num_programs(0) - 1)
    def _():
        hr_ref[...] = str_scr[...]
        hi_ref[...] = sti_scr[...]


def ssm_scan(u, coef, h0r, h0i, *, emit_y):
    bdr, bdi, cdr, cdi, ar, ai, dsk = coef
    steps, width = u.shape
    d_ssm = width // SUBLANES
    nkb, kin, kst = bdr.shape
    npst = nkb * kst
    n = math.gcd(steps, SSM_STEPS_PER_BLOCK)
    rows = n * SUBLANES
    full = lambda a: pl.BlockSpec(a.shape, lambda s: (0,) * a.ndim)
    out_specs = [pl.BlockSpec((SUBLANES, npst), lambda s: (0, 0))] * 2
    out_shape = [jax.ShapeDtypeStruct((SUBLANES, npst), F32)] * 2
    if emit_y:
        out_specs = [pl.BlockSpec((n, width), lambda s: (s, 0))] + out_specs
        out_shape = [jax.ShapeDtypeStruct((steps, width), F32)] + out_shape
    outs = pl.pallas_call(
        functools.partial(_ssm_body, n=n, nkb=nkb, emit_y=emit_y),
        grid=(steps // n,),
        in_specs=[pl.BlockSpec((n, width), lambda s: (s, 0)),
                  full(bdr), full(bdi), full(cdr), full(cdi), full(ar), full(ai), full(dsk),
                  full(h0r), full(h0i)],
        out_specs=out_specs,
        out_shape=out_shape,
        scratch_shapes=[pltpu.VMEM((SUBLANES, npst), F32), pltpu.VMEM((SUBLANES, npst), F32),
                        pltpu.VMEM((rows, npst), F32), pltpu.VMEM((rows, npst), F32),
                        pltpu.VMEM((n, SUBLANES, d_ssm), F32)],
        compiler_params=_cparams("arbitrary"),
        name="ssm_scan_y" if emit_y else "ssm_scan_state",
    )(u, bdr, bdi, cdr, cdi, ar, ai, dsk, h0r, h0i)
    if emit_y:
        return outs[0], outs[1], outs[2]
    return None, outs[0], outs[1]


def _ssm_coef(lam_re, lam_im, log_dt, b_re, b_im, c_re, c_im, d_skip):
    g, p = lam_re.shape
    dt = jnp.exp(log_dt.astype(F32))[:, None]
    lr, li = lam_re.astype(F32), lam_im.astype(F32)
    mag = jnp.exp(lr * dt)
    ar = mag * jnp.cos(li * dt)
    ai = mag * jnp.sin(li * dt)
    den = lr * lr + li * li
    nr = ar - 1.0
    cr = (nr * lr + ai * li) / den
    ci = (ai * lr - nr * li) / den
    br, bi = b_re.astype(F32), b_im.astype(F32)
    bbr = cr[..., None] * br - ci[..., None] * bi
    bbi = cr[..., None] * bi + ci[..., None] * br
    gb = min(SSM_KB_GROUPS, g)
    nkb = g // gb
    eye = jnp.eye(gb, dtype=F32)

    def pack_b(m):
        m = m.reshape(nkb, gb, p, GROUP)
        return jnp.einsum('kaph,ab->kahbp', m, eye).reshape(nkb, gb * GROUP, gb * p).astype(BF16)

    def pack_c(m):
        m = m.reshape(nkb, gb, GROUP, p)
        return jnp.einsum('kahp,ab->kapbh', m, eye).reshape(nkb, gb * p, gb * GROUP).astype(BF16)

    coef = (pack_b(bbr), pack_b(bbi), pack_c(c_re.astype(F32)), pack_c(c_im.astype(F32)),
            ar.reshape(1, g * p), ai.reshape(1, g * p), d_skip.astype(F32).reshape(1, g * GROUP))
    return coef, ar, ai


def _segment_inits(ar, ai, fr, fi, seg_len):
    k = int(round(math.log2(seg_len)))
    assert 2 ** k == seg_len
    pr, pi = ar.reshape(1, -1), ai.reshape(1, -1)
    for _ in range(k):
        pr, pi = pr * pr - pi * pi, 2.0 * pr * pi
    hr = jnp.zeros_like(fr[0:1])
    hi = jnp.zeros_like(fi[0:1])
    hrs, his = [hr], [hi]
    for r in range(SUBLANES - 1):
        hr, hi = pr * hr - pi * hi + fr[r:r + 1], pr * hi + pi * hr + fi[r:r + 1]
        hrs.append(hr)
        his.append(hi)
    return jnp.concatenate(hrs, axis=0), jnp.concatenate(his, axis=0)


def _merge_body(oa_ref, y_ref, ga_ref, gb_ref, ba_ref, bb_ref, wglu_ref, wpa_ref, wpb_ref, o_ref,
                ob_scr):
    n = pl.program_id(1)

    @pl.when(n == 0)
    def _():
        gy = jax.nn.gelu(y_ref[...])
        glu = jnp.dot(gy.astype(BF16), wglu_ref[...], preferred_element_type=F32)
        ob_scr[...] = (gy * jax.nn.sigmoid(glu)).astype(BF16)

    pa = jnp.dot(oa_ref[...], wpa_ref[...], preferred_element_type=F32)
    pb = jnp.dot(ob_scr[...], wpb_ref[...], preferred_element_type=F32)
    gate_a = jax.nn.sigmoid(ga_ref[...].astype(F32) + ba_ref[...])
    gate_b = jax.nn.sigmoid(gb_ref[...].astype(F32) + bb_ref[...])
    o_ref[...] = (gate_a * pa + gate_b * pb).astype(o_ref.dtype)


def mixer_merge(o_a, y, gates, b_gate, w_glu, w_pa, w_pb, l, *, y_permuted):
    t, d_attn = o_a.shape
    d_ssm = w_glu.shape[1]
    d = w_pa.shape[2]
    tm = _tile(t // SUBLANES if y_permuted else t, 1024)
    tn = _tile(d, 1024)
    nd = d // tn
    if y_permuted:
        seg = t // SUBLANES
        mps = seg // tm
        assert y.shape == (seg, SUBLANES * d_ssm)
        y_spec = pl.BlockSpec((tm, d_ssm), lambda m, n: (m % mps, m // mps))
    else:
        y_spec = pl.BlockSpec((tm, d_ssm), lambda m, n: (m, 0))
    bg = b_gate.reshape(1, 2 * d)
    return pl.pallas_call(
        _merge_body,
        grid=(t // tm, nd),
        in_specs=[pl.BlockSpec((tm, d_attn), lambda m, n: (m, 0)),
                  y_spec,
                  pl.BlockSpec((tm, tn), lambda m, n: (m, n)),
                  pl.BlockSpec((tm, tn), lambda m, n: (m, nd + n)),
                  pl.BlockSpec((1, tn), lambda m, n: (0, n)),
                  pl.BlockSpec((1, tn), lambda m, n: (0, nd + n)),
                  pl.BlockSpec((None, d_ssm, d_ssm), lambda m, n: (l, 0, 0)),
                  pl.BlockSpec((None, d_attn, tn), lambda m, n: (l, 0, n)),
                  pl.BlockSpec((None, d_ssm, tn), lambda m, n: (l, 0, n))],
        out_specs=pl.BlockSpec((tm, tn), lambda m, n: (m, n)),
        out_shape=jax.ShapeDtypeStruct((t, d), BF16),
        scratch_shapes=[pltpu.VMEM((tm, d_ssm), BF16)],
        compiler_params=_cparams("arbitrary", "arbitrary"),
        name="mixer_merge",
    )(o_a, y, gates, gates, bg, bg, w_glu, w_pa, w_pb)


def _matmul_res_body(a_ref, w_ref, r_ref, o_ref):
    o_ref[...] = r_ref[...] + jnp.dot(a_ref[...], w_ref[...], preferred_element_type=F32)


def matmul_res(a, w, wi, res):
    t, k = a.shape
    n_out = w.shape[2]
    tm = _tile(t, 1024)
    tn = _tile(n_out, 512)
    return pl.pallas_call(
        _matmul_res_body,
        grid=(t // tm, n_out // tn),
        in_specs=[pl.BlockSpec((tm, k), lambda m, n: (m, 0)),
                  pl.BlockSpec((None, k, tn), lambda m, n: (wi, 0, n)),
                  pl.BlockSpec((tm, tn), lambda m, n: (m, n))],
        out_specs=pl.BlockSpec((tm, tn), lambda m, n: (m, n)),
        out_shape=jax.ShapeDtypeStruct((t, n_out), F32),
        compiler_params=_cparams("arbitrary", "arbitrary"),
        name="matmul_res",
    )(a, w, res)


def _router_top2(h, router_ref, rb_ref, n_experts):
    logits = jnp.dot(h, router_ref[...], preferred_element_type=F32) + rb_ref[...]
    lane = lax.broadcasted_iota(jnp.int32, logits.shape, 1).astype(F32)
    neg = jnp.float32(-jnp.inf)
    logits = jnp.where(lane < n_experts, logits, neg)
    big = jnp.float32(logits.shape[1])
    m1 = jnp.max(logits, axis=-1, keepdims=True)
    i1 = jnp.min(jnp.where(logits == m1, lane, big), axis=-1, keepdims=True)
    rest = jnp.where(lane == i1, neg, logits)
    m2 = jnp.max(rest, axis=-1, keepdims=True)
    i2 = jnp.min(jnp.where(rest == m2, lane, big), axis=-1, keepdims=True)
    e2 = jnp.exp(m2 - m1)
    g1 = 1.0 / (1.0 + e2)
    g2 = e2 / (1.0 + e2)
    return (jnp.where(lane == 0.0, i1, 0.0) + jnp.where(lane == 1.0, i2, 0.0)
            + jnp.where(lane == 2.0, g1, 0.0) + jnp.where(lane == 3.0, g2, 0.0))


def _ffn_up_body(x_hbm, g_ref, w1_ref, w3_ref, o_ref, h_scr, xbuf, sem):
    _load_norm_rows(x_hbm, g_ref, xbuf, sem, h_scr)
    h = h_scr[...]
    a = jnp.dot(h, w1_ref[...], preferred_element_type=F32)
    b = jnp.dot(h, w3_ref[...], preferred_element_type=F32)
    o_ref[...] = (jax.nn.silu(a) * b).astype(o_ref.dtype)


def ffn_up(x, g, w1, w3, wi):
    t, d = x.shape
    f = w1.shape[2]
    tm = _tile(t, 1024)
    tn = _tile(f, 512)
    assert f // tn >= 2
    w_spec = pl.BlockSpec((None, d, tn), lambda m, n: (wi, 0, n))
    return pl.pallas_call(
        _ffn_up_body,
        grid=(t // tm, f // tn),
        in_specs=[pl.BlockSpec(memory_space=pl.ANY),
                  pl.BlockSpec((1, d), lambda m, n: (0, 0)), w_spec, w_spec],
        out_specs=pl.BlockSpec((tm, tn), lambda m, n: (m, n)),
        out_shape=jax.ShapeDtypeStruct((t, f), BF16),
        scratch_shapes=[pltpu.VMEM((tm, d), BF16), pltpu.VMEM((tm, d), F32), pltpu.SemaphoreType.DMA(())],
        compiler_params=_cparams("arbitrary", "arbitrary"),
        name="ffn_up",
    )(x, g.reshape(1, d), w1, w3)


MOE_ROWS = 256
ROUTE_LANES = 128


def _moe_route_body(x_ref, g_ref, router_ref, rb_ref, top_ref, h_scr, *, n_experts):
    _rmsnorm_rows(x_ref, g_ref, h_scr)
    top_ref[...] = _router_top2(h_scr[...], router_ref, rb_ref, n_experts)


def moe_route(x, g, router, router_b):
    t, d = x.shape
    n_experts = router.shape[1]
    tm = _tile(t, 512)
    rpad = jnp.zeros((d, ROUTE_LANES), BF16).at[:, :n_experts].set(router.astype(BF16))
    bpad = jnp.zeros((1, ROUTE_LANES), F32).at[0, :n_experts].set(router_b.astype(F32))
    return pl.pallas_call(
        functools.partial(_moe_route_body, n_experts=n_experts),
        grid=(t // tm,),
        in_specs=[pl.BlockSpec((tm, d), lambda m: (m, 0)), pl.BlockSpec((1, d), lambda m: (0, 0)),
                  pl.BlockSpec((d, ROUTE_LANES), lambda m: (0, 0)),
                  pl.BlockSpec((1, ROUTE_LANES), lambda m: (0, 0))],
        out_specs=pl.BlockSpec((tm, ROUTE_LANES), lambda m: (m, 0)),
        out_shape=jax.ShapeDtypeStruct((t, ROUTE_LANES), F32),
        scratch_shapes=[pltpu.VMEM((tm, d), BF16)],
        compiler_params=_cparams("arbitrary"),
        name="moe_route",
    )(x, g.reshape(1, d), rpad, bpad)


def _moe_plan(top, n_experts):
    t = top.shape[0]
    n_slots = TOP_K * t
    n_tiles = (n_slots + n_experts * (MOE_ROWS - 1)) // MOE_ROWS
    e = top[:, :TOP_K].astype(jnp.int32).reshape(n_slots)
    gate = top[:, TOP_K:2 * TOP_K].reshape(n_slots)
    slot = jnp.arange(n_slots, dtype=jnp.int32)
    order = jnp.argsort(e * n_slots + slot).astype(jnp.int32)
    counts = jnp.sum((e[:, None] == jnp.arange(n_experts)[None, :]).astype(jnp.int32), axis=0)
    padded = ((counts + MOE_ROWS - 1) // MOE_ROWS) * MOE_ROWS
    pend = jnp.cumsum(padded)
    pstart = pend - padded
    cstart = jnp.cumsum(counts) - counts
    row = jnp.arange(n_tiles * MOE_ROWS, dtype=jnp.int32)
    row_e = jnp.minimum(jnp.sum((row[:, None] >= pend[None, :]).astype(jnp.int32), axis=1), n_experts - 1)
    rank = row - pstart[row_e]
    real = rank < counts[row_e]
    row_slot = order[jnp.clip(cstart[row_e] + rank, 0, n_slots - 1)]
    row_token = jnp.where(real, row_slot // TOP_K, 0).astype(jnp.int32)
    row_dst = jnp.where(real, (row_slot % TOP_K) * t + row_slot // TOP_K, 0).astype(jnp.int32)
    row_gate = jnp.where(real, gate[row_slot], 0.0).reshape(-1, 1)
    tile0 = jnp.arange(n_tiles, dtype=jnp.int32) * MOE_ROWS
    tile_e = row_e[tile0]
    tile_rows = jnp.clip(pstart[tile_e] + counts[tile_e] - tile0, 0, MOE_ROWS)
    tile_rows = jnp.where(tile0 < pend[-1], tile_rows, 0).astype(jnp.int32)
    return tile_e, tile_rows, row_token, row_dst, row_gate


def _moe_expert_body(te_ref, tr_ref, tok_ref, dst_ref, x_hbm, g_ref, gate_ref, w1_ref, w3_ref, w2_ref,
                     y_hbm, xbuf, ybuf, gsem, ssem):
    rows = xbuf.shape[1]
    j = pl.program_id(0)
    last = pl.num_programs(0) - 1
    slot = lax.rem(j, 2)

    def gather_start(tile, s):
        def one(i, c):
            pltpu.make_async_copy(x_hbm.at[pl.ds(tok_ref[tile * rows + i], 1), :],
                                  xbuf.at[s, pl.ds(i, 1), :], gsem.at[s]).start()
            return c
        lax.fori_loop(0, rows, one, 0, unroll=8)

    def scatter_start(tile, s, n):
        def one(i, c):
            pltpu.make_async_copy(ybuf.at[s, pl.ds(i, 1), :],
                                  y_hbm.at[pl.ds(dst_ref[tile * rows + i], 1), :], ssem.at[s]).start()
            return c
        lax.fori_loop(0, n, one, 0)

    def scatter_wait(s, n):
        bit = rows
        while bit >= 1:
            @pl.when((n & bit) != 0)
            def _(bit=bit):
                pltpu.make_async_copy(ybuf.at[s, pl.ds(0, bit), :], y_hbm.at[pl.ds(0, bit), :],
                                      ssem.at[s]).wait()
            bit //= 2

    @pl.when((j == 0) & (tr_ref[0] > 0))
    def _():
        gather_start(0, 0)

    nxt = jnp.minimum(j + 1, last)

    @pl.when((j < last) & (tr_ref[nxt] > 0))
    def _():
        gather_start(nxt, 1 - slot)

    @pl.when(tr_ref[j] > 0)
    def _():
        pltpu.make_async_copy(x_hbm.at[pl.ds(0, rows), :], xbuf.at[slot], gsem.at[slot]).wait()
        h = _rmsnorm(xbuf[slot], g_ref[...]).astype(BF16)
        a = jnp.dot(h, w1_ref[...], preferred_element_type=F32)
        b = jnp.dot(h, w3_ref[...], preferred_element_type=F32)
        hid = (jax.nn.silu(a) * b).astype(BF16)
        ybuf[slot] = (jnp.dot(hid, w2_ref[...], preferred_element_type=F32) * gate_ref[...]
                      + 0.5 * xbuf[slot])
        scatter_start(j, slot, tr_ref[j])

    prv = jnp.maximum(j - 1, 0)

    @pl.when(j >= 1)
    def _():
        scatter_wait(1 - slot, tr_ref[prv])

    @pl.when(j == last)
    def _():
        scatter_wait(slot, tr_ref[j])


def moe_experts(x, g, plan, w1, w3, w2, e0, wi2):
    tile_e, tile_rows, row_token, row_dst, row_gate = plan
    t, d = x.shape
    f = w1.shape[2]
    n_tiles = tile_e.shape[0]
    once = pl.Buffered(1)
    grid_spec = pltpu.PrefetchScalarGridSpec(
        num_scalar_prefetch=4,
        grid=(n_tiles,),
        in_specs=[pl.BlockSpec(memory_space=pl.ANY),
                  pl.BlockSpec((1, d), lambda j, te, tr, tok, dst: (0, 0)),
                  pl.BlockSpec((MOE_ROWS, 1), lambda j, te, tr, tok, dst: (j, 0)),
                  pl.BlockSpec((None, d, f), lambda j, te, tr, tok, dst: (e0 + te[j], 0, 0),
                               pipeline_mode=once),
                  pl.BlockSpec((None, d, f), lambda j, te, tr, tok, dst: (e0 + te[j], 0, 0),
                               pipeline_mode=once),
                  pl.BlockSpec((None, f, d), lambda j, te, tr, tok, dst: (wi2, te[j], 0),
                               pipeline_mode=once)],
        out_specs=pl.BlockSpec(memory_space=pl.ANY),
        scratch_shapes=[pltpu.VMEM((2, MOE_ROWS, d), F32), pltpu.VMEM((2, MOE_ROWS, d), F32),
                        pltpu.SemaphoreType.DMA((2,)), pltpu.SemaphoreType.DMA((2,))])
    return pl.pallas_call(
        _moe_expert_body,
        grid_spec=grid_spec,
        out_shape=jax.ShapeDtypeStruct((TOP_K * t, d), F32),
        compiler_params=_cparams("arbitrary"),
        name="moe_experts",
    )(tile_e, tile_rows, row_token, row_dst, x, g.reshape(1, d), row_gate, w1, w3, w2)


def _moe_combine_body(ya_ref, yb_ref, *rest, norm):
    if norm:
        g_ref, o_ref = rest
    else:
        (o_ref,) = rest
    o_ref[...] = ya_ref[...] + yb_ref[...]
    if norm:
        _rmsnorm_rows(o_ref, g_ref, o_ref)


def moe_combine(y2, g=None):
    t, d = y2.shape[0] // TOP_K, y2.shape[1]
    tm = _tile(t, 512)
    y3 = y2.reshape(TOP_K, t, d)
    in_specs = [pl.BlockSpec((None, tm, d), lambda m: (0, m, 0)),
                pl.BlockSpec((None, tm, d), lambda m: (1, m, 0))]
    args = [y3, y3]
    if g is not None:
        in_specs.append(pl.BlockSpec((1, d), lambda m: (0, 0)))
        args.append(g.reshape(1, d))
    return pl.pallas_call(
        functools.partial(_moe_combine_body, norm=g is not None),
        grid=(t // tm,),
        in_specs=in_specs,
        out_specs=pl.BlockSpec((tm, d), lambda m: (m, 0)),
        out_shape=jax.ShapeDtypeStruct((t, d), F32),
        compiler_params=_cparams("arbitrary"),
        name="moe_combine",
    )(*args)


def _final_norm_body(x_ref, g_ref, o_ref):
    _rmsnorm_rows(x_ref, g_ref, o_ref)


def final_norm(x, g):
    t, d = x.shape
    tm = _tile(t, 512)
    return pl.pallas_call(
        _final_norm_body,
        grid=(t // tm,),
        in_specs=[pl.BlockSpec((tm, d), lambda m: (m, 0)), pl.BlockSpec((1, d), lambda m: (0, 0))],
        out_specs=pl.BlockSpec((tm, d), lambda m: (m, 0)),
        out_shape=jax.ShapeDtypeStruct((t, d), F32),
        compiler_params=_cparams("arbitrary"),
        name="final_norm",
    )(x, g.reshape(1, d))


def kernel(x_prompt, x_sample, cache_k, cache_v, state_ssm_re, state_ssm_im, norm1_g, norm2_g, w_in,
           lam_re, lam_im, log_dt, ssm_b_re, ssm_b_im, ssm_c_re, ssm_c_im, ssm_d, w_glu, w_pa, w_pb,
           b_gate, w_o, ffn_w1, ffn_w3, ffn_w2, moe_router, moe_router_b, moe_w1, moe_w3, moe_w2,
           final_g):
    bp, seq, d = x_prompt.shape
    db, ds, _ = x_sample.shape
    depth = w_in.shape[0]
    _, _, past, n_heads, head_dim = cache_k.shape
    d_attn = n_heads * head_dim
    n_groups, n_state = lam_re.shape[1:]
    d_ssm = n_groups * GROUP
    npst = n_groups * n_state
    assert bp == 1 and db == SUBLANES and seq % SUBLANES == 0

    xp = x_prompt.reshape(seq, d)
    xs = x_sample.reshape(db * ds, d)
    w_in_b, w_glu_b, w_pa_b, w_pb_b, w_o_b = (w.astype(BF16) for w in (w_in, w_glu, w_pa, w_pb, w_o))
    ffn_w1_b, ffn_w3_b, ffn_w2_b = (w.astype(BF16) for w in (ffn_w1, ffn_w3, ffn_w2))
    n_moe, n_exp, _, f_e = moe_w1.shape
    moe_w1_b = moe_w1.astype(BF16).reshape(n_moe * n_exp, d, f_e)
    moe_w3_b = moe_w3.astype(BF16).reshape(n_moe * n_exp, d, f_e)
    moe_w2_b = moe_w2.astype(BF16).reshape(n_moe, n_exp * f_e, d)
    outs = [[] for _ in range(8)]
    for l in range(depth):
        i = l // 2
        coef, ar, ai = _ssm_coef(lam_re[l], lam_im[l], log_dt[l], ssm_b_re[l], ssm_b_im[l],
                                 ssm_c_re[l], ssm_c_im[l], ssm_d[l])

        end_gain = final_g if l == depth - 1 else None

        def channel_mix(x):
            if l % 2 == 0:
                hid = ffn_up(x, norm2_g[l], ffn_w1_b, ffn_w3_b, i)
                x = matmul_res(hid, ffn_w2_b, i, x)
                return x if end_gain is None else final_norm(x, end_gain)
            plan = _moe_plan(moe_route(x, norm2_g[l], moe_router[i], moe_router_b[i]), n_exp)
            y2 = moe_experts(x, norm2_g[l], plan, moe_w1_b, moe_w3_b, moe_w2_b, i * n_exp, i)
            return moe_combine(y2, end_gain)

        q, k, v, kvb, u, gates = norm_proj(xp, norm1_g[l], w_in_b, l, d_attn=d_attn, d_ssm=d_ssm,
                                           head_dim=head_dim, permute_u=True)
        o_a = attention_prompt(q, kvb, n_heads=n_heads, head_dim=head_dim)
        zeros = jnp.zeros((SUBLANES, npst), F32)
        _, fr, fi = ssm_scan(u, coef, zeros, zeros, emit_y=False)
        h0r, h0i = _segment_inits(ar, ai, fr, fi, seq // SUBLANES)
        y, hr, hi = ssm_scan(u, coef, h0r, h0i, emit_y=True)
        merged = mixer_merge(o_a, y, gates, b_gate[l], w_glu_b, w_pa_b, w_pb_b, l, y_permuted=True)
        xp = matmul_res(merged, w_o_b, l, xp)
        xp = channel_mix(xp)
        outs[0].append(k)
        outs[1].append(v)
        outs[2].append(hr[SUBLANES - 1].reshape(1, n_groups, n_state))
        outs[3].append(hi[SUBLANES - 1].reshape(1, n_groups, n_state))

        q, k, v, kvb, u, gates = norm_proj(xs, norm1_g[l], w_in_b, l, d_attn=d_attn, d_ssm=d_ssm,
                                           head_dim=head_dim, permute_u=False)
        o_a = attention_sample(q, kvb, cache_k, cache_v, l, ds=ds)
        u_t = u.reshape(db, ds, d_ssm).transpose(1, 0, 2).reshape(ds, db * d_ssm)
        y_t, hr, hi = ssm_scan(u_t, coef, state_ssm_re[l].astype(F32).reshape(db, npst),
                               state_ssm_im[l].astype(F32).reshape(db, npst), emit_y=True)
        y = y_t.reshape(ds, db, d_ssm).transpose(1, 0, 2).reshape(db * ds, d_ssm)
        merged = mixer_merge(o_a, y, gates, b_gate[l], w_glu_b, w_pa_b, w_pb_b, l, y_permuted=False)
        xs = matmul_res(merged, w_o_b, l, xs)
        xs = channel_mix(xs)
        outs[4].append(k)
        outs[5].append(v)
        outs[6].append(hr.reshape(db, n_groups, n_state))
        outs[7].append(hi.reshape(db, n_groups, n_state))

    y_prompt = xp.reshape(bp, seq, d)
    y_sample = xs.reshape(db, ds, d)
    heads = dict(n_heads=n_heads, head_dim=head_dim)
    kv_outs = {0: stack_heads(outs[0], bp, **heads), 1: stack_heads(outs[1], bp, **heads),
               4: stack_heads(outs[4], db, **heads), 5: stack_heads(outs[5], db, **heads)}
    return (y_prompt, y_sample) + tuple(kv_outs[j] if j in kv_outs else jnp.stack(outs[j])
                                        for j in range(8))
```

```python
import functools
import math

import jax
import jax.numpy as jnp
from jax import lax
from jax.experimental import pallas as pl
from jax.experimental.pallas import tpu as pltpu

F32 = jnp.float32
BF16 = jnp.bfloat16

RMS_EPS = 1e-6
TOP_K = 2
GROUP = 16
SUBLANES = 8
V7X_VMEM_LIMIT_BYTES = 56 * 1024 * 1024
SSM_KB_GROUPS = 16
SSM_STEPS_PER_BLOCK = 64


def _cparams(*sem):
    return pltpu.CompilerParams(dimension_semantics=sem, vmem_limit_bytes=V7X_VMEM_LIMIT_BYTES)


def _tile(dim, pref):
    if dim <= pref:
        return dim
    t = pref - pref % 128
    while t >= 128:
        if dim % t == 0:
            return t
        t -= 128
    return dim


def _rmsnorm(x, g):
    ms = jnp.mean(x * x, axis=-1, keepdims=True)
    return (x * lax.rsqrt(ms + RMS_EPS)) * g


NORM_ROWS = 128


def _rmsnorm_rows(x_ref, g_ref, o_ref):
    rows = x_ref.shape[0]
    step = min(NORM_ROWS, rows)
    g = g_ref[...]

    def body(c, carry):
        r0 = pl.multiple_of(c * step, step)
        o_ref[pl.ds(r0, step), :] = _rmsnorm(x_ref[pl.ds(r0, step), :], g).astype(o_ref.dtype)
        return carry

    lax.fori_loop(0, rows // step, body, 0)


def _load_norm_rows(x_hbm, g_ref, xbuf, sem, h_scr):
    m, n = pl.program_id(0), pl.program_id(1)
    tm = xbuf.shape[0]

    def copy(mi):
        return pltpu.make_async_copy(x_hbm.at[pl.ds(pl.multiple_of(mi * tm, tm), tm), :], xbuf, sem)

    @pl.when((m == 0) & (n == 0))
    def _():
        copy(0).start()

    @pl.when(n == 0)
    def _():
        copy(m).wait()
        _rmsnorm_rows(xbuf, g_ref, h_scr)

    @pl.when((n == 1) & (m + 1 < pl.num_programs(0)))
    def _():
        copy(m + 1).start()


def _norm_proj_body(x_hbm, g_ref, w_ref, q_ref, k_ref, v_ref, kvb_ref, u_ref, gate_ref, h_scr, xbuf, sem,
                    *, nq, ns, q_scale):
    n = pl.program_id(1)
    _load_norm_rows(x_hbm, g_ref, xbuf, sem, h_scr)

    def z():
        return jnp.dot(h_scr[...], w_ref[...], preferred_element_type=F32)

    @pl.when(n < nq)
    def _():
        q_ref[...] = (z() * q_scale).astype(BF16)

    @pl.when((n >= nq) & (n < 2 * nq))
    def _():
        zz = z()
        k_ref[...] = zz
        kvb_ref[...] = zz.astype(BF16)

    @pl.when((n >= 2 * nq) & (n < 3 * nq))
    def _():
        zz = z()
        v_ref[...] = zz
        kvb_ref[...] = zz.astype(BF16)

    @pl.when((n >= 3 * nq) & (n < 3 * nq + ns))
    def _():
        u_ref[...] = z()

    @pl.when(n >= 3 * nq + ns)
    def _():
        gate_ref[...] = z().astype(BF16)


def norm_proj(x, g, w, l, *, d_attn, d_ssm, head_dim, permute_u):
    t, d = x.shape
    d_in = w.shape[2]
    tn = _tile(d_ssm, 512)
    assert d_attn % tn == 0 and d % tn == 0
    nq, ns = d_attn // tn, d_ssm // tn
    ng = 2 * d // tn
    assert d_in == (3 * nq + ns + ng) * tn
    tm = _tile(t // SUBLANES if permute_u else t, 1024)
    if permute_u:
        seg = t // SUBLANES
        assert seg % tm == 0
        mps = seg // tm
        u_shape = (seg, SUBLANES * d_ssm)
        u_map = lambda m, n: (m % mps, (m // mps) * ns + jnp.clip(n - 3 * nq, 0, ns - 1))
    else:
        u_shape = (t, d_ssm)
        u_map = lambda m, n: (m, jnp.clip(n - 3 * nq, 0, ns - 1))
    blk = lambda imap: pl.BlockSpec((tm, tn), imap)
    outs = pl.pallas_call(
        functools.partial(_norm_proj_body, nq=nq, ns=ns, q_scale=head_dim ** -0.5),
        grid=(t // tm, d_in // tn),
        in_specs=[pl.BlockSpec(memory_space=pl.ANY),
                  pl.BlockSpec((1, d), lambda m, n: (0, 0)),
                  pl.BlockSpec((None, d, tn), lambda m, n: (l, 0, n))],
        out_specs=[blk(lambda m, n: (m, jnp.minimum(n, nq - 1))),
                   blk(lambda m, n: (m, jnp.clip(n - nq, 0, nq - 1))),
                   blk(lambda m, n: (m, jnp.clip(n - 2 * nq, 0, nq - 1))),
                   blk(lambda m, n: (m, jnp.clip(n - nq, 0, 2 * nq - 1))),
                   blk(u_map),
                   blk(lambda m, n: (m, jnp.clip(n - 3 * nq - ns, 0, ng - 1)))],
        out_shape=[jax.ShapeDtypeStruct((t, d_attn), BF16),
                   jax.ShapeDtypeStruct((t, d_attn), F32),
                   jax.ShapeDtypeStruct((t, d_attn), F32),
                   jax.ShapeDtypeStruct((t, 2 * d_attn), BF16),
                   jax.ShapeDtypeStruct(u_shape, F32),
                   jax.ShapeDtypeStruct((t, 2 * d), BF16)],
        scratch_shapes=[pltpu.VMEM((tm, d), BF16), pltpu.VMEM((tm, d), F32), pltpu.SemaphoreType.DMA(())],
        compiler_params=_cparams("arbitrary", "arbitrary"),
        name="norm_proj",
    )(x, g.reshape(1, d), w)
    return outs


def _softplus(z):
    return jnp.maximum(z, 0.0) + jnp.log(1.0 + jnp.exp(-jnp.abs(z)))


def _tri(tk):
    r = lax.broadcasted_iota(jnp.int32, (tk, tk), 0)
    c = lax.broadcasted_iota(jnp.int32, (tk, tk), 1)
    return (r >= c).astype(BF16)


def _sb_tile(q, kb, vb, tri, later, mask):
    z = lax.dot_general(q, kb, (((1,), (1,)), ((), ())), preferred_element_type=F32)
    sp = _softplus(z)
    if mask is not None:
        sp = jnp.where(mask, sp, 0.0)
    hi = sp.astype(BF16)
    lo = (sp - hi.astype(F32)).astype(BF16)
    within = (jnp.dot(hi, tri, preferred_element_type=F32)
              + jnp.dot(lo, tri, preferred_element_type=F32))
    w = jnp.exp(jnp.minimum(z - (within + later), 0.0))
    if mask is not None:
        w = jnp.where(mask, w, 0.0)
    pv = jnp.dot(w.astype(BF16), vb, preferred_element_type=F32)
    return pv, later + within[:, 0:1]


SB_ZERO_EXPONENT = 110.0
SB_NORM_SLACK = 1.001
KNORM_ROWS = 512


def _attn_prompt_body(q_ref, k_ref, v_ref, o_ref, acc_scr, kmax_scr, *, tq, tk):
    i = pl.program_id(1)
    nsub = tq // tk
    base = i * nsub
    tri = _tri(tk)
    later = jnp.zeros((tq, 1), F32)
    acc_scr[...] = jnp.zeros_like(acc_scr)

    @pl.when(i == 0)
    def _():
        step = min(KNORM_ROWS, k_ref.shape[0])

        def knorm(c, m):
            kf = k_ref[pl.ds(pl.multiple_of(c * step, step), step), :].astype(F32)
            return jnp.maximum(m, jnp.sum(kf * kf, axis=1, keepdims=True))

        m = lax.fori_loop(0, k_ref.shape[0] // step, knorm, jnp.zeros((step, 1), F32))
        kmax_scr[...] = jnp.broadcast_to(jnp.sqrt(jnp.max(m, axis=0, keepdims=True)), kmax_scr.shape)

    qf = q_ref[...].astype(F32)
    zmax = (jnp.sqrt(jnp.sum(qf * qf, axis=1, keepdims=True)) * kmax_scr[0:1, 0:1]) * SB_NORM_SLACK

    for jr in range(nsub - 1, -1, -1):
        r0 = jr * tk
        k0 = pl.multiple_of((base + jr) * tk, tk)
        row = lax.broadcasted_iota(jnp.int32, (tq - r0, tk), 0)
        col = lax.broadcasted_iota(jnp.int32, (tq - r0, tk), 1)
        pv, lat = _sb_tile(q_ref[r0:, :], k_ref[pl.ds(k0, tk), :], v_ref[pl.ds(k0, tk), :], tri,
                           later[r0:], col < row)
        acc_scr[r0:, :] += pv
        later = lat if r0 == 0 else jnp.concatenate([later[:r0], lat], axis=0)

    def live(later):
        return (jnp.min(later - zmax) <= SB_ZERO_EXPONENT).astype(jnp.int32)

    def cond(state):
        s, _, go = state
        return (s < base) & (go > 0)

    def body(state):
        s, later, _ = state
        j0 = pl.multiple_of((base - 1 - s) * tk, tk)
        pv, later = _sb_tile(q_ref[...], k_ref[pl.ds(j0, tk), :], v_ref[pl.ds(j0, tk), :], tri,
                             later, None)
        acc_scr[...] += pv
        return s + 1, later, live(later)

    lax.while_loop(cond, body, (jnp.int32(0), later, live(later)))
    o_ref[...] = acc_scr[...].astype(o_ref.dtype)


ATTN_Q_ROWS = 1024
ATTN_K_ROWS = 256


def attention_prompt(q, kvb, *, n_heads, head_dim):
    t = q.shape[0]
    tk = _tile(t, ATTN_K_ROWS)
    tq = _tile(t, ATTN_Q_ROWS)
    assert tq % tk == 0
    single = pl.Buffered(1)
    return pl.pallas_call(
        functools.partial(_attn_prompt_body, tq=tq, tk=tk),
        grid=(n_heads, t // tq),
        in_specs=[pl.BlockSpec((tq, head_dim), lambda h, i: (i, h)),
                  pl.BlockSpec((t, head_dim), lambda h, i: (0, h), pipeline_mode=single),
                  pl.BlockSpec((t, head_dim), lambda h, i: (0, n_heads + h), pipeline_mode=single)],
        out_specs=pl.BlockSpec((tq, head_dim), lambda h, i: (i, h)),
        out_shape=jax.ShapeDtypeStruct((t, n_heads * head_dim), BF16),
        scratch_shapes=[pltpu.VMEM((tq, head_dim), F32), pltpu.VMEM((SUBLANES, 128), F32)],
        compiler_params=_cparams("arbitrary", "arbitrary"),
        name="attn_prompt",
    )(q, kvb, kvb)


def _attn_sample_body(q_ref, kn_ref, vn_ref, ck_hbm, cv_hbm, o_ref, kbuf, vbuf, sem, *, l, ds, tk):
    n_heads = pl.num_programs(1)
    step = pl.program_id(0) * n_heads + pl.program_id(1)
    slot = lax.rem(step, 2)

    def fetch(s, sl):
        b, h = s // n_heads, lax.rem(s, n_heads)
        return (pltpu.make_async_copy(ck_hbm.at[l, b, :, h, :], kbuf.at[sl], sem.at[0, sl]),
                pltpu.make_async_copy(cv_hbm.at[l, b, :, h, :], vbuf.at[sl], sem.at[1, sl]))

    @pl.when(step == 0)
    def _():
        for cp in fetch(0, 0):
            cp.start()

    @pl.when(step + 1 < pl.num_programs(0) * n_heads)
    def _():
        for cp in fetch(step + 1, 1 - slot):
            cp.start()

    for cp in fetch(step, slot):
        cp.wait()

    q = q_ref[...]
    row = lax.broadcasted_iota(jnp.int32, (ds, ds), 0)
    col = lax.broadcasted_iota(jnp.int32, (ds, ds), 1)
    acc, later = _sb_tile(q, kn_ref[...], vn_ref[...], _tri(ds), jnp.zeros((ds, 1), F32), col < row)
    tri = _tri(tk)
    for j in range(kbuf.shape[1] // tk - 1, -1, -1):
        kb = kbuf[slot, j * tk:(j + 1) * tk, :].astype(BF16)
        vb = vbuf[slot, j * tk:(j + 1) * tk, :].astype(BF16)
        pv, later = _sb_tile(q, kb, vb, tri, later, None)
        acc = acc + pv
    o_ref[...] = acc.astype(o_ref.dtype)


def attention_sample(q, kvb, cache_k, cache_v, l, *, ds):
    _, b, past, n_heads, head_dim = cache_k.shape
    tk = _tile(past, ATTN_K_ROWS)
    return pl.pallas_call(
        functools.partial(_attn_sample_body, l=l, ds=ds, tk=tk),
        grid=(b, n_heads),
        in_specs=[pl.BlockSpec((ds, head_dim), lambda bi, h: (bi, h)),
                  pl.BlockSpec((ds, head_dim), lambda bi, h: (bi, h)),
                  pl.BlockSpec((ds, head_dim), lambda bi, h: (bi, n_heads + h)),
                  pl.BlockSpec(memory_space=pl.ANY), pl.BlockSpec(memory_space=pl.ANY)],
        out_specs=pl.BlockSpec((ds, head_dim), lambda bi, h: (bi, h)),
        out_shape=jax.ShapeDtypeStruct((b * ds, n_heads * head_dim), BF16),
        scratch_shapes=[pltpu.VMEM((2, past, head_dim), F32), pltpu.VMEM((2, past, head_dim), F32),
                        pltpu.SemaphoreType.DMA((2, 2))],
        compiler_params=_cparams("arbitrary", "arbitrary"),
        name="attn_sample",
    )(q, kvb, kvb, cache_k, cache_v)


def _stack_heads_body(*refs, n_heads, head_dim):
    *in_refs, o_ref = refs
    l = pl.program_id(0)
    for i, x_ref in enumerate(in_refs):
        @pl.when(l == i)
        def _(x_ref=x_ref):
            for h in range(n_heads):
                o_ref[:, h, :] = x_ref[:, h * head_dim:(h + 1) * head_dim]


def stack_heads(xs, batch, *, n_heads, head_dim):
    depth = len(xs)
    t = xs[0].shape[0]
    tm = _tile(t, 512)
    nm = t // tm
    in_specs = [pl.BlockSpec((tm, n_heads * head_dim),
                             lambda l, m, i=i: (jnp.clip(m + (l - i) * nm, 0, nm - 1), 0))
                for i in range(depth)]
    out = pl.pallas_call(
        functools.partial(_stack_heads_body, n_heads=n_heads, head_dim=head_dim),
        grid=(depth, nm),
        in_specs=in_specs,
        out_specs=pl.BlockSpec((None, None, tm, n_heads, head_dim), lambda l, m: (l, 0, m, 0, 0)),
        out_shape=jax.ShapeDtypeStruct((depth, 1, t, n_heads, head_dim), F32),
        compiler_params=_cparams("arbitrary", "arbitrary"),
        name="stack_heads",
    )(*xs)
    return out.reshape(depth, batch, t // batch, n_heads, head_dim)


def _ssm_body(u_ref, bdr_ref, bdi_ref, cdr_ref, cdi_ref, ar_ref, ai_ref, d_ref, h0r_ref, h0i_ref,
              *rest, n, nkb, emit_y):
    if emit_y:
        y_ref, hr_ref, hi_ref, str_scr, sti_scr, xr_scr, xi_scr, rows_scr = rest
    else:
        hr_ref, hi_ref, str_scr, sti_scr, xr_scr, xi_scr, rows_scr = rest
    step = pl.program_id(0)
    kin = bdr_ref.shape[1]
    kst = bdr_ref.shape[2]

    @pl.when(step == 0)
    def _():
        str_scr[...] = h0r_ref[...]
        sti_scr[...] = h0i_ref[...]

    d_ssm = nkb * kin
    for r in range(SUBLANES):
        rows_scr[:, r, :] = u_ref[:, r * d_ssm:(r + 1) * d_ssm]
    u = rows_scr[...].reshape(n * SUBLANES, d_ssm)
    ub = u.astype(BF16)
    for kb in range(nkb):
        ukb = ub[:, kb * kin:(kb + 1) * kin]
        xr_scr[:, kb * kst:(kb + 1) * kst] = jnp.dot(ukb, bdr_ref[kb], preferred_element_type=F32)
        xi_scr[:, kb * kst:(kb + 1) * kst] = jnp.dot(ukb, bdi_ref[kb], preferred_element_type=F32)

    for kb in range(nkb):
        cs = slice(kb * kst, (kb + 1) * kst)
        ar = jnp.broadcast_to(ar_ref[:, cs], (SUBLANES, kst))
        ai = jnp.broadcast_to(ai_ref[:, cs], (SUBLANES, kst))

        def scan_step(tt, carry, cs=cs, ar=ar, ai=ai):
            xr, xi = carry
            r0 = pl.multiple_of(tt * SUBLANES, SUBLANES)
            nxr = ar * xr - ai * xi + xr_scr[pl.ds(r0, SUBLANES), cs]
            nxi = ar * xi + ai * xr + xi_scr[pl.ds(r0, SUBLANES), cs]
            xr_scr[pl.ds(r0, SUBLANES), cs] = nxr
            xi_scr[pl.ds(r0, SUBLANES), cs] = nxi
            return nxr, nxi

        xr, xi = lax.fori_loop(0, n, scan_step, (str_scr[:, cs], sti_scr[:, cs]), unroll=4)
        str_scr[:, cs] = xr
        sti_scr[:, cs] = xi

    if emit_y:
        ys = []
        for kb in range(nkb):
            cs = slice(kb * kst, (kb + 1) * kst)
            yk = (jnp.dot(xr_scr[:, cs].astype(BF16), cdr_ref[kb], preferred_element_type=F32)
                  - jnp.dot(xi_scr[:, cs].astype(BF16), cdi_ref[kb], preferred_element_type=F32))
            os_ = slice(kb * kin, (kb + 1) * kin)
            ys.append(yk + d_ref[:, os_] * u[:, os_])
        rows_scr[...] = jnp.concatenate(ys, axis=1).reshape(n, SUBLANES, d_ssm)
        for r in range(SUBLANES):
            y_ref[:, r * d_ssm:(r + 1) * d_ssm] = rows_scr[:, r, :]

    @pl.when(step == pl.num_programs(0) - 1)
    def _():
        hr_ref[...] = str_scr[...]
        hi_ref[...] = sti_scr[...]


def ssm_scan(u, coef, h0r, h0i, *, emit_y):
    bdr, bdi, cdr, cdi, ar, ai, dsk = coef
    steps, width = u.shape
    d_ssm = width // SUBLANES
    nkb, kin, kst = bdr.shape
    npst = nkb * kst
    n = math.gcd(steps, SSM_STEPS_PER_BLOCK)
    rows = n * SUBLANES
    full = lambda a: pl.BlockSpec(a.shape, lambda s: (0,) * a.ndim)
    out_specs = [pl.BlockSpec((SUBLANES, npst), lambda s: (0, 0))] * 2
    out_shape = [jax.ShapeDtypeStruct((SUBLANES, npst), F32)] * 2
    if emit_y:
        out_specs = [pl.BlockSpec((n, width), lambda s: (s, 0))] + out_specs
        out_shape = [jax.ShapeDtypeStruct((steps, width), F32)] + out_shape
    outs = pl.pallas_call(
        functools.partial(_ssm_body, n=n, nkb=nkb, emit_y=emit_y),
        grid=(steps // n,),
        in_specs=[pl.BlockSpec((n, width), lambda s: (s, 0)),
                  full(bdr), full(bdi), full(cdr), full(cdi), full(ar), full(ai), full(dsk),
                  full(h0r), full(h0i)],
        out_specs=out_specs,
        out_shape=out_shape,
        scratch_shapes=[pltpu.VMEM((SUBLANES, npst), F32), pltpu.VMEM((SUBLANES, npst), F32),
                        pltpu.VMEM((rows, npst), F32), pltpu.VMEM((rows, npst), F32),
                        pltpu.VMEM((n, SUBLANES, d_ssm), F32)],
        compiler_params=_cparams("arbitrary"),
        name="ssm_scan_y" if emit_y else "ssm_scan_state",
    )(u, bdr, bdi, cdr, cdi, ar, ai, dsk, h0r, h0i)
    if emit_y:
        return outs[0], outs[1], outs[2]
    return None, outs[0], outs[1]


def _ssm_coef(lam_re, lam_im, log_dt, b_re, b_im, c_re, c_im, d_skip):
    g, p = lam_re.shape
    dt = jnp.exp(log_dt.astype(F32))[:, None]
    lr, li = lam_re.astype(F32), lam_im.astype(F32)
    mag = jnp.exp(lr * dt)
    ar = mag * jnp.cos(li * dt)
    ai = mag * jnp.sin(li * dt)
    den = lr * lr + li * li
    nr = ar - 1.0
    cr = (nr * lr + ai * li) / den
    ci = (ai * lr - nr * li) / den
    br, bi = b_re.astype(F32), b_im.astype(F32)
    bbr = cr[..., None] * br - ci[..., None] * bi
    bbi = cr[..., None] * bi + ci[..., None] * br
    gb = min(SSM_KB_GROUPS, g)
    nkb = g // gb
    eye = jnp.eye(gb, dtype=F32)

    def pack_b(m):
        m = m.reshape(nkb, gb, p, GROUP)
        return jnp.einsum('kaph,ab->kahbp', m, eye).reshape(nkb, gb * GROUP, gb * p).astype(BF16)

    def pack_c(m):
        m = m.reshape(nkb, gb, GROUP, p)
        return jnp.einsum('kahp,ab->kapbh', m, eye).reshape(nkb, gb * p, gb * GROUP).astype(BF16)

    coef = (pack_b(bbr), pack_b(bbi), pack_c(c_re.astype(F32)), pack_c(c_im.astype(F32)),
            ar.reshape(1, g * p), ai.reshape(1, g * p), d_skip.astype(F32).reshape(1, g * GROUP))
    return coef, ar, ai


def _segment_inits(ar, ai, fr, fi, seg_len):
    k = int(round(math.log2(seg_len)))
    assert 2 ** k == seg_len
    pr, pi = ar.reshape(1, -1), ai.reshape(1, -1)
    for _ in range(k):
        pr, pi = pr * pr - pi * pi, 2.0 * pr * pi
    hr = jnp.zeros_like(fr[0:1])
    hi = jnp.zeros_like(fi[0:1])
    hrs, his = [hr], [hi]
    for r in range(SUBLANES - 1):
        hr, hi = pr * hr - pi * hi + fr[r:r + 1], pr * hi + pi * hr + fi[r:r + 1]
        hrs.append(hr)
        his.append(hi)
    return jnp.concatenate(hrs, axis=0), jnp.concatenate(his, axis=0)


def _merge_body(oa_ref, y_ref, ga_ref, gb_ref, ba_ref, bb_ref, wglu_ref, wpa_ref, wpb_ref, o_ref,
                ob_scr):
    n = pl.program_id(1)

    @pl.when(n == 0)
    def _():
        gy = jax.nn.gelu(y_ref[...])
        glu = jnp.dot(gy.astype(BF16), wglu_ref[...], preferred_element_type=F32)
        ob_scr[...] = (gy * jax.nn.sigmoid(glu)).astype(BF16)

    pa = jnp.dot(oa_ref[...], wpa_ref[...], preferred_element_type=F32)
    pb = jnp.dot(ob_scr[...], wpb_ref[...], preferred_element_type=F32)
    gate_a = jax.nn.sigmoid(ga_ref[...].astype(F32) + ba_ref[...])
    gate_b = jax.nn.sigmoid(gb_ref[...].astype(F32) + bb_ref[...])
    o_ref[...] = (gate_a * pa + gate_b * pb).astype(o_ref.dtype)


def mixer_merge(o_a, y, gates, b_gate, w_glu, w_pa, w_pb, l, *, y_permuted):
    t, d_attn = o_a.shape
    d_ssm = w_glu.shape[1]
    d = w_pa.shape[2]
    tm = _tile(t // SUBLANES if y_permuted else t, 1024)
    tn = _tile(d, 1024)
    nd = d // tn
    if y_permuted:
        seg = t // SUBLANES
        mps = seg // tm
        assert y.shape == (seg, SUBLANES * d_ssm)
        y_spec = pl.BlockSpec((tm, d_ssm), lambda m, n: (m % mps, m // mps))
    else:
        y_spec = pl.BlockSpec((tm, d_ssm), lambda m, n: (m, 0))
    bg = b_gate.reshape(1, 2 * d)
    return pl.pallas_call(
        _merge_body,
        grid=(t // tm, nd),
        in_specs=[pl.BlockSpec((tm, d_attn), lambda m, n: (m, 0)),
                  y_spec,
                  pl.BlockSpec((tm, tn), lambda m, n: (m, n)),
                  pl.BlockSpec((tm, tn), lambda m, n: (m, nd + n)),
                  pl.BlockSpec((1, tn), lambda m, n: (0, n)),
                  pl.BlockSpec((1, tn), lambda m, n: (0, nd + n)),
                  pl.BlockSpec((None, d_ssm, d_ssm), lambda m, n: (l, 0, 0)),
                  pl.BlockSpec((None, d_attn, tn), lambda m, n: (l, 0, n)),
                  pl.BlockSpec((None, d_ssm, tn), lambda m, n: (l, 0, n))],
        out_specs=pl.BlockSpec((tm, tn), lambda m, n: (m, n)),
        out_shape=jax.ShapeDtypeStruct((t, d), BF16),
        scratch_shapes=[pltpu.VMEM((tm, d_ssm), BF16)],
        compiler_params=_cparams("arbitrary", "arbitrary"),
        name="mixer_merge",
    )(o_a, y, gates, gates, bg, bg, w_glu, w_pa, w_pb)


MATMUL_W_TILE_BYTES = 8 * 1024 * 1024


def _matmul_res_body(a_ref, w_ref, r_ref, o_ref):
    o_ref[...] = r_ref[...] + jnp.dot(a_ref[...], w_ref[...], preferred_element_type=F32)


def matmul_res(a, w, wi, res):
    t, k = a.shape
    n_out = w.shape[2]
    tm = _tile(t, 1024)
    tn = _tile(n_out, MATMUL_W_TILE_BYTES // (2 * k))
    return pl.pallas_call(
        _matmul_res_body,
        grid=(t // tm, n_out // tn),
        in_specs=[pl.BlockSpec((tm, k), lambda m, n: (m, 0)),
                  pl.BlockSpec((None, k, tn), lambda m, n: (wi, 0, n)),
                  pl.BlockSpec((tm, tn), lambda m, n: (m, n))],
        out_specs=pl.BlockSpec((tm, tn), lambda m, n: (m, n)),
        out_shape=jax.ShapeDtypeStruct((t, n_out), F32),
        compiler_params=_cparams("arbitrary", "arbitrary"),
        name="matmul_res",
    )(a, w, res)


def _router_top2(h, router_ref, rb_ref, n_experts):
    logits = jnp.dot(h, router_ref[...], preferred_element_type=F32) + rb_ref[...]
    lane = lax.broadcasted_iota(jnp.int32, logits.shape, 1).astype(F32)
    neg = jnp.float32(-jnp.inf)
    logits = jnp.where(lane < n_experts, logits, neg)
    big = jnp.float32(logits.shape[1])
    m1 = jnp.max(logits, axis=-1, keepdims=True)
    i1 = jnp.min(jnp.where(logits == m1, lane, big), axis=-1, keepdims=True)
    rest = jnp.where(lane == i1, neg, logits)
    m2 = jnp.max(rest, axis=-1, keepdims=True)
    i2 = jnp.min(jnp.where(rest == m2, lane, big), axis=-1, keepdims=True)
    e2 = jnp.exp(m2 - m1)
    g1 = 1.0 / (1.0 + e2)
    g2 = e2 / (1.0 + e2)
    return (jnp.where(lane == 0.0, i1, 0.0) + jnp.where(lane == 1.0, i2, 0.0)
            + jnp.where(lane == 2.0, g1, 0.0) + jnp.where(lane == 3.0, g2, 0.0))


def _ffn_up_body(x_hbm, g_ref, w1_ref, w3_ref, o_ref, h_scr, xbuf, sem):
    _load_norm_rows(x_hbm, g_ref, xbuf, sem, h_scr)
    h = h_scr[...]
    a = jnp.dot(h, w1_ref[...], preferred_element_type=F32)
    b = jnp.dot(h, w3_ref[...], preferred_element_type=F32)
    o_ref[...] = (jax.nn.silu(a) * b).astype(o_ref.dtype)


def ffn_up(x, g, w1, w3, wi):
    t, d = x.shape
    f = w1.shape[2]
    tm = _tile(t, 1024)
    tn = _tile(f, 512)
    assert f // tn >= 2
    w_spec = pl.BlockSpec((None, d, tn), lambda m, n: (wi, 0, n))
    return pl.pallas_call(
        _ffn_up_body,
        grid=(t // tm, f // tn),
        in_specs=[pl.BlockSpec(memory_space=pl.ANY),
                  pl.BlockSpec((1, d), lambda m, n: (0, 0)), w_spec, w_spec],
        out_specs=pl.BlockSpec((tm, tn), lambda m, n: (m, n)),
        out_shape=jax.ShapeDtypeStruct((t, f), BF16),
        scratch_shapes=[pltpu.VMEM((tm, d), BF16), pltpu.VMEM((tm, d), F32), pltpu.SemaphoreType.DMA(())],
        compiler_params=_cparams("arbitrary", "arbitrary"),
        name="ffn_up",
    )(x, g.reshape(1, d), w1, w3)


MOE_ROWS = 256
ROUTE_LANES = 128


def _moe_route_body(x_ref, g_ref, router_ref, rb_ref, top_ref, h_scr, *, n_experts):
    _rmsnorm_rows(x_ref, g_ref, h_scr)
    top_ref[...] = _router_top2(h_scr[...], router_ref, rb_ref, n_experts)


def moe_route(x, g, router, router_b):
    t, d = x.shape
    n_experts = router.shape[1]
    tm = _tile(t, 512)
    rpad = jnp.zeros((d, ROUTE_LANES), BF16).at[:, :n_experts].set(router.astype(BF16))
    bpad = jnp.zeros((1, ROUTE_LANES), F32).at[0, :n_experts].set(router_b.astype(F32))
    return pl.pallas_call(
        functools.partial(_moe_route_body, n_experts=n_experts),
        grid=(t // tm,),
        in_specs=[pl.BlockSpec((tm, d), lambda m: (m, 0)), pl.BlockSpec((1, d), lambda m: (0, 0)),
                  pl.BlockSpec((d, ROUTE_LANES), lambda m: (0, 0)),
                  pl.BlockSpec((1, ROUTE_LANES), lambda m: (0, 0))],
        out_specs=pl.BlockSpec((tm, ROUTE_LANES), lambda m: (m, 0)),
        out_shape=jax.ShapeDtypeStruct((t, ROUTE_LANES), F32),
        scratch_shapes=[pltpu.VMEM((tm, d), BF16)],
        compiler_params=_cparams("arbitrary"),
        name="moe_route",
    )(x, g.reshape(1, d), rpad, bpad)


def _moe_plan(top, n_experts):
    t = top.shape[0]
    n_slots = TOP_K * t
    n_tiles = (n_slots + n_experts * (MOE_ROWS - 1)) // MOE_ROWS
    e = top[:, :TOP_K].astype(jnp.int32).reshape(n_slots)
    gate = top[:, TOP_K:2 * TOP_K].reshape(n_slots)
    slot = jnp.arange(n_slots, dtype=jnp.int32)
    order = jnp.argsort(e * n_slots + slot).astype(jnp.int32)
    counts = jnp.sum((e[:, None] == jnp.arange(n_experts)[None, :]).astype(jnp.int32), axis=0)
    padded = ((counts + MOE_ROWS - 1) // MOE_ROWS) * MOE_ROWS
    pend = jnp.cumsum(padded)
    pstart = pend - padded
    cstart = jnp.cumsum(counts) - counts
    row = jnp.arange(n_tiles * MOE_ROWS, dtype=jnp.int32)
    row_e = jnp.minimum(jnp.sum((row[:, None] >= pend[None, :]).astype(jnp.int32), axis=1), n_experts - 1)
    rank = row - pstart[row_e]
    real = rank < counts[row_e]
    row_slot = order[jnp.clip(cstart[row_e] + rank, 0, n_slots - 1)]
    row_token = jnp.where(real, row_slot // TOP_K, 0).astype(jnp.int32)
    row_dst = jnp.where(real, (row_slot % TOP_K) * t + row_slot // TOP_K, 0).astype(jnp.int32)
    row_gate = jnp.where(real, gate[row_slot], 0.0).reshape(-1, 1)
    tile0 = jnp.arange(n_tiles, dtype=jnp.int32) * MOE_ROWS
    tile_e = row_e[tile0]
    tile_rows = jnp.clip(pstart[tile_e] + counts[tile_e] - tile0, 0, MOE_ROWS)
    tile_rows = jnp.where(tile0 < pend[-1], tile_rows, 0).astype(jnp.int32)
    return tile_e, tile_rows, row_token, row_dst, row_gate


def _moe_expert_body(te_ref, tr_ref, tok_ref, dst_ref, x_hbm, g_ref, gate_ref, w1_ref, w3_ref, w2_ref,
                     y_hbm, xbuf, ybuf, gsem, ssem):
    rows = xbuf.shape[1]
    j = pl.program_id(0)
    last = pl.num_programs(0) - 1
    slot = lax.rem(j, 2)

    def gather_start(tile, s):
        def one(i, c):
            pltpu.make_async_copy(x_hbm.at[pl.ds(tok_ref[tile * rows + i], 1), :],
                                  xbuf.at[s, pl.ds(i, 1), :], gsem.at[s]).start()
            return c
        lax.fori_loop(0, rows, one, 0, unroll=8)

    def scatter_start(tile, s, n):
        def one(i, c):
            pltpu.make_async_copy(ybuf.at[s, pl.ds(i, 1), :],
                                  y_hbm.at[pl.ds(dst_ref[tile * rows + i], 1), :], ssem.at[s]).start()
            return c
        lax.fori_loop(0, n, one, 0)

    def scatter_wait(s, n):
        bit = rows
        while bit >= 1:
            @pl.when((n & bit) != 0)
            def _(bit=bit):
                pltpu.make_async_copy(ybuf.at[s, pl.ds(0, bit), :], y_hbm.at[pl.ds(0, bit), :],
                                      ssem.at[s]).wait()
            bit //= 2

    @pl.when((j == 0) & (tr_ref[0] > 0))
    def _():
        gather_start(0, 0)

    nxt = jnp.minimum(j + 1, last)

    @pl.when((j < last) & (tr_ref[nxt] > 0))
    def _():
        gather_start(nxt, 1 - slot)

    @pl.when(tr_ref[j] > 0)
    def _():
        pltpu.make_async_copy(x_hbm.at[pl.ds(0, rows), :], xbuf.at[slot], gsem.at[slot]).wait()
        h = _rmsnorm(xbuf[slot], g_ref[...]).astype(BF16)
        a = jnp.dot(h, w1_ref[...], preferred_element_type=F32)
        b = jnp.dot(h, w3_ref[...], preferred_element_type=F32)
        hid = (jax.nn.silu(a) * b).astype(BF16)
        ybuf[slot] = (jnp.dot(hid, w2_ref[...], preferred_element_type=F32) * gate_ref[...]
                      + 0.5 * xbuf[slot])
        scatter_start(j, slot, tr_ref[j])

    prv = jnp.maximum(j - 1, 0)

    @pl.when(j >= 1)
    def _():
        scatter_wait(1 - slot, tr_ref[prv])

    @pl.when(j == last)
    def _():
        scatter_wait(slot, tr_ref[j])


def moe_experts(x, g, plan, w1, w3, w2, e0, wi2):
    tile_e, tile_rows, row_token, row_dst, row_gate = plan
    t, d = x.shape
    f = w1.shape[2]
    n_tiles = tile_e.shape[0]
    once = pl.Buffered(1)
    grid_spec = pltpu.PrefetchScalarGridSpec(
        num_scalar_prefetch=4,
        grid=(n_tiles,),
        in_specs=[pl.BlockSpec(memory_space=pl.ANY),
                  pl.BlockSpec((1, d), lambda j, te, tr, tok, dst: (0, 0)),
                  pl.BlockSpec((MOE_ROWS, 1), lambda j, te, tr, tok, dst: (j, 0)),
                  pl.BlockSpec((None, d, f), lambda j, te, tr, tok, dst: (e0 + te[j], 0, 0),
                               pipeline_mode=once),
                  pl.BlockSpec((None, d, f), lambda j, te, tr, tok, dst: (e0 + te[j], 0, 0),
                               pipeline_mode=once),
                  pl.BlockSpec((None, f, d), lambda j, te, tr, tok, dst: (wi2, te[j], 0),
                               pipeline_mode=once)],
        out_specs=pl.BlockSpec(memory_space=pl.ANY),
        scratch_shapes=[pltpu.VMEM((2, MOE_ROWS, d), F32), pltpu.VMEM((2, MOE_ROWS, d), F32),
                        pltpu.SemaphoreType.DMA((2,)), pltpu.SemaphoreType.DMA((2,))])
    return pl.pallas_call(
        _moe_expert_body,
        grid_spec=grid_spec,
        out_shape=jax.ShapeDtypeStruct((TOP_K * t, d), F32),
        compiler_params=_cparams("arbitrary"),
        name="moe_experts",
    )(tile_e, tile_rows, row_token, row_dst, x, g.reshape(1, d), row_gate, w1, w3, w2)


def _moe_combine_body(ya_ref, yb_ref, *rest, norm):
    if norm:
        g_ref, o_ref = rest
    else:
        (o_ref,) = rest
    o_ref[...] = ya_ref[...] + yb_ref[...]
    if norm:
        _rmsnorm_rows(o_ref, g_ref, o_ref)


def moe_combine(y2, g=None):
    t, d = y2.shape[0] // TOP_K, y2.shape[1]
    tm = _tile(t, 512)
    y3 = y2.reshape(TOP_K, t, d)
    in_specs = [pl.BlockSpec((None, tm, d), lambda m: (0, m, 0)),
                pl.BlockSpec((None, tm, d), lambda m: (1, m, 0))]
    args = [y3, y3]
    if g is not None:
        in_specs.append(pl.BlockSpec((1, d), lambda m: (0, 0)))
        args.append(g.reshape(1, d))
    return pl.pallas_call(
        functools.partial(_moe_combine_body, norm=g is not None),
        grid=(t // tm,),
        in_specs=in_specs,
        out_specs=pl.BlockSpec((tm, d), lambda m: (m, 0)),
        out_shape=jax.ShapeDtypeStruct((t, d), F32),
        compiler_params=_cparams("arbitrary"),
        name="moe_combine",
    )(*args)


def _final_norm_body(x_ref, g_ref, o_ref):
    _rmsnorm_rows(x_ref, g_ref, o_ref)


def final_norm(x, g):
    t, d = x.shape
    tm = _tile(t, 512)
    return pl.pallas_call(
        _final_norm_body,
        grid=(t // tm,),
        in_specs=[pl.BlockSpec((tm, d), lambda m: (m, 0)), pl.BlockSpec((1, d), lambda m: (0, 0))],
        out_specs=pl.BlockSpec((tm, d), lambda m: (m, 0)),
        out_shape=jax.ShapeDtypeStruct((t, d), F32),
        compiler_params=_cparams("arbitrary"),
        name="final_norm",
    )(x, g.reshape(1, d))


def kernel(x_prompt, x_sample, cache_k, cache_v, state_ssm_re, state_ssm_im, norm1_g, norm2_g, w_in,
           lam_re, lam_im, log_dt, ssm_b_re, ssm_b_im, ssm_c_re, ssm_c_im, ssm_d, w_glu, w_pa, w_pb,
           b_gate, w_o, ffn_w1, ffn_w3, ffn_w2, moe_router, moe_router_b, moe_w1, moe_w3, moe_w2,
           final_g):
    bp, seq, d = x_prompt.shape
    db, ds, _ = x_sample.shape
    depth = w_in.shape[0]
    _, _, past, n_heads, head_dim = cache_k.shape
    d_attn = n_heads * head_dim
    n_groups, n_state = lam_re.shape[1:]
    d_ssm = n_groups * GROUP
    npst = n_groups * n_state
    assert bp == 1 and db == SUBLANES and seq % SUBLANES == 0

    xp = x_prompt.reshape(seq, d)
    xs = x_sample.reshape(db * ds, d)
    w_in_b, w_glu_b, w_pa_b, w_pb_b, w_o_b = (w.astype(BF16) for w in (w_in, w_glu, w_pa, w_pb, w_o))
    ffn_w1_b, ffn_w3_b, ffn_w2_b = (w.astype(BF16) for w in (ffn_w1, ffn_w3, ffn_w2))
    n_moe, n_exp, _, f_e = moe_w1.shape
    moe_w1_b = moe_w1.astype(BF16).reshape(n_moe * n_exp, d, f_e)
    moe_w3_b = moe_w3.astype(BF16).reshape(n_moe * n_exp, d, f_e)
    moe_w2_b = moe_w2.astype(BF16).reshape(n_moe, n_exp * f_e, d)
    outs = [[] for _ in range(8)]
    for l in range(depth):
        i = l // 2
        coef, ar, ai = _ssm_coef(lam_re[l], lam_im[l], log_dt[l], ssm_b_re[l], ssm_b_im[l],
                                 ssm_c_re[l], ssm_c_im[l], ssm_d[l])

        end_gain = final_g if l == depth - 1 else None

        def channel_mix(x):
            if l % 2 == 0:
                hid = ffn_up(x, norm2_g[l], ffn_w1_b, ffn_w3_b, i)
                x = matmul_res(hid, ffn_w2_b, i, x)
                return x if end_gain is None else final_norm(x, end_gain)
            plan = _moe_plan(moe_route(x, norm2_g[l], moe_router[i], moe_router_b[i]), n_exp)
            y2 = moe_experts(x, norm2_g[l], plan, moe_w1_b, moe_w3_b, moe_w2_b, i * n_exp, i)
            return moe_combine(y2, end_gain)

        q, k, v, kvb, u, gates = norm_proj(xp, norm1_g[l], w_in_b, l, d_attn=d_attn, d_ssm=d_ssm,
                                           head_dim=head_dim, permute_u=True)
        o_a = attention_prompt(q, kvb, n_heads=n_heads, head_dim=head_dim)
        zeros = jnp.zeros((SUBLANES, npst), F32)
        _, fr, fi = ssm_scan(u, coef, zeros, zeros, emit_y=False)
        h0r, h0i = _segment_inits(ar, ai, fr, fi, seq // SUBLANES)
        y, hr, hi = ssm_scan(u, coef, h0r, h0i, emit_y=True)
        merged = mixer_merge(o_a, y, gates, b_gate[l], w_glu_b, w_pa_b, w_pb_b, l, y_permuted=True)
        xp = matmul_res(merged, w_o_b, l, xp)
        xp = channel_mix(xp)
        outs[0].append(k)
        outs[1].append(v)
        outs[2].append(hr[SUBLANES - 1].reshape(1, n_groups, n_state))
        outs[3].append(hi[SUBLANES - 1].reshape(1, n_groups, n_state))

        q, k, v, kvb, u, gates = norm_proj(xs, norm1_g[l], w_in_b, l, d_attn=d_attn, d_ssm=d_ssm,
                                           head_dim=head_dim, permute_u=False)
        o_a = attention_sample(q, kvb, cache_k, cache_v, l, ds=ds)
        u_t = u.reshape(db, ds, d_ssm).transpose(1, 0, 2).reshape(ds, db * d_ssm)
        y_t, hr, hi = ssm_scan(u_t, coef, state_ssm_re[l].astype(F32).reshape(db, npst),
                               state_ssm_im[l].astype(F32).reshape(db, npst), emit_y=True)
        y = y_t.reshape(ds, db, d_ssm).transpose(1, 0, 2).reshape(db * ds, d_ssm)
        merged = mixer_merge(o_a, y, gates, b_gate[l], w_glu_b, w_pa_b, w_pb_b, l, y_permuted=False)
        xs = matmul_res(merged, w_o_b, l, xs)
        xs = channel_mix(xs)
        outs[4].append(k)
        outs[5].append(v)
        outs[6].append(hr.reshape(db, n_groups, n_state))
        outs[7].append(hi.reshape(db, n_groups, n_state))

    y_prompt = xp.reshape(bp, seq, d)
    y_sample = xs.reshape(db, ds, d)
    heads = dict(n_heads=n_heads, head_dim=head_dim)
    kv_outs = {0: stack_heads(outs[0], bp, **heads), 1: stack_heads(outs[1], bp, **heads),
               4: stack_heads(outs[4], db, **heads), 5: stack_heads(outs[5], db, **heads)}
    return (y_prompt, y_sample) + tuple(kv_outs[j] if j in kv_outs else jnp.stack(outs[j])
                                        for j in range(8))
```

```python
import functools
import math

import jax
import jax.numpy as jnp
from jax import lax
from jax.experimental import pallas as pl
from jax.experimental.pallas import tpu as pltpu

F32 = jnp.float32
BF16 = jnp.bfloat16

RMS_EPS = 1e-6
TOP_K = 2
GROUP = 16
SUBLANES = 8
V7X_VMEM_LIMIT_BYTES = 56 * 1024 * 1024
SSM_KB_GROUPS = 16
SSM_STEPS_PER_BLOCK = 64


def _cparams(*sem):
    return pltpu.CompilerParams(dimension_semantics=sem, vmem_limit_bytes=V7X_VMEM_LIMIT_BYTES)


def _tile(dim, pref):
    if dim <= pref:
        return dim
    t = pref - pref % 128
    while t >= 128:
        if dim % t == 0:
            return t
        t -= 128
    return dim


def _rmsnorm(x, g):
    ms = jnp.mean(x * x, axis=-1, keepdims=True)
    return (x * lax.rsqrt(ms + RMS_EPS)) * g


NORM_ROWS = 128


def _rmsnorm_rows(x_ref, g_ref, o_ref):
    rows = x_ref.shape[0]
    step = min(NORM_ROWS, rows)
    g = g_ref[...]

    def body(c, carry):
        r0 = pl.multiple_of(c * step, step)
        o_ref[pl.ds(r0, step), :] = _rmsnorm(x_ref[pl.ds(r0, step), :], g).astype(o_ref.dtype)
        return carry

    lax.fori_loop(0, rows // step, body, 0)


def _load_norm_rows(x_hbm, g_ref, xbuf, sem, h_scr):
    m, n = pl.program_id(0), pl.program_id(1)
    tm = xbuf.shape[0]

    def copy(mi):
        return pltpu.make_async_copy(x_hbm.at[pl.ds(pl.multiple_of(mi * tm, tm), tm), :], xbuf, sem)

    @pl.when((m == 0) & (n == 0))
    def _():
        copy(0).start()

    @pl.when(n == 0)
    def _():
        copy(m).wait()
        _rmsnorm_rows(xbuf, g_ref, h_scr)

    @pl.when((n == 1) & (m + 1 < pl.num_programs(0)))
    def _():
        copy(m + 1).start()


def _norm_proj_body(x_hbm, g_ref, w_ref, q_ref, k_ref, v_ref, kvb_ref, u_ref, gate_ref, h_scr, xbuf, sem,
                    *, nq, ns, q_scale):
    n = pl.program_id(1)
    _load_norm_rows(x_hbm, g_ref, xbuf, sem, h_scr)

    def z():
        return jnp.dot(h_scr[...], w_ref[...], preferred_element_type=F32)

    @pl.when(n < nq)
    def _():
        q_ref[...] = (z() * q_scale).astype(BF16)

    @pl.when((n >= nq) & (n < 2 * nq))
    def _():
        zz = z()
        k_ref[...] = zz
        kvb_ref[...] = zz.astype(BF16)

    @pl.when((n >= 2 * nq) & (n < 3 * nq))
    def _():
        zz = z()
        v_ref[...] = zz
        kvb_ref[...] = zz.astype(BF16)

    @pl.when((n >= 3 * nq) & (n < 3 * nq + ns))
    def _():
        u_ref[...] = z()

    @pl.when(n >= 3 * nq + ns)
    def _():
        gate_ref[...] = z().astype(BF16)


def norm_proj(x, g, w, l, *, d_attn, d_ssm, head_dim, permute_u):
    t, d = x.shape
    d_in = w.shape[2]
    tn = _tile(d_ssm, 512)
    assert d_attn % tn == 0 and d % tn == 0
    nq, ns = d_attn // tn, d_ssm // tn
    ng = 2 * d // tn
    assert d_in == (3 * nq + ns + ng) * tn
    tm = _tile(t // SUBLANES if permute_u else t, 1024)
    if permute_u:
        seg = t // SUBLANES
        assert seg % tm == 0
        mps = seg // tm
        u_shape = (seg, SUBLANES * d_ssm)
        u_map = lambda m, n: (m % mps, (m // mps) * ns + jnp.clip(n - 3 * nq, 0, ns - 1))
    else:
        u_shape = (t, d_ssm)
        u_map = lambda m, n: (m, jnp.clip(n - 3 * nq, 0, ns - 1))
    blk = lambda imap: pl.BlockSpec((tm, tn), imap)
    outs = pl.pallas_call(
        functools.partial(_norm_proj_body, nq=nq, ns=ns, q_scale=head_dim ** -0.5),
        grid=(t // tm, d_in // tn),
        in_specs=[pl.BlockSpec(memory_space=pl.ANY),
                  pl.BlockSpec((1, d), lambda m, n: (0, 0)),
                  pl.BlockSpec((None, d, tn), lambda m, n: (l, 0, n))],
        out_specs=[blk(lambda m, n: (m, jnp.minimum(n, nq - 1))),
                   blk(lambda m, n: (m, jnp.clip(n - nq, 0, nq - 1))),
                   blk(lambda m, n: (m, jnp.clip(n - 2 * nq, 0, nq - 1))),
                   blk(lambda m, n: (m, jnp.clip(n - nq, 0, 2 * nq - 1))),
                   blk(u_map),
                   blk(lambda m, n: (m, jnp.clip(n - 3 * nq - ns, 0, ng - 1)))],
        out_shape=[jax.ShapeDtypeStruct((t, d_attn), BF16),
                   jax.ShapeDtypeStruct((t, d_attn), F32),
                   jax.ShapeDtypeStruct((t, d_attn), F32),
                   jax.ShapeDtypeStruct((t, 2 * d_attn), BF16),
                   jax.ShapeDtypeStruct(u_shape, F32),
                   jax.ShapeDtypeStruct((t, 2 * d), BF16)],
        scratch_shapes=[pltpu.VMEM((tm, d), BF16), pltpu.VMEM((tm, d), F32), pltpu.SemaphoreType.DMA(())],
        compiler_params=_cparams("arbitrary", "arbitrary"),
        name="norm_proj",
    )(x, g.reshape(1, d), w)
    return outs


def _softplus(z):
    return jnp.maximum(z, 0.0) + jnp.log(1.0 + jnp.exp(-jnp.abs(z)))


def _tri(tk):
    r = lax.broadcasted_iota(jnp.int32, (tk, tk), 0)
    c = lax.broadcasted_iota(jnp.int32, (tk, tk), 1)
    return (r >= c).astype(BF16)


def _sb_tile(q, kb, vb, tri, later, mask):
    z = lax.dot_general(q, kb, (((1,), (1,)), ((), ())), preferred_element_type=F32)
    sp = _softplus(z)
    if mask is not None:
        sp = jnp.where(mask, sp, 0.0)
    hi = sp.astype(BF16)
    lo = (sp - hi.astype(F32)).astype(BF16)
    within = (jnp.dot(hi, tri, preferred_element_type=F32)
              + jnp.dot(lo, tri, preferred_element_type=F32))
    w = jnp.exp(jnp.minimum(z - (within + later), 0.0))
    if mask is not None:
        w = jnp.where(mask, w, 0.0)
    pv = jnp.dot(w.astype(BF16), vb, preferred_element_type=F32)
    return pv, later + within[:, 0:1]


SB_ZERO_EXPONENT = 110.0
SB_NORM_SLACK = 1.001
KNORM_ROWS = 512


def _attn_prompt_body(q_ref, k_ref, v_ref, o_ref, acc_scr, kmax_scr, *, tq, tk):
    i = pl.program_id(1)
    nsub = tq // tk
    base = i * nsub
    tri = _tri(tk)
    later = jnp.zeros((tq, 1), F32)
    acc_scr[...] = jnp.zeros_like(acc_scr)

    @pl.when(i == 0)
    def _():
        step = min(KNORM_ROWS, k_ref.shape[0])

        def knorm(c, m):
            kf = k_ref[pl.ds(pl.multiple_of(c * step, step), step), :].astype(F32)
            return jnp.maximum(m, jnp.sum(kf * kf, axis=1, keepdims=True))

        m = lax.fori_loop(0, k_ref.shape[0] // step, knorm, jnp.zeros((step, 1), F32))
        kmax_scr[...] = jnp.broadcast_to(jnp.sqrt(jnp.max(m, axis=0, keepdims=True)), kmax_scr.shape)

    qf = q_ref[...].astype(F32)
    zmax = (jnp.sqrt(jnp.sum(qf * qf, axis=1, keepdims=True)) * kmax_scr[0:1, 0:1]) * SB_NORM_SLACK

    for jr in range(nsub - 1, -1, -1):
        r0 = jr * tk
        k0 = pl.multiple_of((base + jr) * tk, tk)
        row = lax.broadcasted_iota(jnp.int32, (tq - r0, tk), 0)
        col = lax.broadcasted_iota(jnp.int32, (tq - r0, tk), 1)
        pv, lat = _sb_tile(q_ref[r0:, :], k_ref[pl.ds(k0, tk), :], v_ref[pl.ds(k0, tk), :], tri,
                           later[r0:], col < row)
        acc_scr[r0:, :] += pv
        later = lat if r0 == 0 else jnp.concatenate([later[:r0], lat], axis=0)

    def live(later):
        return (jnp.min(later - zmax) <= SB_ZERO_EXPONENT).astype(jnp.int32)

    def cond(state):
        s, _, go = state
        return (s < base) & (go > 0)

    def body(state):
        s, later, _ = state
        j0 = pl.multiple_of((base - 1 - s) * tk, tk)
        pv, later = _sb_tile(q_ref[...], k_ref[pl.ds(j0, tk), :], v_ref[pl.ds(j0, tk), :], tri,
                             later, None)
        acc_scr[...] += pv
        return s + 1, later, live(later)

    lax.while_loop(cond, body, (jnp.int32(0), later, live(later)))
    o_ref[...] = acc_scr[...].astype(o_ref.dtype)


ATTN_Q_ROWS = 1024
ATTN_K_ROWS = 256


def attention_prompt(q, kvb, *, n_heads, head_dim):
    t = q.shape[0]
    tk = _tile(t, ATTN_K_ROWS)
    tq = _tile(t, ATTN_Q_ROWS)
    assert tq % tk == 0
    single = pl.Buffered(1)
    return pl.pallas_call(
        functools.partial(_attn_prompt_body, tq=tq, tk=tk),
        grid=(n_heads, t // tq),
        in_specs=[pl.BlockSpec((tq, head_dim), lambda h, i: (i, h)),
                  pl.BlockSpec((t, head_dim), lambda h, i: (0, h), pipeline_mode=single),
                  pl.BlockSpec((t, head_dim), lambda h, i: (0, n_heads + h), pipeline_mode=single)],
        out_specs=pl.BlockSpec((tq, head_dim), lambda h, i: (i, h)),
        out_shape=jax.ShapeDtypeStruct((t, n_heads * head_dim), BF16),
        scratch_shapes=[pltpu.VMEM((tq, head_dim), F32), pltpu.VMEM((SUBLANES, 128), F32)],
        compiler_params=_cparams("arbitrary", "arbitrary"),
        name="attn_prompt",
    )(q, kvb, kvb)


def _attn_sample_body(q_ref, kn_ref, vn_ref, ck_hbm, cv_hbm, o_ref, kbuf, vbuf, sem, *, l, ds, tk):
    n_heads = pl.num_programs(1)
    step = pl.program_id(0) * n_heads + pl.program_id(1)
    slot = lax.rem(step, 2)

    def fetch(s, sl):
        b, h = s // n_heads, lax.rem(s, n_heads)
        return (pltpu.make_async_copy(ck_hbm.at[l, b, :, h, :], kbuf.at[sl], sem.at[0, sl]),
                pltpu.make_async_copy(cv_hbm.at[l, b, :, h, :], vbuf.at[sl], sem.at[1, sl]))

    @pl.when(step == 0)
    def _():
        for cp in fetch(0, 0):
            cp.start()

    @pl.when(step + 1 < pl.num_programs(0) * n_heads)
    def _():
        for cp in fetch(step + 1, 1 - slot):
            cp.start()

    for cp in fetch(step, slot):
        cp.wait()

    q = q_ref[...]
    row = lax.broadcasted_iota(jnp.int32, (ds, ds), 0)
    col = lax.broadcasted_iota(jnp.int32, (ds, ds), 1)
    acc, later = _sb_tile(q, kn_ref[...], vn_ref[...], _tri(ds), jnp.zeros((ds, 1), F32), col < row)
    tri = _tri(tk)
    for j in range(kbuf.shape[1] // tk - 1, -1, -1):
        kb = kbuf[slot, j * tk:(j + 1) * tk, :].astype(BF16)
        vb = vbuf[slot, j * tk:(j + 1) * tk, :].astype(BF16)
        pv, later = _sb_tile(q, kb, vb, tri, later, None)
        acc = acc + pv
    o_ref[...] = acc.astype(o_ref.dtype)


def attention_sample(q, kvb, cache_k, cache_v, l, *, ds):
    _, b, past, n_heads, head_dim = cache_k.shape
    tk = _tile(past, ATTN_K_ROWS)
    return pl.pallas_call(
        functools.partial(_attn_sample_body, l=l, ds=ds, tk=tk),
        grid=(b, n_heads),
        in_specs=[pl.BlockSpec((ds, head_dim), lambda bi, h: (bi, h)),
                  pl.BlockSpec((ds, head_dim), lambda bi, h: (bi, h)),
                  pl.BlockSpec((ds, head_dim), lambda bi, h: (bi, n_heads + h)),
                  pl.BlockSpec(memory_space=pl.ANY), pl.BlockSpec(memory_space=pl.ANY)],
        out_specs=pl.BlockSpec((ds, head_dim), lambda bi, h: (bi, h)),
        out_shape=jax.ShapeDtypeStruct((b * ds, n_heads * head_dim), BF16),
        scratch_shapes=[pltpu.VMEM((2, past, head_dim), F32), pltpu.VMEM((2, past, head_dim), F32),
                        pltpu.SemaphoreType.DMA((2, 2))],
        compiler_params=_cparams("arbitrary", "arbitrary"),
        name="attn_sample",
    )(q, kvb, kvb, cache_k, cache_v)


def _stack_heads_body(*refs, n_heads, head_dim):
    *in_refs, o_ref = refs
    l = pl.program_id(0)
    for i, x_ref in enumerate(in_refs):
        @pl.when(l == i)
        def _(x_ref=x_ref):
            for h in range(n_heads):
                o_ref[:, h, :] = x_ref[:, h * head_dim:(h + 1) * head_dim]


def stack_heads(xs, batch, *, n_heads, head_dim):
    depth = len(xs)
    t = xs[0].shape[0]
    tm = _tile(t, 512)
    nm = t // tm
    in_specs = [pl.BlockSpec((tm, n_heads * head_dim),
                             lambda l, m, i=i: (jnp.clip(m + (l - i) * nm, 0, nm - 1), 0))
                for i in range(depth)]
    out = pl.pallas_call(
        functools.partial(_stack_heads_body, n_heads=n_heads, head_dim=head_dim),
        grid=(depth, nm),
        in_specs=in_specs,
        out_specs=pl.BlockSpec((None, None, tm, n_heads, head_dim), lambda l, m: (l, 0, m, 0, 0)),
        out_shape=jax.ShapeDtypeStruct((depth, 1, t, n_heads, head_dim), F32),
        compiler_params=_cparams("arbitrary", "arbitrary"),
        name="stack_heads",
    )(*xs)
    return out.reshape(depth, batch, t // batch, n_heads, head_dim)


def _ssm_body(u_ref, bdr_ref, bdi_ref, cdr_ref, cdi_ref, ar_ref, ai_ref, d_ref, h0r_ref, h0i_ref,
              *rest, n, nkb, emit_y):
    if emit_y:
        y_ref, hr_ref, hi_ref, str_scr, sti_scr, xr_scr, xi_scr, rows_scr = rest
    else:
        hr_ref, hi_ref, str_scr, sti_scr, xr_scr, xi_scr, rows_scr = rest
    step = pl.program_id(0)
    kin = bdr_ref.shape[1]
    kst = bdr_ref.shape[2]

    @pl.when(step == 0)
    def _():
        str_scr[...] = h0r_ref[...]
        sti_scr[...] = h0i_ref[...]

    d_ssm = nkb * kin
    for r in range(SUBLANES):
        rows_scr[:, r, :] = u_ref[:, r * d_ssm:(r + 1) * d_ssm]
    u = rows_scr[...].reshape(n * SUBLANES, d_ssm)
    ub = u.astype(BF16)
    for kb in range(nkb):
        ukb = ub[:, kb * kin:(kb + 1) * kin]
        xr_scr[:, kb * kst:(kb + 1) * kst] = jnp.dot(ukb, bdr_ref[kb], preferred_element_type=F32)
        xi_scr[:, kb * kst:(kb + 1) * kst] = jnp.dot(ukb, bdi_ref[kb], preferred_element_type=F32)

    for kb in range(nkb):
        cs = slice(kb * kst, (kb + 1) * kst)
        ar = jnp.broadcast_to(ar_ref[:, cs], (SUBLANES, kst))
        ai = jnp.broadcast_to(ai_ref[:, cs], (SUBLANES, kst))

        def scan_step(tt, carry, cs=cs, ar=ar, ai=ai):
            xr, xi = carry
            r0 = pl.multiple_of(tt * SUBLANES, SUBLANES)
            nxr = ar * xr - ai * xi + xr_scr[pl.ds(r0, SUBLANES), cs]
            nxi = ar * xi + ai * xr + xi_scr[pl.ds(r0, SUBLANES), cs]
            xr_scr[pl.ds(r0, SUBLANES), cs] = nxr
            xi_scr[pl.ds(r0, SUBLANES), cs] = nxi
            return nxr, nxi

        xr, xi = lax.fori_loop(0, n, scan_step, (str_scr[:, cs], sti_scr[:, cs]), unroll=4)
        str_scr[:, cs] = xr
        sti_scr[:, cs] = xi

    if emit_y:
        ys = []
        for kb in range(nkb):
            cs = slice(kb * kst, (kb + 1) * kst)
            yk = (jnp.dot(xr_scr[:, cs].astype(BF16), cdr_ref[kb], preferred_element_type=F32)
                  - jnp.dot(xi_scr[:, cs].astype(BF16), cdi_ref[kb], preferred_element_type=F32))
            os_ = slice(kb * kin, (kb + 1) * kin)
            ys.append(yk + d_ref[:, os_] * u[:, os_])
        rows_scr[...] = jnp.concatenate(ys, axis=1).reshape(n, SUBLANES, d_ssm)
        for r in range(SUBLANES):
            y_ref[:, r * d_ssm:(r + 1) * d_ssm] = rows_scr[:, r, :]

    @pl.when(step == pl.num_programs(0) - 1)
    def _():
        hr_ref[...] = str_scr[...]
        hi_ref[...] = sti_scr[...]


def ssm_scan(u, coef, h0r, h0i, *, emit_y):
    bdr, bdi, cdr, cdi, ar, ai, dsk = coef
    steps, width = u.shape
    d_ssm = width // SUBLANES
    nkb, kin, kst = bdr.shape
    npst = nkb * kst
    n = math.gcd(steps, SSM_STEPS_PER_BLOCK)
    rows = n * SUBLANES
    full = lambda a: pl.BlockSpec(a.shape, lambda s: (0,) * a.ndim)
    out_specs = [pl.BlockSpec((SUBLANES, npst), lambda s: (0, 0))] * 2
    out_shape = [jax.ShapeDtypeStruct((SUBLANES, npst), F32)] * 2
    if emit_y:
        out_specs = [pl.BlockSpec((n, width), lambda s: (s, 0))] + out_specs
        out_shape = [jax.ShapeDtypeStruct((steps, width), F32)] + out_shape
    outs = pl.pallas_call(
        functools.partial(_ssm_body, n=n, nkb=nkb, emit_y=emit_y),
        grid=(steps // n,),
        in_specs=[pl.BlockSpec((n, width), lambda s: (s, 0)),
                  full(bdr), full(bdi), full(cdr), full(cdi), full(ar), full(ai), full(dsk),
                  full(h0r), full(h0i)],
        out_specs=out_specs,
        out_shape=out_shape,
        scratch_shapes=[pltpu.VMEM((SUBLANES, npst), F32), pltpu.VMEM((SUBLANES, npst), F32),
                        pltpu.VMEM((rows, npst), F32), pltpu.VMEM((rows, npst), F32),
                        pltpu.VMEM((n, SUBLANES, d_ssm), F32)],
        compiler_params=_cparams("arbitrary"),
        name="ssm_scan_y" if emit_y else "ssm_scan_state",
    )(u, bdr, bdi, cdr, cdi, ar, ai, dsk, h0r, h0i)
    if emit_y:
        return outs[0], outs[1], outs[2]
    return None, outs[0], outs[1]


def _ssm_coef(lam_re, lam_im, log_dt, b_re, b_im, c_re, c_im, d_skip):
    g, p = lam_re.shape
    dt = jnp.exp(log_dt.astype(F32))[:, None]
    lr, li = lam_re.astype(F32), lam_im.astype(F32)
    mag = jnp.exp(lr * dt)
    ar = mag * jnp.cos(li * dt)
    ai = mag * jnp.sin(li * dt)
    den = lr * lr + li * li
    nr = ar - 1.0
    cr = (nr * lr + ai * li) / den
    ci = (ai * lr - nr * li) / den
    br, bi = b_re.astype(F32), b_im.astype(F32)
    bbr = cr[..., None] * br - ci[..., None] * bi
    bbi = cr[..., None] * bi + ci[..., None] * br
    gb = min(SSM_KB_GROUPS, g)
    nkb = g // gb
    eye = jnp.eye(gb, dtype=F32)

    def pack_b(m):
        m = m.reshape(nkb, gb, p, GROUP)
        return jnp.einsum('kaph,ab->kahbp', m, eye).reshape(nkb, gb * GROUP, gb * p).astype(BF16)

    def pack_c(m):
        m = m.reshape(nkb, gb, GROUP, p)
        return jnp.einsum('kahp,ab->kapbh', m, eye).reshape(nkb, gb * p, gb * GROUP).astype(BF16)

    coef = (pack_b(bbr), pack_b(bbi), pack_c(c_re.astype(F32)), pack_c(c_im.astype(F32)),
            ar.reshape(1, g * p), ai.reshape(1, g * p), d_skip.astype(F32).reshape(1, g * GROUP))
    return coef, ar, ai


def _segment_inits(ar, ai, fr, fi, seg_len):
    k = int(round(math.log2(seg_len)))
    assert 2 ** k == seg_len
    pr, pi = ar.reshape(1, -1), ai.reshape(1, -1)
    for _ in range(k):
        pr, pi = pr * pr - pi * pi, 2.0 * pr * pi
    hr = jnp.zeros_like(fr[0:1])
    hi = jnp.zeros_like(fi[0:1])
    hrs, his = [hr], [hi]
    for r in range(SUBLANES - 1):
        hr, hi = pr * hr - pi * hi + fr[r:r + 1], pr * hi + pi * hr + fi[r:r + 1]
        hrs.append(hr)
        his.append(hi)
    return jnp.concatenate(hrs, axis=0), jnp.concatenate(his, axis=0)


def _merge_body(oa_ref, y_ref, ga_ref, gb_ref, ba_ref, bb_ref, wglu_ref, wpa_ref, wpb_ref, o_ref,
                ob_scr):
    n = pl.program_id(1)

    @pl.when(n == 0)
    def _():
        gy = jax.nn.gelu(y_ref[...])
        glu = jnp.dot(gy.astype(BF16), wglu_ref[...], preferred_element_type=F32)
        ob_scr[...] = (gy * jax.nn.sigmoid(glu)).astype(BF16)

    pa = jnp.dot(oa_ref[...], wpa_ref[...], preferred_element_type=F32)
    pb = jnp.dot(ob_scr[...], wpb_ref[...], preferred_element_type=F32)
    gate_a = jax.nn.sigmoid(ga_ref[...].astype(F32) + ba_ref[...])
    gate_b = jax.nn.sigmoid(gb_ref[...].astype(F32) + bb_ref[...])
    o_ref[...] = (gate_a * pa + gate_b * pb).astype(o_ref.dtype)


def mixer_merge(o_a, y, gates, b_gate, w_glu, w_pa, w_pb, l, *, y_permuted):
    t, d_attn = o_a.shape
    d_ssm = w_glu.shape[1]
    d = w_pa.shape[2]
    tm = _tile(t // SUBLANES if y_permuted else t, 1024)
    tn = _tile(d, 1024)
    nd = d // tn
    if y_permuted:
        seg = t // SUBLANES
        mps = seg // tm
        assert y.shape == (seg, SUBLANES * d_ssm)
        y_spec = pl.BlockSpec((tm, d_ssm), lambda m, n: (m % mps, m // mps))
    else:
        y_spec = pl.BlockSpec((tm, d_ssm), lambda m, n: (m, 0))
    bg = b_gate.reshape(1, 2 * d)
    return pl.pallas_call(
        _merge_body,
        grid=(t // tm, nd),
        in_specs=[pl.BlockSpec((tm, d_attn), lambda m, n: (m, 0)),
                  y_spec,
                  pl.BlockSpec((tm, tn), lambda m, n: (m, n)),
                  pl.BlockSpec((tm, tn), lambda m, n: (m, nd + n)),
                  pl.BlockSpec((1, tn), lambda m, n: (0, n)),
                  pl.BlockSpec((1, tn), lambda m, n: (0, nd + n)),
                  pl.BlockSpec((None, d_ssm, d_ssm), lambda m, n: (l, 0, 0)),
                  pl.BlockSpec((None, d_attn, tn), lambda m, n: (l, 0, n)),
                  pl.BlockSpec((None, d_ssm, tn), lambda m, n: (l, 0, n))],
        out_specs=pl.BlockSpec((tm, tn), lambda m, n: (m, n)),
        out_shape=jax.ShapeDtypeStruct((t, d), BF16),
        scratch_shapes=[pltpu.VMEM((tm, d_ssm), BF16)],
        compiler_params=_cparams("arbitrary", "arbitrary"),
        name="mixer_merge",
    )(o_a, y, gates, gates, bg, bg, w_glu, w_pa, w_pb)


MATMUL_W_TILE_BYTES = 8 * 1024 * 1024


def _matmul_res_body(a_ref, w_ref, r_ref, o_ref):
    o_ref[...] = r_ref[...] + jnp.dot(a_ref[...], w_ref[...], preferred_element_type=F32)


def matmul_res(a, w, wi, res):
    t, k = a.shape
    n_out = w.shape[2]
    tm = _tile(t, 1024)
    tn = _tile(n_out, MATMUL_W_TILE_BYTES // (2 * k))
    return pl.pallas_call(
        _matmul_res_body,
        grid=(t // tm, n_out // tn),
        in_specs=[pl.BlockSpec((tm, k), lambda m, n: (m, 0)),
                  pl.BlockSpec((None, k, tn), lambda m, n: (wi, 0, n)),
                  pl.BlockSpec((tm, tn), lambda m, n: (m, n))],
        out_specs=pl.BlockSpec((tm, tn), lambda m, n: (m, n)),
        out_shape=jax.ShapeDtypeStruct((t, n_out), F32),
        compiler_params=_cparams("arbitrary", "arbitrary"),
        name="matmul_res",
    )(a, w, res)


def _router_top2(h, router_ref, rb_ref, n_experts):
    logits = jnp.dot(h, router_ref[...], preferred_element_type=F32) + rb_ref[...]
    lane = lax.broadcasted_iota(jnp.int32, logits.shape, 1).astype(F32)
    neg = jnp.float32(-jnp.inf)
    logits = jnp.where(lane < n_experts, logits, neg)
    big = jnp.float32(logits.shape[1])
    m1 = jnp.max(logits, axis=-1, keepdims=True)
    i1 = jnp.min(jnp.where(logits == m1, lane, big), axis=-1, keepdims=True)
    rest = jnp.where(lane == i1, neg, logits)
    m2 = jnp.max(rest, axis=-1, keepdims=True)
    i2 = jnp.min(jnp.where(rest == m2, lane, big), axis=-1, keepdims=True)
    e2 = jnp.exp(m2 - m1)
    g1 = 1.0 / (1.0 + e2)
    g2 = e2 / (1.0 + e2)
    return (jnp.where(lane == 0.0, i1, 0.0) + jnp.where(lane == 1.0, i2, 0.0)
            + jnp.where(lane == 2.0, g1, 0.0) + jnp.where(lane == 3.0, g2, 0.0))


def _ffn_up_body(x_hbm, g_ref, w1_ref, w3_ref, o_ref, h_scr, xbuf, sem):
    _load_norm_rows(x_hbm, g_ref, xbuf, sem, h_scr)
    h = h_scr[...]
    a = jnp.dot(h, w1_ref[...], preferred_element_type=F32)
    b = jnp.dot(h, w3_ref[...], preferred_element_type=F32)
    o_ref[...] = (jax.nn.silu(a) * b).astype(o_ref.dtype)


def ffn_up(x, g, w1, w3, wi):
    t, d = x.shape
    f = w1.shape[2]
    tm = _tile(t, 1024)
    tn = _tile(f, 512)
    assert f // tn >= 2
    w_spec = pl.BlockSpec((None, d, tn), lambda m, n: (wi, 0, n))
    return pl.pallas_call(
        _ffn_up_body,
        grid=(t // tm, f // tn),
        in_specs=[pl.BlockSpec(memory_space=pl.ANY),
                  pl.BlockSpec((1, d), lambda m, n: (0, 0)), w_spec, w_spec],
        out_specs=pl.BlockSpec((tm, tn), lambda m, n: (m, n)),
        out_shape=jax.ShapeDtypeStruct((t, f), BF16),
        scratch_shapes=[pltpu.VMEM((tm, d), BF16), pltpu.VMEM((tm, d), F32), pltpu.SemaphoreType.DMA(())],
        compiler_params=_cparams("arbitrary", "arbitrary"),
        name="ffn_up",
    )(x, g.reshape(1, d), w1, w3)


MOE_ROWS = 256
ROUTE_LANES = 128


def _moe_route_body(x_ref, g_ref, router_ref, rb_ref, top_ref, h_scr, *, n_experts):
    _rmsnorm_rows(x_ref, g_ref, h_scr)
    top_ref[...] = _router_top2(h_scr[...], router_ref, rb_ref, n_experts)


def moe_route(x, g, router, router_b):
    t, d = x.shape
    n_experts = router.shape[1]
    tm = _tile(t, 512)
    rpad = jnp.zeros((d, ROUTE_LANES), BF16).at[:, :n_experts].set(router.astype(BF16))
    bpad = jnp.zeros((1, ROUTE_LANES), F32).at[0, :n_experts].set(router_b.astype(F32))
    return pl.pallas_call(
        functools.partial(_moe_route_body, n_experts=n_experts),
        grid=(t // tm,),
        in_specs=[pl.BlockSpec((tm, d), lambda m: (m, 0)), pl.BlockSpec((1, d), lambda m: (0, 0)),
                  pl.BlockSpec((d, ROUTE_LANES), lambda m: (0, 0)),
                  pl.BlockSpec((1, ROUTE_LANES), lambda m: (0, 0))],
        out_specs=pl.BlockSpec((tm, ROUTE_LANES), lambda m: (m, 0)),
        out_shape=jax.ShapeDtypeStruct((t, ROUTE_LANES), F32),
        scratch_shapes=[pltpu.VMEM((tm, d), BF16)],
        compiler_params=_cparams("arbitrary"),
        name="moe_route",
    )(x, g.reshape(1, d), rpad, bpad)


def _moe_plan(top, n_experts):
    t = top.shape[0]
    n_slots = TOP_K * t
    n_tiles = (n_slots + n_experts * (MOE_ROWS - 1)) // MOE_ROWS
    e = top[:, :TOP_K].astype(jnp.int32).reshape(n_slots)
    gate = top[:, TOP_K:2 * TOP_K].reshape(n_slots)
    slot = jnp.arange(n_slots, dtype=jnp.int32)
    order = jnp.argsort(e * n_slots + slot).astype(jnp.int32)
    counts = jnp.sum((e[:, None] == jnp.arange(n_experts)[None, :]).astype(jnp.int32), axis=0)
    padded = ((counts + MOE_ROWS - 1) // MOE_ROWS) * MOE_ROWS
    pend = jnp.cumsum(padded)
    pstart = pend - padded
    cstart = jnp.cumsum(counts) - counts
    row = jnp.arange(n_tiles * MOE_ROWS, dtype=jnp.int32)
    row_e = jnp.minimum(jnp.sum((row[:, None] >= pend[None, :]).astype(jnp.int32), axis=1), n_experts - 1)
    rank = row - pstart[row_e]
    real = rank < counts[row_e]
    row_slot = order[jnp.clip(cstart[row_e] + rank, 0, n_slots - 1)]
    row_token = jnp.where(real, row_slot // TOP_K, 0).astype(jnp.int32)
    row_dst = jnp.where(real, (row_slot % TOP_K) * t + row_slot // TOP_K, 0).astype(jnp.int32)
    row_gate = jnp.where(real, gate[row_slot], 0.0).reshape(-1, 1)
    tile0 = jnp.arange(n_tiles, dtype=jnp.int32) * MOE_ROWS
    tile_e = row_e[tile0]
    tile_rows = jnp.clip(pstart[tile_e] + counts[tile_e] - tile0, 0, MOE_ROWS)
    tile_rows = jnp.where(tile0 < pend[-1], tile_rows, 0).astype(jnp.int32)
    return tile_e, tile_rows, row_token, row_dst, row_gate


def _moe_expert_body(te_ref, tr_ref, tok_ref, dst_ref, x_hbm, g_ref, gate_ref, w1_ref, w3_ref, w2_ref,
                     y_hbm, xbuf, ybuf, gsem, ssem):
    rows = xbuf.shape[1]
    j = pl.program_id(0)
    last = pl.num_programs(0) - 1
    slot = lax.rem(j, 2)

    def gather_row(tile, s, i, priority):
        pltpu.make_async_copy(x_hbm.at[pl.ds(tok_ref[tile * rows + i], 1), :],
                              xbuf.at[s, pl.ds(i, 1), :], gsem.at[s]).start(priority=priority)

    def scatter_row(tile, s, i, priority):
        pltpu.make_async_copy(ybuf.at[s, pl.ds(i, 1), :],
                              y_hbm.at[pl.ds(dst_ref[tile * rows + i], 1), :],
                              ssem.at[s]).start(priority=priority)

    def gather_start(tile, s):
        def pair(p, c):
            gather_row(tile, s, 2 * p, 0)
            gather_row(tile, s, 2 * p + 1, 1)
            return c
        lax.fori_loop(0, rows // 2, pair, 0, unroll=4)

    def scatter_start(tile, s, n):
        def pair(p, c):
            scatter_row(tile, s, 2 * p, 0)
            scatter_row(tile, s, 2 * p + 1, 1)
            return c
        lax.fori_loop(0, n // 2, pair, 0)

        @pl.when(lax.rem(n, 2) == 1)
        def _():
            scatter_row(tile, s, n - 1, 0)

    def scatter_wait(s, n):
        bit = rows
        while bit >= 1:
            @pl.when((n & bit) != 0)
            def _(bit=bit):
                pltpu.make_async_copy(ybuf.at[s, pl.ds(0, bit), :], y_hbm.at[pl.ds(0, bit), :],
                                      ssem.at[s]).wait()
            bit //= 2

    @pl.when((j == 0) & (tr_ref[0] > 0))
    def _():
        gather_start(0, 0)

    nxt = jnp.minimum(j + 1, last)

    @pl.when((j < last) & (tr_ref[nxt] > 0))
    def _():
        gather_start(nxt, 1 - slot)

    @pl.when(tr_ref[j] > 0)
    def _():
        pltpu.make_async_copy(x_hbm.at[pl.ds(0, rows), :], xbuf.at[slot], gsem.at[slot]).wait()
        h = _rmsnorm(xbuf[slot], g_ref[...]).astype(BF16)
        a = jnp.dot(h, w1_ref[...], preferred_element_type=F32)
        b = jnp.dot(h, w3_ref[...], preferred_element_type=F32)
        hid = (jax.nn.silu(a) * b).astype(BF16)
        ybuf[slot] = (jnp.dot(hid, w2_ref[...], preferred_element_type=F32) * gate_ref[...]
                      + 0.5 * xbuf[slot])
        scatter_start(j, slot, tr_ref[j])

    prv = jnp.maximum(j - 1, 0)

    @pl.when(j >= 1)
    def _():
        scatter_wait(1 - slot, tr_ref[prv])

    @pl.when(j == last)
    def _():
        scatter_wait(slot, tr_ref[j])


def moe_experts(x, g, plan, w1, w3, w2, e0, wi2):
    tile_e, tile_rows, row_token, row_dst, row_gate = plan
    t, d = x.shape
    f = w1.shape[2]
    n_tiles = tile_e.shape[0]
    once = pl.Buffered(1)
    grid_spec = pltpu.PrefetchScalarGridSpec(
        num_scalar_prefetch=4,
        grid=(n_tiles,),
        in_specs=[pl.BlockSpec(memory_space=pl.ANY),
                  pl.BlockSpec((1, d), lambda j, te, tr, tok, dst: (0, 0)),
                  pl.BlockSpec((MOE_ROWS, 1), lambda j, te, tr, tok, dst: (j, 0)),
                  pl.BlockSpec((None, d, f), lambda j, te, tr, tok, dst: (e0 + te[j], 0, 0),
                               pipeline_mode=once),
                  pl.BlockSpec((None, d, f), lambda j, te, tr, tok, dst: (e0 + te[j], 0, 0),
                               pipeline_mode=once),
                  pl.BlockSpec((None, f, d), lambda j, te, tr, tok, dst: (wi2, te[j], 0),
                               pipeline_mode=once)],
        out_specs=pl.BlockSpec(memory_space=pl.ANY),
        scratch_shapes=[pltpu.VMEM((2, MOE_ROWS, d), F32), pltpu.VMEM((2, MOE_ROWS, d), F32),
                        pltpu.SemaphoreType.DMA((2,)), pltpu.SemaphoreType.DMA((2,))])
    return pl.pallas_call(
        _moe_expert_body,
        grid_spec=grid_spec,
        out_shape=jax.ShapeDtypeStruct((TOP_K * t, d), F32),
        compiler_params=_cparams("arbitrary"),
        name="moe_experts",
    )(tile_e, tile_rows, row_token, row_dst, x, g.reshape(1, d), row_gate, w1, w3, w2)


def _moe_combine_body(ya_ref, yb_ref, *rest, norm):
    if norm:
        g_ref, o_ref = rest
    else:
        (o_ref,) = rest
    o_ref[...] = ya_ref[...] + yb_ref[...]
    if norm:
        _rmsnorm_rows(o_ref, g_ref, o_ref)


def moe_combine(y2, g=None):
    t, d = y2.shape[0] // TOP_K, y2.shape[1]
    tm = _tile(t, 512)
    y3 = y2.reshape(TOP_K, t, d)
    in_specs = [pl.BlockSpec((None, tm, d), lambda m: (0, m, 0)),
                pl.BlockSpec((None, tm, d), lambda m: (1, m, 0))]
    args = [y3, y3]
    if g is not None:
        in_specs.append(pl.BlockSpec((1, d), lambda m: (0, 0)))
        args.append(g.reshape(1, d))
    return pl.pallas_call(
        functools.partial(_moe_combine_body, norm=g is not None),
        grid=(t // tm,),
        in_specs=in_specs,
        out_specs=pl.BlockSpec((tm, d), lambda m: (m, 0)),
        out_shape=jax.ShapeDtypeStruct((t, d), F32),
        compiler_params=_cparams("arbitrary"),
        name="moe_combine",
    )(*args)


def _final_norm_body(x_ref, g_ref, o_ref):
    _rmsnorm_rows(x_ref, g_ref, o_ref)


def final_norm(x, g):
    t, d = x.shape
    tm = _tile(t, 512)
    return pl.pallas_call(
        _final_norm_body,
        grid=(t // tm,),
        in_specs=[pl.BlockSpec((tm, d), lambda m: (m, 0)), pl.BlockSpec((1, d), lambda m: (0, 0))],
        out_specs=pl.BlockSpec((tm, d), lambda m: (m, 0)),
        out_shape=jax.ShapeDtypeStruct((t, d), F32),
        compiler_params=_cparams("arbitrary"),
        name="final_norm",
    )(x, g.reshape(1, d))


def kernel(x_prompt, x_sample, cache_k, cache_v, state_ssm_re, state_ssm_im, norm1_g, norm2_g, w_in,
           lam_re, lam_im, log_dt, ssm_b_re, ssm_b_im, ssm_c_re, ssm_c_im, ssm_d, w_glu, w_pa, w_pb,
           b_gate, w_o, ffn_w1, ffn_w3, ffn_w2, moe_router, moe_router_b, moe_w1, moe_w3, moe_w2,
           final_g):
    bp, seq, d = x_prompt.shape
    db, ds, _ = x_sample.shape
    depth = w_in.shape[0]
    _, _, past, n_heads, head_dim = cache_k.shape
    d_attn = n_heads * head_dim
    n_groups, n_state = lam_re.shape[1:]
    d_ssm = n_groups * GROUP
    npst = n_groups * n_state
    assert bp == 1 and db == SUBLANES and seq % SUBLANES == 0

    xp = x_prompt.reshape(seq, d)
    xs = x_sample.reshape(db * ds, d)
    w_in_b, w_glu_b, w_pa_b, w_pb_b, w_o_b = (w.astype(BF16) for w in (w_in, w_glu, w_pa, w_pb, w_o))
    ffn_w1_b, ffn_w3_b, ffn_w2_b = (w.astype(BF16) for w in (ffn_w1, ffn_w3, ffn_w2))
    n_moe, n_exp, _, f_e = moe_w1.shape
    moe_w1_b = moe_w1.astype(BF16).reshape(n_moe * n_exp, d, f_e)
    moe_w3_b = moe_w3.astype(BF16).reshape(n_moe * n_exp, d, f_e)
    moe_w2_b = moe_w2.astype(BF16).reshape(n_moe, n_exp * f_e, d)
    outs = [[] for _ in range(8)]
    for l in range(depth):
        i = l // 2
        coef, ar, ai = _ssm_coef(lam_re[l], lam_im[l], log_dt[l], ssm_b_re[l], ssm_b_im[l],
                                 ssm_c_re[l], ssm_c_im[l], ssm_d[l])

        end_gain = final_g if l == depth - 1 else None

        def channel_mix(x):
            if l % 2 == 0:
                hid = ffn_up(x, norm2_g[l], ffn_w1_b, ffn_w3_b, i)
                x = matmul_res(hid, ffn_w2_b, i, x)
                return x if end_gain is None else final_norm(x, end_gain)
            plan = _moe_plan(moe_route(x, norm2_g[l], moe_router[i], moe_router_b[i]), n_exp)
            y2 = moe_experts(x, norm2_g[l], plan, moe_w1_b, moe_w3_b, moe_w2_b, i * n_exp, i)
            return moe_combine(y2, end_gain)

        q, k, v, kvb, u, gates = norm_proj(xp, norm1_g[l], w_in_b, l, d_attn=d_attn, d_ssm=d_ssm,
                                           head_dim=head_dim, permute_u=True)
        o_a = attention_prompt(q, kvb, n_heads=n_heads, head_dim=head_dim)
        zeros = jnp.zeros((SUBLANES, npst), F32)
        _, fr, fi = ssm_scan(u, coef, zeros, zeros, emit_y=False)
        h0r, h0i = _segment_inits(ar, ai, fr, fi, seq // SUBLANES)
        y, hr, hi = ssm_scan(u, coef, h0r, h0i, emit_y=True)
        merged = mixer_merge(o_a, y, gates, b_gate[l], w_glu_b, w_pa_b, w_pb_b, l, y_permuted=True)
        xp = matmul_res(merged, w_o_b, l, xp)
        xp = channel_mix(xp)
        outs[0].append(k)
        outs[1].append(v)
        outs[2].append(hr[SUBLANES - 1].reshape(1, n_groups, n_state))
        outs[3].append(hi[SUBLANES - 1].reshape(1, n_groups, n_state))

        q, k, v, kvb, u, gates = norm_proj(xs, norm1_g[l], w_in_b, l, d_attn=d_attn, d_ssm=d_ssm,
                                           head_dim=head_dim, permute_u=False)
        o_a = attention_sample(q, kvb, cache_k, cache_v, l, ds=ds)
        u_t = u.reshape(db, ds, d_ssm).transpose(1, 0, 2).reshape(ds, db * d_ssm)
        y_t, hr, hi = ssm_scan(u_t, coef, state_ssm_re[l].astype(F32).reshape(db, npst),
                               state_ssm_im[l].astype(F32).reshape(db, npst), emit_y=True)
        y = y_t.reshape(ds, db, d_ssm).transpose(1, 0, 2).reshape(db * ds, d_ssm)
        merged = mixer_merge(o_a, y, gates, b_gate[l], w_glu_b, w_pa_b, w_pb_b, l, y_permuted=False)
        xs = matmul_res(merged, w_o_b, l, xs)
        xs = channel_mix(xs)
        outs[4].append(k)
        outs[5].append(v)
        outs[6].append(hr.reshape(db, n_groups, n_state))
        outs[7].append(hi.reshape(db, n_groups, n_state))

    y_prompt = xp.reshape(bp, seq, d)
    y_sample = xs.reshape(db, ds, d)
    heads = dict(n_heads=n_heads, head_dim=head_dim)
    kv_outs = {0: stack_heads(outs[0], bp, **heads), 1: stack_heads(outs[1], bp, **heads),
               4: stack_heads(outs[4], db, **heads), 5: stack_heads(outs[5], db, **heads)}
    return (y_prompt, y_sample) + tuple(kv_outs[j] if j in kv_outs else jnp.stack(outs[j])
                                        for j in range(8))
```
